```python
import math
import jax, jax.numpy as jnp
from jax import lax
import numpy as np

D_MODEL = 1024
BATCH = 8
SEQ = 2048
DEPTH = 1
DEC_BATCH = 128
DEC_SEQ = 8
PAST_LEN = 16384
PAGE_SIZE = 128

DIFF_HEADS = 4
DIFF_KV_HEADS = 2
DIFF_GROUP = DIFF_HEADS // DIFF_KV_HEADS
DIFF_HEAD_DIM = 64
DIFF_V_DIM = 2 * DIFF_HEAD_DIM
MLA_HEADS = 8
MLA_Q_LORA = 256
MLA_KV_LORA = 128
MLA_NOPE = 64
MLA_ROPE = 32
MLA_V = 64
N_MEM = 256
MEM_HEADS = 4
MEM_HEAD_DIM = 128
N_BRANCH = 3
N_EXPERTS = 32
TOP_K = 4
D_EXPERT = 1024
SWIGLU_ALPHA = 1.702
SWIGLU_LIMIT = 7.0
MOE_BLOCK = 128

ROPE_THETA = 10000.0
NORM_EPS = 1e-6
SUBLN_EPS = 1e-5
Q_BLOCK = 128
NEG_INF = -1e30
F32 = jnp.float32

DIFF_SCALE = DIFF_HEAD_DIM ** -0.5
MLA_SCALE = (MLA_NOPE + MLA_ROPE) ** -0.5
MEM_SCALE = MEM_HEAD_DIM ** -0.5

DIFF_Q_W = DIFF_HEADS * 2 * DIFF_HEAD_DIM
DIFF_K_W = DIFF_KV_HEADS * 2 * DIFF_HEAD_DIM
DIFF_V_W = DIFF_KV_HEADS * DIFF_V_DIM
MEM_Q_W = MEM_HEADS * MEM_HEAD_DIM
IN_SPLITS = (DIFF_Q_W, DIFF_K_W, DIFF_V_W, MLA_Q_LORA, MLA_KV_LORA, MLA_ROPE, MEM_Q_W, N_BRANCH * D_MODEL)
IN_WIDTH = sum(IN_SPLITS)
BRANCH_WIDTHS = (DIFF_HEADS * DIFF_V_DIM, MLA_HEADS * MLA_V, MEM_HEADS * MEM_HEAD_DIM)
MIX_WIDTH = sum(BRANCH_WIDTHS)

kernel_name = 'hybrid_diffattn_mla_memxattn_moe_step'


def split_last(z, sizes):
    out, off = [], 0
    for s in sizes:
        out.append(z[..., off:off + s])
        off += s
    return out


def rms_norm(x, gain, eps=NORM_EPS):
    xf = x.astype(F32)
    y = xf * lax.rsqrt(jnp.mean(xf * xf, axis=-1, keepdims=True) + eps)
    return (y * gain.astype(F32)).astype(x.dtype)


def rope(x, pos):
    d = x.shape[-1]
    half = d // 2
    inv_freq = jnp.power(ROPE_THETA, -jnp.arange(half, dtype=F32) / half)
    ang = pos.astype(F32)[:, None] * inv_freq[None, :]
    shape = (1, pos.shape[0]) + (1,) * (x.ndim - 3) + (half,)
    cos = jnp.cos(ang).reshape(shape)
    sin = jnp.sin(ang).reshape(shape)
    xf = x.astype(F32)
    x1, x2 = xf[..., :half], xf[..., half:]
    return jnp.concatenate([x1 * cos - x2 * sin, x2 * cos + x1 * sin], axis=-1).astype(x.dtype)


def tied_gain(g):
    return jnp.concatenate([g[:MLA_NOPE], g[MLA_NOPE:], g[MLA_NOPE:]])


def causal_mask(n):
    return jnp.arange(n)[:, None] >= jnp.arange(n)[None, :]


def online_step(carry, s, values, pv):
    m, l, acc = carry
    m_new = jnp.maximum(m, jnp.max(s, axis=-1))
    alpha = jnp.exp(m - m_new)
    p = jnp.exp(s - m_new[..., None])
    return (m_new, l * alpha + jnp.sum(p, axis=-1),
            acc * alpha[..., None] + pv(p, values.astype(F32)))


def mixer_inputs(x, pos, lp):
    B, T, _ = x.shape
    h = rms_norm(x, lp['norm_mix'])
    z = h @ lp['w_in']
    zq, zk, zv, zcq, zckv, zkr, zmq, zg = split_last(z, IN_SPLITS)
    dq = zq.reshape(B, T, DIFF_KV_HEADS, DIFF_GROUP, 2, DIFF_HEAD_DIM)
    dq = rope(rms_norm(dq, lp['diff_q_norm']), pos)
    dk = zk.reshape(B, T, DIFF_KV_HEADS, 2, DIFF_HEAD_DIM)
    dk = rope(rms_norm(dk, lp['diff_k_norm']), pos)
    dv = zv.reshape(B, T, DIFF_KV_HEADS, DIFF_V_DIM)
    c_q = rms_norm(zcq, lp['mla_q_a_norm'])
    mq = jnp.einsum('btr,rhd->bthd', c_q, lp['w_mla_uq'])
    mq = rms_norm(mq, tied_gain(lp['mla_q_norm']))
    mq = jnp.concatenate([mq[..., :MLA_NOPE], rope(mq[..., MLA_NOPE:], pos)], axis=-1)
    ckv = rms_norm(zckv, lp['mla_kv_a_norm'])
    krope = rope(zkr, pos)
    memq = rms_norm(zmq.reshape(B, T, MEM_HEADS, MEM_HEAD_DIM), lp['mem_q_norm'])
    gates = jax.nn.sigmoid((zg + lp['b_gate']).astype(F32)).astype(x.dtype)
    gates = gates.reshape(B, T, N_BRANCH, D_MODEL)
    return dq, dk, dv, mq, ckv, krope, memq, gates


def diff_scores(q, k):
    return jnp.einsum('bqngcd,bsncd->bngcqs', q, k).astype(F32) * DIFF_SCALE


def diff_prompt_attention(q, k, v, lam):
    B, T = q.shape[:2]
    nb = T // Q_BLOCK
    qb = q.reshape((B, nb, Q_BLOCK) + q.shape[2:]).swapaxes(0, 1)
    kpos = jnp.arange(T)

    def block(args):
        i, qi = args
        qpos = i * Q_BLOCK + jnp.arange(Q_BLOCK)
        s = jnp.where(kpos[None, :] <= qpos[:, None], diff_scores(qi, k), NEG_INF)
        p = jax.nn.softmax(s, axis=-1)
        a = (p[:, :, :, 0] - lam * p[:, :, :, 1]).astype(v.dtype)
        return jnp.einsum('bngqs,bsnv->bqngv', a, v)

    o = lax.map(block, (jnp.arange(nb), qb))
    return o.swapaxes(0, 1).reshape(B, T, DIFF_HEADS, DIFF_V_DIM)


def diff_sample_attention(q, k_new, v_new, cache_k, cache_v, page_table, layer, lam):
    B, Q, n, g, c, _ = q.shape
    pv = lambda p, vv: jnp.einsum('bngcqs,bsnv->bngcqv', p, vv)
    carry = (jnp.full((B, n, g, c, Q), NEG_INF, F32), jnp.zeros((B, n, g, c, Q), F32),
             jnp.zeros((B, n, g, c, Q, DIFF_V_DIM), F32))

    def page_step(carry, phys):
        s = diff_scores(q, cache_k[layer, phys])
        return online_step(carry, s, cache_v[layer, phys], pv), None

    carry, _ = lax.scan(page_step, carry, page_table.T)
    s = jnp.where(causal_mask(Q), diff_scores(q, k_new), NEG_INF)
    m, l, acc = online_step(carry, s, v_new, pv)
    o = acc / l[..., None]
    o = o[:, :, :, 0] - lam * o[:, :, :, 1]
    return o.transpose(0, 3, 1, 2, 4).reshape(B, Q, DIFF_HEADS, DIFF_V_DIM).astype(v_new.dtype)


def mla_keys(ckv, krope_rot, w_uk, k_norm):
    k_nope = jnp.einsum('bsr,rhd->bshd', ckv, w_uk)
    k_rope = jnp.broadcast_to(krope_rot[:, :, None, :], k_nope.shape[:-1] + (MLA_ROPE,))
    k = jnp.concatenate([k_nope, k_rope.astype(k_nope.dtype)], axis=-1)
    return rms_norm(k, tied_gain(k_norm))


def mla_prompt_attention(q, k, ckv, w_uv):
    B, T, H, dq = q.shape
    nb = T // Q_BLOCK
    qb = q.reshape(B, nb, Q_BLOCK, H, dq).swapaxes(0, 1)
    kpos = jnp.arange(T)

    def block(args):
        i, qi = args
        qpos = i * Q_BLOCK + jnp.arange(Q_BLOCK)
        s = jnp.einsum('bqhd,bshd->bhqs', qi, k).astype(F32) * MLA_SCALE
        s = jnp.where(kpos[None, :] <= qpos[:, None], s, NEG_INF)
        p = jax.nn.softmax(s, axis=-1).astype(ckv.dtype)
        return jnp.einsum('bhqs,bsr->bqhr', p, ckv)

    lat = lax.map(block, (jnp.arange(nb), qb)).swapaxes(0, 1).reshape(B, T, H, MLA_KV_LORA)
    return jnp.einsum('bthr,rhd->bthd', lat, w_uv).reshape(B, T, H * MLA_V)


def mla_sample_attention(q, ckv_new, krope_new, cache_ckv, cache_krope, page_table, layer, lp):
    B, Q, H, _ = q.shape
    pv = lambda p, c: jnp.einsum('bhqs,bsr->bhqr', p, c)
    carry = (jnp.full((B, H, Q), NEG_INF, F32), jnp.zeros((B, H, Q), F32),
             jnp.zeros((B, H, Q, MLA_KV_LORA), F32))

    def page_step(carry, phys):
        c = cache_ckv[layer, phys]
        k = mla_keys(c, cache_krope[layer, phys], lp['w_mla_uk'], lp['mla_k_norm'])
        s = jnp.einsum('bqhd,bshd->bhqs', q, k).astype(F32) * MLA_SCALE
        return online_step(carry, s, c, pv), None

    carry, _ = lax.scan(page_step, carry, page_table.T)
    k = mla_keys(ckv_new, krope_new, lp['w_mla_uk'], lp['mla_k_norm'])
    s = jnp.einsum('bqhd,bshd->bhqs', q, k).astype(F32) * MLA_SCALE
    s = jnp.where(causal_mask(Q), s, NEG_INF)
    m, l, acc = online_step(carry, s, ckv_new, pv)
    lat = acc / l[..., None]
    o = jnp.einsum('bhqr,rhd->bqhd', lat, lp['w_mla_uv'].astype(F32))
    return o.reshape(B, Q, H * MLA_V).astype(q.dtype)


def memory_kv(mem, lp):
    B, M, _ = mem.shape
    kv = rms_norm(mem, lp['norm_mem']) @ lp['w_mem_kv']
    k, v = split_last(kv, (MEM_Q_W, MEM_Q_W))
    k = rms_norm(k.reshape(B, M, MEM_HEADS, MEM_HEAD_DIM), lp['mem_k_norm'])
    return k, v.reshape(B, M, MEM_HEADS, MEM_HEAD_DIM)


def mem_attention(q, mk, mv):
    B, T = q.shape[:2]
    s = jnp.einsum('bthd,bmhd->bhtm', q, mk).astype(F32) * MEM_SCALE
    p = jax.nn.softmax(s, axis=-1).astype(mv.dtype)
    return jnp.einsum('bhtm,bmhd->bthd', p, mv).reshape(B, T, MEM_Q_W)


def mixer_merge(x, o_diff, o_mla, o_mem, gates, lp, lam_init):
    B, T, _ = x.shape
    o_diff = (rms_norm(o_diff, lp['diff_subln'], SUBLN_EPS) * (1.0 - lam_init)).reshape(B, T, -1)
    branches = (o_diff, o_mla, o_mem)
    merged, off = 0.0, 0
    for i, w in enumerate(BRANCH_WIDTHS):
        merged = merged + gates[:, :, i] * (branches[i] @ lp['w_branch'][off:off + w])
        off += w
    return x + merged @ lp['w_out']


def moe_ffn(h, lp):
    lead = h.shape[:-1]
    xt = h.reshape(-1, D_MODEL)
    N = xt.shape[0]
    logits = (xt @ lp['w_router']).astype(F32) + lp['b_router'].astype(F32)
    top_val, top_idx = lax.top_k(logits, TOP_K)
    gate_w = jax.nn.softmax(top_val, axis=-1)
    flat_e = top_idx.reshape(-1).astype(jnp.int32)
    flat_tok = jnp.arange(N * TOP_K, dtype=jnp.int32) // TOP_K
    flat_w = gate_w.reshape(-1)
    order = jnp.argsort(flat_e)
    sorted_e = flat_e[order]
    counts = jnp.zeros((N_EXPERTS,), jnp.int32).at[flat_e].add(1)
    padded = (counts + MOE_BLOCK - 1) // MOE_BLOCK * MOE_BLOCK
    start = jnp.cumsum(counts) - counts
    pend = jnp.cumsum(padded)
    pstart = pend - padded
    rank = jnp.arange(N * TOP_K, dtype=jnp.int32) - start[sorted_e]
    dest = pstart[sorted_e] + rank
    n_rows = (N * TOP_K + N_EXPERTS * (MOE_BLOCK - 1) + MOE_BLOCK - 1) // MOE_BLOCK * MOE_BLOCK
    n_blocks = n_rows // MOE_BLOCK
    row_tok = jnp.full((n_rows,), N, jnp.int32).at[dest].set(flat_tok[order])
    row_w = jnp.zeros((n_rows,), F32).at[dest].set(flat_w[order])
    block_e = jnp.minimum(jnp.searchsorted(pend, jnp.arange(n_blocks, dtype=jnp.int32) * MOE_BLOCK,
                                           side='right'), N_EXPERTS - 1)
    x_pad = jnp.concatenate([xt, jnp.zeros((1, D_MODEL), xt.dtype)], axis=0)
    x_rows = x_pad[row_tok].reshape(n_blocks, MOE_BLOCK, D_MODEL)

    def expert_block(args):
        xb, e = args
        gu = xb @ lp['w_gate_up'][e] + lp['b_gate_up'][e]
        gate, up = gu[..., :D_EXPERT], gu[..., D_EXPERT:]
        gate = jnp.minimum(gate, SWIGLU_LIMIT)
        up = jnp.clip(up, -SWIGLU_LIMIT, SWIGLU_LIMIT)
        act = (up + 1.0) * gate * jax.nn.sigmoid(SWIGLU_ALPHA * gate)
        return act @ lp['w_down'][e] + lp['b_down'][e]

    y_rows = lax.map(expert_block, (x_rows, block_e)).reshape(n_rows, D_MODEL)
    out = jnp.zeros((N + 1, D_MODEL), F32).at[row_tok].add(row_w[:, None] * y_rows.astype(F32))
    return out[:N].astype(h.dtype).reshape(lead + (D_MODEL,))


def setup_inputs(seed: int = 0) -> dict:
    key = jax.random.key(seed)
    ks = list(jax.random.split(key, 48))

    def nrm(shape, scale=1.0):
        return jax.random.normal(ks.pop(), shape, F32) * scale

    def gain(shape):
        return 1.0 + nrm(shape, 0.02)

    L = DEPTH
    n_pages = PAST_LEN // PAGE_SIZE
    used = DEC_BATCH * n_pages
    n_pool = used + max(1, used // 4)
    page_table = jax.random.permutation(ks.pop(), n_pool)[:used].reshape(DEC_BATCH, n_pages).astype(jnp.int32)
    return {
        'x_prompt': nrm((BATCH, SEQ, D_MODEL)),
        'x_sample': nrm((DEC_BATCH, DEC_SEQ, D_MODEL)),
        'mem_prompt': nrm((BATCH, N_MEM, D_MODEL)),
        'cache_diff_k': nrm((L, n_pool, PAGE_SIZE, DIFF_KV_HEADS, 2, DIFF_HEAD_DIM)),
        'cache_diff_v': nrm((L, n_pool, PAGE_SIZE, DIFF_KV_HEADS, DIFF_V_DIM)),
        'cache_mla_ckv': nrm((L, n_pool, PAGE_SIZE, MLA_KV_LORA)),
        'cache_mla_krope': nrm((L, n_pool, PAGE_SIZE, MLA_ROPE)),
        'cache_mem_k': nrm((L, DEC_BATCH, N_MEM, MEM_HEADS, MEM_HEAD_DIM)),
        'cache_mem_v': nrm((L, DEC_BATCH, N_MEM, MEM_HEADS, MEM_HEAD_DIM)),
        'page_table': page_table,
        'norm_mix': gain((L, D_MODEL)),
        'norm_mem': gain((L, D_MODEL)),
        'w_in': nrm((L, D_MODEL, IN_WIDTH), D_MODEL ** -0.5),
        'b_gate': nrm((L, N_BRANCH * D_MODEL), 0.02),
        'diff_q_norm': gain((L, DIFF_HEAD_DIM)),
        'diff_k_norm': gain((L, DIFF_HEAD_DIM)),
        'diff_lambda': nrm((L, 4, DIFF_HEAD_DIM), 0.1),
        'diff_subln': gain((L, DIFF_V_DIM)),
        'mla_q_a_norm': gain((L, MLA_Q_LORA)),
        'w_mla_uq': nrm((L, MLA_Q_LORA, MLA_HEADS, MLA_NOPE + MLA_ROPE), MLA_Q_LORA ** -0.5),
        'mla_kv_a_norm': gain((L, MLA_KV_LORA)),
        'w_mla_uk': nrm((L, MLA_KV_LORA, MLA_HEADS, MLA_NOPE), MLA_KV_LORA ** -0.5),
        'w_mla_uv': nrm((L, MLA_KV_LORA, MLA_HEADS, MLA_V), MLA_KV_LORA ** -0.5),
        'mla_q_norm': gain((L, MLA_NOPE + MLA_ROPE // 2)),
        'mla_k_norm': gain((L, MLA_NOPE + MLA_ROPE // 2)),
        'w_mem_kv': nrm((L, D_MODEL, 2 * MEM_Q_W), D_MODEL ** -0.5),
        'mem_q_norm': gain((L, MEM_HEAD_DIM)),
        'mem_k_norm': gain((L, MEM_HEAD_DIM)),
        'w_branch': nrm((L, MIX_WIDTH, D_MODEL), BRANCH_WIDTHS[0] ** -0.5),
        'w_out': nrm((L, D_MODEL, D_MODEL), D_MODEL ** -0.5),
        'norm_ffn': gain((L, D_MODEL)),
        'w_router': nrm((L, D_MODEL, N_EXPERTS), D_MODEL ** -0.5),
        'b_router': nrm((L, N_EXPERTS), 0.01),
        'w_gate_up': nrm((L, N_EXPERTS, D_MODEL, 2 * D_EXPERT), D_MODEL ** -0.5),
        'b_gate_up': nrm((L, N_EXPERTS, 2 * D_EXPERT), 0.01),
        'w_down': nrm((L, N_EXPERTS, D_EXPERT, D_MODEL), D_EXPERT ** -0.5),
        'b_down': nrm((L, N_EXPERTS, D_MODEL), 0.01),
    }


def reference(x_prompt, x_sample, mem_prompt, cache_diff_k, cache_diff_v, cache_mla_ckv,
              cache_mla_krope, cache_mem_k, cache_mem_v, page_table,
              norm_mix, norm_mem, w_in, b_gate, diff_q_norm, diff_k_norm, diff_lambda, diff_subln,
              mla_q_a_norm, w_mla_uq, mla_kv_a_norm, w_mla_uk, w_mla_uv, mla_q_norm, mla_k_norm,
              w_mem_kv, mem_q_norm, mem_k_norm, w_branch, w_out,
              norm_ffn, w_router, b_router, w_gate_up, b_gate_up, w_down, b_down):
    past_len = page_table.shape[1] * cache_diff_k.shape[2]
    pos_p = jnp.arange(x_prompt.shape[1], dtype=jnp.int32)
    pos_s = past_len + jnp.arange(x_sample.shape[1], dtype=jnp.int32)
    xp, xs = x_prompt, x_sample
    p_dk, p_dv, p_ckv, p_kr, p_mk, p_mv = [], [], [], [], [], []
    s_dk, s_dv, s_ckv, s_kr = [], [], [], []
    for layer in range(DEPTH):
        lp = {
            'norm_mix': norm_mix[layer], 'norm_mem': norm_mem[layer], 'w_in': w_in[layer],
            'b_gate': b_gate[layer], 'diff_q_norm': diff_q_norm[layer], 'diff_k_norm': diff_k_norm[layer],
            'diff_subln': diff_subln[layer], 'mla_q_a_norm': mla_q_a_norm[layer],
            'w_mla_uq': w_mla_uq[layer], 'mla_kv_a_norm': mla_kv_a_norm[layer],
            'w_mla_uk': w_mla_uk[layer], 'w_mla_uv': w_mla_uv[layer], 'mla_q_norm': mla_q_norm[layer],
            'mla_k_norm': mla_k_norm[layer], 'w_mem_kv': w_mem_kv[layer], 'mem_q_norm': mem_q_norm[layer],
            'mem_k_norm': mem_k_norm[layer], 'w_branch': w_branch[layer], 'w_out': w_out[layer],
            'w_router': w_router[layer], 'b_router': b_router[layer], 'w_gate_up': w_gate_up[layer],
            'b_gate_up': b_gate_up[layer], 'w_down': w_down[layer], 'b_down': b_down[layer],
        }
        lam_init = 0.8 - 0.6 * math.exp(-0.3 * layer)
        lamp = diff_lambda[layer].astype(F32)
        lam = jnp.exp(jnp.sum(lamp[0] * lamp[1])) - jnp.exp(jnp.sum(lamp[2] * lamp[3])) + lam_init

        dq, dk, dv, mq, ckv, krope, memq, gates = mixer_inputs(xp, pos_p, lp)
        o_diff = diff_prompt_attention(dq, dk, dv, lam)
        k_mla = mla_keys(ckv, krope, lp['w_mla_uk'], lp['mla_k_norm'])
        o_mla = mla_prompt_attention(mq, k_mla, ckv, lp['w_mla_uv'])
        mk, mv = memory_kv(mem_prompt, lp)
        o_mem = mem_attention(memq, mk, mv)
        xp = mixer_merge(xp, o_diff, o_mla, o_mem, gates, lp, lam_init)
        xp = xp + moe_ffn(rms_norm(xp, norm_ffn[layer]), lp)
        p_dk.append(dk); p_dv.append(dv); p_ckv.append(ckv); p_kr.append(krope)
        p_mk.append(mk); p_mv.append(mv)

        dq, dk, dv, mq, ckv, krope, memq, gates = mixer_inputs(xs, pos_s, lp)
        o_diff = diff_sample_attention(dq, dk, dv, cache_diff_k, cache_diff_v, page_table, layer, lam)
        o_mla = mla_sample_attention(mq, ckv, krope, cache_mla_ckv, cache_mla_krope, page_table, layer, lp)
        o_mem = mem_attention(memq, cache_mem_k[layer], cache_mem_v[layer])
        xs = mixer_merge(xs, o_diff, o_mla, o_mem, gates, lp, lam_init)
        xs = xs + moe_ffn(rms_norm(xs, norm_ffn[layer]), lp)
        s_dk.append(dk); s_dv.append(dv); s_ckv.append(ckv); s_kr.append(krope)

    y_prompt, y_sample = xp, xs
    new_p_diff_k = jnp.stack(p_dk)
    new_p_diff_v = jnp.stack(p_dv)
    new_p_mla_ckv = jnp.stack(p_ckv)
    new_p_mla_krope = jnp.stack(p_kr)
    new_p_mem_k = jnp.stack(p_mk)
    new_p_mem_v = jnp.stack(p_mv)
    new_s_diff_k = jnp.stack(s_dk)
    new_s_diff_v = jnp.stack(s_dv)
    new_s_mla_ckv = jnp.stack(s_ckv)
    new_s_mla_krope = jnp.stack(s_kr)
    return (y_prompt, y_sample, new_p_diff_k, new_p_diff_v, new_p_mla_ckv, new_p_mla_krope,
            new_p_mem_k, new_p_mem_v, new_s_diff_k, new_s_diff_v, new_s_mla_ckv, new_s_mla_krope)
```

```python
import functools
import math

import jax
import jax.numpy as jnp
from jax import lax
from jax.experimental import pallas as pl
from jax.experimental.pallas import tpu as pltpu

F32 = jnp.float32
BF16 = jnp.bfloat16

D_MODEL = 1024
DIFF_HEAD_DIM = 64
DIFF_V_DIM = 128
MLA_HEADS = 8
MLA_Q_LORA = 256
MLA_KV_LORA = 128
MLA_NOPE = 64
MLA_ROPE = 32
MLA_V = 64
MLA_QK = MLA_NOPE + MLA_ROPE
MEM_HEADS = 4
MEM_HEAD_DIM = 128
N_MEM = 256
N_EXPERTS = 32
TOP_K = 4
D_EXPERT = 1024
SWIGLU_ALPHA = 1.702
SWIGLU_LIMIT = 7.0
ROPE_THETA = 10000.0
NORM_EPS = 1e-6
SUBLN_EPS = 1e-5
NEG_INF = -1e30
PAGE = 128

DIFF_SCALE = DIFF_HEAD_DIM ** -0.5
MLA_SCALE = MLA_QK ** -0.5
MEM_SCALE = MEM_HEAD_DIM ** -0.5

LANES = 128
MXU_DIM = 256
VMEM_LIMIT = 52 * 1024 * 1024
TOKEN_TILE = 256
ATTN_TILE = 256
PAGES_PER_STEP = 8
MOE_ROWS = 256

_OFF_Q, _OFF_K, _OFF_V, _OFF_CQ, _OFF_CKV, _OFF_KR, _OFF_MQ, _OFF_G = 0, 512, 768, 1024, 1280, 1408, 1440, 1952


def _params(sem):
    return pltpu.CompilerParams(dimension_semantics=sem, vmem_limit_bytes=VMEM_LIMIT)


def _full(a):
    nd = a.ndim
    return pl.BlockSpec(a.shape, lambda *_: (0,) * nd)


def _dot(a, b):
    return jnp.dot(a, b, preferred_element_type=F32)


def _dot_nt(a, b):
    return lax.dot_general(a, b, (((1,), (1,)), ((), ())), preferred_element_type=F32)


def _rms_rows(x, eps=NORM_EPS):
    return x * lax.rsqrt(jnp.mean(x * x, axis=-1, keepdims=True) + eps)


def _group_sumsq(v, g_ref):
    sq = (v * v).astype(BF16)
    parts = [_dot(sq[:, j * MXU_DIM:(j + 1) * MXU_DIM], g_ref[...]) for j in range(v.shape[1] // MXU_DIM)]
    return parts[0] if len(parts) == 1 else jnp.concatenate(parts, axis=-1)


def _rope_lanes(v, c, sa, sb, half):
    outs = []
    for j in range(v.shape[1] // LANES):
        b = v[:, j * LANES:(j + 1) * LANES]
        outs.append(b * c + pltpu.roll(b, LANES - half, 1) * sa + pltpu.roll(b, half, 1) * sb)
    return outs[0] if len(outs) == 1 else jnp.concatenate(outs, axis=-1)


def _inproj_kernel(x_ref, gmix_ref, wa_ref, wuq_ref, wk_ref, g64_ref, g128_ref,
                   gq_ref, gk_ref, gcq_ref, gckv_ref, gmq_ref, gkm_ref, gmemq_ref,
                   c64_ref, sa64_ref, sb64_ref, c32_ref, sa32_ref, sb32_ref,
                   dq_ref, dk32_ref, dk16_ref, dv32_ref, dv16_ref, mq_ref,
                   ckv32_ref, ckv16_ref, kr32_ref, kmla_ref, memq_ref):
    x = x_ref[...]
    h = (_rms_rows(x) * gmix_ref[...]).astype(BF16)
    z = _dot(h, wa_ref[...])
    c64, sa64, sb64 = c64_ref[...], sa64_ref[...], sb64_ref[...]
    c32, sa32, sb32 = c32_ref[...], sa32_ref[...], sb32_ref[...]

    zq = z[:, 0:512]
    qn = zq * lax.rsqrt(_group_sumsq(zq, g64_ref) * (1.0 / DIFF_HEAD_DIM) + NORM_EPS) * gq_ref[...]
    dq_ref[...] = _rope_lanes(qn, c64, sa64, sb64, 32).astype(BF16)

    zk = z[:, 512:768]
    kn = zk * lax.rsqrt(_group_sumsq(zk, g64_ref) * (1.0 / DIFF_HEAD_DIM) + NORM_EPS) * gk_ref[...]
    dk = _rope_lanes(kn, c64, sa64, sb64, 32)
    dk32_ref[...] = dk
    dk16_ref[...] = dk.astype(BF16)

    dv = z[:, 768:1024]
    dv32_ref[...] = dv
    dv16_ref[...] = dv.astype(BF16)

    cq = (_rms_rows(z[:, 1024:1280]) * gcq_ref[...]).astype(BF16)
    mqr = _dot(cq, wuq_ref[...])
    mqn = mqr * lax.rsqrt(_group_sumsq(mqr, g128_ref) * (1.0 / MLA_QK) + NORM_EPS) * gmq_ref[...]
    mq_ref[...] = _rope_lanes(mqn, c32, sa32, sb32, 16).astype(BF16)

    ckv = _rms_rows(z[:, 1280:1408]) * gckv_ref[...]
    ckv32_ref[...] = ckv
    ckv16 = ckv.astype(BF16)
    ckv16_ref[...] = ckv16

    krb = _rope_lanes(z[:, 1408:1536], c32, sa32, sb32, 16)
    kr32_ref[...] = krb[:, 64:96]
    kin = jnp.concatenate([ckv16, krb.astype(BF16)], axis=-1)
    kraw = _dot(kin, wk_ref[...])
    kmla_ref[...] = (kraw * lax.rsqrt(_group_sumsq(kraw, g128_ref) * (1.0 / MLA_QK) + NORM_EPS)
                     * gkm_ref[...]).astype(BF16)

    zm = z[:, 1536:2048]
    memq_ref[...] = (zm * lax.rsqrt(_group_sumsq(zm, g128_ref) * (1.0 / MEM_HEAD_DIM) + NORM_EPS)
                     * gmemq_ref[...]).astype(BF16)


def _inproj(x2d, tabs, wts):
    n = x2d.shape[0]
    tm = min(TOKEN_TILE, n)
    period = tabs[0].shape[0] // tm
    row = lambda w: pl.BlockSpec((tm, w), lambda i: (i, 0))
    tab = pl.BlockSpec((tm, LANES), lambda i: (i % period, 0))
    out_w = [(512, BF16), (256, F32), (256, BF16), (256, F32), (256, BF16), (1024, BF16),
             (128, F32), (128, BF16), (32, F32), (1024, BF16), (512, BF16)]
    return pl.pallas_call(
        _inproj_kernel,
        grid=(n // tm,),
        in_specs=[row(D_MODEL)] + [_full(a) for a in wts] + [tab] * 6,
        out_specs=[row(w) for w, _ in out_w],
        out_shape=[jax.ShapeDtypeStruct((n, w), dt) for w, dt in out_w],
        compiler_params=_params(("parallel",)),
        name="inproj",
    )(x2d, *wts, *tabs)


def _memkv_kernel(x_ref, g_ref, w_ref, g128_ref, gk_ref, k32_ref, k16_ref, v32_ref, v16_ref):
    h = (_rms_rows(x_ref[...]) * g_ref[...]).astype(BF16)
    kv = _dot(h, w_ref[...])
    k = kv[:, 0:512]
    k = k * lax.rsqrt(_group_sumsq(k, g128_ref) * (1.0 / MEM_HEAD_DIM) + NORM_EPS) * gk_ref[...]
    v = kv[:, 512:1024]
    k32_ref[...] = k
    k16_ref[...] = k.astype(BF16)
    v32_ref[...] = v
    v16_ref[...] = v.astype(BF16)


def _memkv(mem2d, g, w, g128, gk):
    n = mem2d.shape[0]
    tm = min(TOKEN_TILE, n)
    row = lambda w_: pl.BlockSpec((tm, w_), lambda i: (i, 0))
    return pl.pallas_call(
        _memkv_kernel,
        grid=(n // tm,),
        in_specs=[row(D_MODEL), _full(g), _full(w), _full(g128), _full(gk)],
        out_specs=[row(512)] * 4,
        out_shape=[jax.ShapeDtypeStruct((n, 512), dt) for dt in (F32, BF16, F32, BF16)],
        compiler_params=_params(("parallel",)),
        name="memkv",
    )(mem2d, g, w, g128, gk)


def _online_update(s, m_ref, l_ref, rows=None):
    sl = slice(None) if rows is None else rows
    m_old = m_ref[sl, :]
    m_new = jnp.maximum(m_old, jnp.max(s, axis=-1, keepdims=True))
    alpha = jnp.exp(m_old - m_new)
    p = jnp.exp(s - m_new)
    l_ref[sl, :] = alpha * l_ref[sl, :] + jnp.sum(p, axis=-1, keepdims=True)
    m_ref[sl, :] = m_new
    return p, alpha


def _diff_prompt_kernel(lam_ref, q_ref, k_ref, v_ref, gs_ref, o_ref, q4_ref, m_ref, l_ref, acc_ref, *, tq):
    i = pl.program_id(2)
    lane = lax.broadcasted_iota(jnp.int32, (tq, LANES), 1)
    zero = jnp.zeros((tq, LANES), BF16)
    for g in range(2):
        qg = q_ref[:, g * LANES:(g + 1) * LANES]
        q4_ref[g * tq:(g + 1) * tq, :] = jnp.where(lane < DIFF_HEAD_DIM, qg, zero)
        q4_ref[(2 + g) * tq:(3 + g) * tq, :] = jnp.where(lane >= DIFF_HEAD_DIM, qg, zero)
    m_ref[...] = jnp.full(m_ref.shape, NEG_INF, F32)
    l_ref[...] = jnp.zeros(l_ref.shape, F32)
    acc_ref[...] = jnp.zeros(acc_ref.shape, F32)

    def step(j, masked):
        off = pl.multiple_of(j * tq, tq)
        k = k_ref[pl.ds(off, tq), :]
        v = v_ref[pl.ds(off, tq), :]
        s = _dot_nt(q4_ref[...], k)
        if masked:
            qpos = lax.broadcasted_iota(jnp.int32, s.shape, 0) & (tq - 1)
            kpos = lax.broadcasted_iota(jnp.int32, s.shape, 1)
            s = jnp.where(kpos <= qpos, s, NEG_INF)
        p, alpha = _online_update(s, m_ref, l_ref)
        acc_ref[...] = alpha * acc_ref[...] + _dot(p.astype(BF16), v)

    def body(j, carry):
        step(j, False)
        return carry

    lax.fori_loop(0, i, body, 0)
    step(i, True)

    o = acc_ref[...] / l_ref[...]
    od = o[0:2 * tq] - lam_ref[0] * o[2 * tq:4 * tq]
    od = _rms_rows(od, SUBLN_EPS) * gs_ref[...]
    for g in range(2):
        o_ref[:, g * LANES:(g + 1) * LANES] = od[g * tq:(g + 1) * tq].astype(BF16)


def _diff_prompt(lam, dq, dk16, dv16, gs):
    b, t, _ = dq.shape
    tq = min(ATTN_TILE, t)
    kern = functools.partial(_diff_prompt_kernel, tq=tq)
    return pl.pallas_call(
        kern,
        grid=(b, 2, t // tq),
        in_specs=[pl.BlockSpec(memory_space=pltpu.SMEM),
                  pl.BlockSpec((None, tq, 256), lambda bi, n, i: (bi, i, n)),
                  pl.BlockSpec((None, t, LANES), lambda bi, n, i: (bi, 0, n)),
                  pl.BlockSpec((None, t, LANES), lambda bi, n, i: (bi, 0, n)),
                  pl.BlockSpec((1, LANES), lambda bi, n, i: (0, 0))],
        out_specs=pl.BlockSpec((None, tq, 256), lambda bi, n, i: (bi, i, n)),
        out_shape=jax.ShapeDtypeStruct((b, t, 512), BF16),
        scratch_shapes=[pltpu.VMEM((4 * tq, LANES), BF16), pltpu.VMEM((4 * tq, 1), F32),
                        pltpu.VMEM((4 * tq, 1), F32), pltpu.VMEM((4 * tq, LANES), F32)],
        compiler_params=_params(("parallel", "parallel", "parallel")),
        name="diff_prompt",
    )(lam, dq, dk16, dv16, gs)


def _mla_prompt_kernel(q_ref, k_ref, c_ref, wuv_ref, o_ref, p_ref, al_ref, m_ref, l_ref, acc_ref, *, tq):
    i = pl.program_id(1)
    m_ref[...] = jnp.full(m_ref.shape, NEG_INF, F32)
    l_ref[...] = jnp.zeros(l_ref.shape, F32)
    acc_ref[...] = jnp.zeros(acc_ref.shape, F32)

    def step(j, masked):
        off = pl.multiple_of(j * tq, tq)
        for h in range(MLA_HEADS):
            rows = slice(h * tq, (h + 1) * tq)
            s = _dot_nt(q_ref[:, h * LANES:(h + 1) * LANES], k_ref[pl.ds(off, tq), h * LANES:(h + 1) * LANES])
            if masked:
                qpos = lax.broadcasted_iota(jnp.int32, s.shape, 0)
                kpos = lax.broadcasted_iota(jnp.int32, s.shape, 1)
                s = jnp.where(kpos <= qpos, s, NEG_INF)
            p, alpha = _online_update(s, m_ref, l_ref, rows)
            p_ref[rows, :] = p.astype(BF16)
            al_ref[rows, :] = alpha
        acc_ref[...] = al_ref[...] * acc_ref[...] + _dot(p_ref[...], c_ref[pl.ds(off, tq), :])

    def body(j, carry):
        step(j, False)
        return carry

    lax.fori_loop(0, i, body, 0)
    step(i, True)

    lat = (acc_ref[...] / l_ref[...]).astype(BF16)
    o = _dot(lat[0:tq], wuv_ref[0])
    for h in range(1, MLA_HEADS):
        o = o + _dot(lat[h * tq:(h + 1) * tq], wuv_ref[h])
    o_ref[...] = o.astype(BF16)


def _mla_prompt(mq, kmla, ckv16, wuv2):
    b, t, _ = mq.shape
    tq = min(ATTN_TILE, t)
    kern = functools.partial(_mla_prompt_kernel, tq=tq)
    return pl.pallas_call(
        kern,
        grid=(b, t // tq),
        in_specs=[pl.BlockSpec((None, tq, 1024), lambda bi, i: (bi, i, 0)),
                  pl.BlockSpec((None, t, 1024), lambda bi, i: (bi, 0, 0)),
                  pl.BlockSpec((None, t, LANES), lambda bi, i: (bi, 0, 0)),
                  pl.BlockSpec(wuv2.shape, lambda bi, i: (0, 0, 0))],
        out_specs=pl.BlockSpec((None, tq, 512), lambda bi, i: (bi, i, 0)),
        out_shape=jax.ShapeDtypeStruct((b, t, 512), BF16),
        scratch_shapes=[pltpu.VMEM((MLA_HEADS * tq, tq), BF16), pltpu.VMEM((MLA_HEADS * tq, 1), F32),
                        pltpu.VMEM((MLA_HEADS * tq, 1), F32), pltpu.VMEM((MLA_HEADS * tq, 1), F32),
                        pltpu.VMEM((MLA_HEADS * tq, LANES), F32)],
        compiler_params=_params(("parallel", "parallel")),
        name="mla_prompt",
    )(mq, kmla, ckv16, wuv2)


def _softmax_pv(s, v):
    m = jnp.max(s, axis=-1, keepdims=True)
    p = jnp.exp(s - m)
    l = jnp.sum(p, axis=-1, keepdims=True)
    return _dot(p.astype(BF16), v) / l


def _mem_prompt_kernel(q_ref, k_ref, v_ref, o_ref):
    for h in range(MEM_HEADS):
        sl = slice(h * LANES, (h + 1) * LANES)
        s = _dot_nt(q_ref[:, sl], k_ref[:, sl])
        o_ref[:, sl] = _softmax_pv(s, v_ref[:, sl]).astype(BF16)


def _mem_prompt(memq, mk16, mv16):
    b, t, _ = memq.shape
    tq = min(2 * ATTN_TILE, t)
    return pl.pallas_call(
        _mem_prompt_kernel,
        grid=(b, t // tq),
        in_specs=[pl.BlockSpec((None, tq, 512), lambda bi, i: (bi, i, 0)),
                  pl.BlockSpec((None, N_MEM, 512), lambda bi, i: (bi, 0, 0)),
                  pl.BlockSpec((None, N_MEM, 512), lambda bi, i: (bi, 0, 0))],
        out_specs=pl.BlockSpec((None, tq, 512), lambda bi, i: (bi, i, 0)),
        out_shape=jax.ShapeDtypeStruct((b, t, 512), BF16),
        compiler_params=_params(("parallel", "parallel")),
        name="mem_prompt",
    )(memq, mk16, mv16)


def _mem_sample_kernel(q_ref, k_ref, v_ref, o_ref, *, bb):
    for bi in range(bb):
        for h in range(MEM_HEADS):
            sl = slice(h * LANES, (h + 1) * LANES)
            k = k_ref[bi, pl.ds(h, N_MEM, stride=MEM_HEADS), :].astype(BF16)
            v = v_ref[bi, pl.ds(h, N_MEM, stride=MEM_HEADS), :].astype(BF16)
            s = _dot_nt(q_ref[bi, :, sl], k)
            o_ref[bi, :, sl] = _softmax_pv(s, v).astype(BF16)


def _mem_sample(memq, ck, cv, b_off):
    b, t, _ = memq.shape
    bb = math.gcd(b, 4)
    off = b_off // bb
    kern = functools.partial(_mem_sample_kernel, bb=bb)
    rows = N_MEM * MEM_HEADS
    return pl.pallas_call(
        kern,
        grid=(b // bb,),
        in_specs=[pl.BlockSpec((bb, t, 512), lambda i: (i, 0, 0)),
                  pl.BlockSpec((bb, rows, LANES), lambda i: (i + off, 0, 0)),
                  pl.BlockSpec((bb, rows, LANES), lambda i: (i + off, 0, 0))],
        out_specs=pl.BlockSpec((bb, t, 512), lambda i: (i, 0, 0)),
        out_shape=jax.ShapeDtypeStruct((b, t, 512), BF16),
        compiler_params=_params(("parallel",)),
        name="mem_sample",
    )(memq, ck, cv)


def _page_copies(pt_ref, step, slot, ch, pairs):
    cps = []
    for src, buf, sem in pairs:
        for r in range(ch):
            cps.append(pltpu.make_async_copy(src.at[pt_ref[step * ch + r]], buf.at[slot, r], sem.at[slot]))
    return cps


def _stream_step(pt_ref, ch, pairs):
    nc = pl.num_programs(1)
    s = pl.program_id(0) * nc + pl.program_id(1)
    total = pl.num_programs(0) * nc
    slot = s & 1

    @pl.when(s == 0)
    def _():
        for cp in _page_copies(pt_ref, 0, 0, ch, pairs):
            cp.start()

    @pl.when(s + 1 < total)
    def _():
        for cp in _page_copies(pt_ref, s + 1, 1 - slot, ch, pairs):
            cp.start()

    for cp in _page_copies(pt_ref, s, slot, ch, pairs):
        cp.wait()
    return slot


def _diff_sample_kernel(pt_ref, lam_ref, q_ref, kn_ref, vn_ref, gs_ref, kc_hbm, vc_hbm, o_ref,
                        kbuf, vbuf, sem, m_ref, l_ref, acc_ref, *, ch):
    j = pl.program_id(1)
    slot = _stream_step(pt_ref, ch, [(kc_hbm, kbuf, sem.at[0]), (vc_hbm, vbuf, sem.at[1])])

    @pl.when(j == 0)
    def _():
        m_ref[...] = jnp.full(m_ref.shape, NEG_INF, F32)
        l_ref[...] = jnp.zeros(l_ref.shape, F32)
        acc_ref[...] = jnp.zeros(acc_ref.shape, F32)

    q = q_ref[...]
    kt = jnp.concatenate([kbuf[slot, r].astype(BF16) for r in range(ch)], axis=-1)
    s = _dot(q, kt)
    p, alpha = _online_update(s, m_ref, l_ref)
    p = p.astype(BF16)
    pv = []
    for n in range(2):
        v = jnp.concatenate([vbuf[slot, r, pl.ds(n, PAGE, stride=2), :].astype(BF16) for r in range(ch)], axis=0)
        pv.append(_dot(p[n * 32:(n + 1) * 32], v))
    acc_ref[...] = alpha * acc_ref[...] + jnp.concatenate(pv, axis=0)

    @pl.when(j == pl.num_programs(1) - 1)
    def _():
        kn = kn_ref[...]
        vn = vn_ref[...]
        nq = kn.shape[0]
        s2 = _dot_nt(q, kn)
        qpos = lax.broadcasted_iota(jnp.int32, s2.shape, 0) & (nq - 1)
        kpos = lax.broadcasted_iota(jnp.int32, s2.shape, 1)
        s2 = jnp.where(kpos <= qpos, s2, NEG_INF)
        p2, alpha2 = _online_update(s2, m_ref, l_ref)
        p2 = p2.astype(BF16)
        pv2 = [_dot(p2[n * 32:(n + 1) * 32], vn[:, n * LANES:(n + 1) * LANES]) for n in range(2)]
        o = (alpha2 * acc_ref[...] + jnp.concatenate(pv2, axis=0)) / l_ref[...]
        for n in range(2):
            on = o[n * 32:(n + 1) * 32]
            od = on[0:16] - lam_ref[0] * on[16:32]
            od = _rms_rows(od, SUBLN_EPS) * gs_ref[...]
            for g in range(2):
                hh = n * 2 + g
                o_ref[:, hh * LANES:(hh + 1) * LANES] = od[g * nq:(g + 1) * nq].astype(BF16)


def _diff_sample(pt_flat, lam, qbd, kn16, vn16, gs, kc, vc, n_pages):
    b, nq, _ = kn16.shape
    ch = math.gcd(PAGES_PER_STEP, n_pages)
    kern = functools.partial(_diff_sample_kernel, ch=ch)
    grid_spec = pltpu.PrefetchScalarGridSpec(
        num_scalar_prefetch=1,
        grid=(b, n_pages // ch),
        in_specs=[pl.BlockSpec(memory_space=pltpu.SMEM),
                  pl.BlockSpec((None, 64, 256), lambda bi, j, pt: (bi, 0, 0)),
                  pl.BlockSpec((None, nq, 256), lambda bi, j, pt: (bi, 0, 0)),
                  pl.BlockSpec((None, nq, 256), lambda bi, j, pt: (bi, 0, 0)),
                  pl.BlockSpec((1, LANES), lambda bi, j, pt: (0, 0)),
                  pl.BlockSpec(memory_space=pl.ANY),
                  pl.BlockSpec(memory_space=pl.ANY)],
        out_specs=pl.BlockSpec((None, nq, 512), lambda bi, j, pt: (bi, 0, 0)),
        scratch_shapes=[pltpu.VMEM((2, ch, 256, LANES), F32), pltpu.VMEM((2, ch, 256, LANES), F32),
                        pltpu.SemaphoreType.DMA((2, 2)),
                        pltpu.VMEM((64, 1), F32), pltpu.VMEM((64, 1), F32), pltpu.VMEM((64, LANES), F32)],
    )
    return pl.pallas_call(
        kern,
        grid_spec=grid_spec,
        out_shape=jax.ShapeDtypeStruct((b, nq, 512), BF16),
        compiler_params=_params(("arbitrary", "arbitrary")),
        name="diff_sample",
    )(pt_flat, lam, qbd, kn16, vn16, gs, kc, vc)


def _mla_sample_kernel(pt_ref, q_ref, kn_ref, cn_ref, gk_ref, aq_ref, wukt_ref, wuv_ref, cc_hbm, rc_hbm, o_ref,
                       cbuf, rbuf, sem, lw_ref, qr_ref, m_ref, l_ref, acc_ref, *, ch):
    j = pl.program_id(1)
    slot = _stream_step(pt_ref, ch, [(cc_hbm, cbuf, sem.at[0]), (rc_hbm, rbuf, sem.at[1])])
    nq = q_ref.shape[0]
    nr = MLA_HEADS * nq
    nk = MLA_HEADS * MLA_NOPE

    @pl.when(j == 0)
    def _():
        m_ref[...] = jnp.full(m_ref.shape, NEG_INF, F32)
        l_ref[...] = jnp.zeros(l_ref.shape, F32)
        acc_ref[...] = jnp.zeros(acc_ref.shape, F32)
        lw_ref[0:nk, :] = wukt_ref[...]
        qg = (q_ref[...].astype(F32) * gk_ref[...]).astype(BF16)
        for h in range(MLA_HEADS):
            qa = _dot(qg[:, h * LANES:(h + 1) * LANES], aq_ref[h])
            lw_ref[nk + h * nq:nk + (h + 1) * nq, :] = qa[:, 0:LANES].astype(BF16)
            qr_ref[h * nq:(h + 1) * nq, :] = qa[:, LANES + 64:LANES + 96].astype(BF16)

    c = jnp.concatenate([cbuf[slot, r] for r in range(ch)], axis=0).astype(BF16)
    krt = jnp.concatenate([rbuf[slot, r] for r in range(ch)], axis=-1)
    big = _dot_nt(lw_ref[...], c)
    nkeys = big.shape[1]
    knt = big[0:nk]
    ssq = jnp.sum((knt * knt).reshape(MLA_HEADS, MLA_NOPE, nkeys), axis=1)
    ssq = ssq + jnp.sum(krt * krt, axis=0, keepdims=True)
    rn = lax.rsqrt(ssq * (1.0 / MLA_QK) + NORM_EPS)
    s = big[nk:nk + nr] + _dot(qr_ref[...], krt.astype(BF16))
    s = (s.reshape(MLA_HEADS, nq, nkeys) * rn[:, None, :]).reshape(nr, nkeys)
    p, alpha = _online_update(s, m_ref, l_ref)
    acc_ref[...] = alpha * acc_ref[...] + _dot(p.astype(BF16), c)

    @pl.when(j == pl.num_programs(1) - 1)
    def _():
        qf = q_ref[...]
        kn = kn_ref[...]
        s2 = jnp.concatenate([_dot_nt(qf[:, h * LANES:(h + 1) * LANES], kn[:, h * LANES:(h + 1) * LANES])
                              for h in range(MLA_HEADS)], axis=0)
        qpos = lax.broadcasted_iota(jnp.int32, s2.shape, 0) & (nq - 1)
        kpos = lax.broadcasted_iota(jnp.int32, s2.shape, 1)
        s2 = jnp.where(kpos <= qpos, s2, NEG_INF)
        p2, alpha2 = _online_update(s2, m_ref, l_ref)
        lat = ((alpha2 * acc_ref[...] + _dot(p2.astype(BF16), cn_ref[...])) / l_ref[...]).astype(BF16)
        o = _dot(lat[0:nq], wuv_ref[0])
        for h in range(1, MLA_HEADS):
            o = o + _dot(lat[h * nq:(h + 1) * nq], wuv_ref[h])
        o_ref[...] = o.astype(BF16)


def _mla_sample(pt_flat, mq, kmla, ckv16, gk_pad, aq, wukt, wuv2, cc, rc, n_pages):
    b, nq, _ = mq.shape
    ch = math.gcd(PAGES_PER_STEP, n_pages)
    kern = functools.partial(_mla_sample_kernel, ch=ch)
    nr = MLA_HEADS * nq
    grid_spec = pltpu.PrefetchScalarGridSpec(
        num_scalar_prefetch=1,
        grid=(b, n_pages // ch),
        in_specs=[pl.BlockSpec((None, nq, 1024), lambda bi, j, pt: (bi, 0, 0)),
                  pl.BlockSpec((None, nq, 1024), lambda bi, j, pt: (bi, 0, 0)),
                  pl.BlockSpec((None, nq, LANES), lambda bi, j, pt: (bi, 0, 0)),
                  pl.BlockSpec((1, 1024), lambda bi, j, pt: (0, 0)),
                  pl.BlockSpec(aq.shape, lambda bi, j, pt: (0, 0, 0)),
                  pl.BlockSpec(wukt.shape, lambda bi, j, pt: (0, 0)),
                  pl.BlockSpec(wuv2.shape, lambda bi, j, pt: (0, 0, 0)),
                  pl.BlockSpec(memory_space=pl.ANY),
                  pl.BlockSpec(memory_space=pl.ANY)],
        out_specs=pl.BlockSpec((None, nq, 512), lambda bi, j, pt: (bi, 0, 0)),
        scratch_shapes=[pltpu.VMEM((2, ch, PAGE, LANES), F32), pltpu.VMEM((2, ch, MLA_ROPE, PAGE), F32),
                        pltpu.SemaphoreType.DMA((2, 2)),
                        pltpu.VMEM((MLA_HEADS * MLA_NOPE + nr, LANES), BF16), pltpu.VMEM((nr, MLA_ROPE), BF16),
                        pltpu.VMEM((nr, 1), F32), pltpu.VMEM((nr, 1), F32), pltpu.VMEM((nr, LANES), F32)],
    )
    return pl.pallas_call(
        kern,
        grid_spec=grid_spec,
        out_shape=jax.ShapeDtypeStruct((b, nq, 512), BF16),
        compiler_params=_params(("arbitrary", "arbitrary")),
        name="mla_sample",
    )(pt_flat, mq, kmla, ckv16, gk_pad, aq, wukt, wuv2, cc, rc)


def _merge_kernel(x_ref, od_ref, om_ref, oc_ref, gmix_ref, wg_ref, bg_ref, wbr_ref, wout_ref, gffn_ref,
                  wrh_ref, wrl_ref, br_ref, xm_ref, h2_ref, tv_ref, ti_ref):
    x = x_ref[...]
    h = (_rms_rows(x) * gmix_ref[...]).astype(BF16)
    gates = jax.nn.sigmoid(_dot(h, wg_ref[...]) + bg_ref[...])
    merged = gates[:, 0:D_MODEL] * _dot(od_ref[...], wbr_ref[0])
    merged = merged + gates[:, D_MODEL:2 * D_MODEL] * _dot(om_ref[...], wbr_ref[1])
    merged = merged + gates[:, 2 * D_MODEL:3 * D_MODEL] * _dot(oc_ref[...], wbr_ref[2])
    xm = x + _dot(merged.astype(BF16), wout_ref[...])
    xm_ref[...] = xm
    h2 = _rms_rows(xm) * gffn_ref[...]
    for s in range(D_MODEL // LANES):
        h2_ref[:, s, :] = h2[:, s * LANES:(s + 1) * LANES]
    hh = h2.astype(BF16)
    hl = (h2 - hh.astype(F32)).astype(BF16)
    logits = _dot(hh, wrh_ref[...]) + _dot(hl, wrh_ref[...]) + _dot(hh, wrl_ref[...]) + br_ref[...]
    lane = lax.broadcasted_iota(jnp.int32, logits.shape, 1)
    logits = jnp.where(lane < N_EXPERTS, logits, -jnp.inf)
    tv = jnp.zeros(logits.shape, F32)
    ti = jnp.zeros(logits.shape, jnp.int32)
    vals = []
    for k in range(TOP_K):
        mx = jnp.max(logits, axis=-1, keepdims=True)
        idx = jnp.min(jnp.where(logits == mx, lane, LANES), axis=-1, keepdims=True)
        vals.append(mx)
        ti = jnp.where(lane == k, idx, ti)
        logits = jnp.where(lane == idx, -jnp.inf, logits)
    es = [jnp.exp(v - vals[0]) for v in vals]
    den = es[0] + es[1] + es[2] + es[3]
    for k in range(TOP_K):
        tv = jnp.where(lane == k, es[k] / den, tv)
    tv_ref[...] = tv
    ti_ref[...] = ti


def _merge(x2d, od, om, oc, wts):
    n = x2d.shape[0]
    tm = min(TOKEN_TILE, n)
    row = lambda w: pl.BlockSpec((tm, w), lambda i: (i, 0))
    return pl.pallas_call(
        _merge_kernel,
        grid=(n // tm,),
        in_specs=[row(D_MODEL), row(512), row(512), row(512)] + [_full(a) for a in wts],
        out_specs=[row(D_MODEL), pl.BlockSpec((tm, 8, LANES), lambda i: (i, 0, 0)), row(LANES), row(LANES)],
        out_shape=[jax.ShapeDtypeStruct((n, D_MODEL), F32), jax.ShapeDtypeStruct((n, 8, LANES), F32),
                   jax.ShapeDtypeStruct((n, LANES), F32), jax.ShapeDtypeStruct((n, LANES), jnp.int32)],
        compiler_params=_params(("parallel",)),
        name="merge",
    )(x2d, od, om, oc, *wts)


def _expert_kernel(be_ref, first_ref, nval_ref, slot_ref, wgu_ref, bgu_ref, wd_ref, bd_ref, x_hbm, y_hbm,
                   slot_smem, xbuf, ybuf, wgu16, wd16, sem):
    i = pl.program_id(0)
    nv = nval_ref[i]

    @pl.when(i == 0)
    def _():
        xbuf[...] = jnp.zeros(xbuf.shape, F32)

    @pl.when(nv > 0)
    def _():
        cp = pltpu.make_async_copy(slot_ref.at[0], slot_smem, sem.at[0])
        cp.start()
        cp.wait()

        def gather(r, carry):
            tok = lax.shift_right_logical(slot_smem[0, r], 2)
            pltpu.make_async_copy(x_hbm.at[tok], xbuf.at[r], sem.at[1]).start()
            return carry

        lax.fori_loop(0, nv, gather, 0)

        @pl.when(first_ref[i] == 1)
        def _():
            wgu16[...] = wgu_ref[...].astype(BF16)
            wd16[...] = wd_ref[...].astype(BF16)

        def gwait(r, carry):
            pltpu.make_async_copy(x_hbm.at[0], xbuf.at[r], sem.at[1]).wait()
            return carry

        lax.fori_loop(0, nv, gwait, 0)

        x = jnp.concatenate([xbuf[:, s, :] for s in range(D_MODEL // LANES)], axis=-1).astype(BF16)
        gu = _dot(x, wgu16[...]) + bgu_ref[...]
        gate = jnp.minimum(gu[:, 0:D_EXPERT], SWIGLU_LIMIT)
        up = jnp.clip(gu[:, D_EXPERT:2 * D_EXPERT], -SWIGLU_LIMIT, SWIGLU_LIMIT)
        act = (up + 1.0) * gate * jax.nn.sigmoid(SWIGLU_ALPHA * gate)
        y = _dot(act.astype(BF16), wd16[...]) + bd_ref[...]
        for s in range(D_MODEL // LANES):
            ybuf[:, s, :] = y[:, s * LANES:(s + 1) * LANES]

        def scatter(r, carry):
            pltpu.make_async_copy(ybuf.at[r], y_hbm.at[slot_smem[0, r]], sem.at[2]).start()
            return carry

        lax.fori_loop(0, nv, scatter, 0)

        def swait(r, carry):
            pltpu.make_async_copy(ybuf.at[r], y_hbm.at[0], sem.at[2]).wait()
            return carry

        lax.fori_loop(0, nv, swait, 0)


def _experts(block_e, first, nval, row_slot3, wgu, bgu, wd, bd, x3, n_slots):
    n_blocks = row_slot3.shape[0]
    bm = row_slot3.shape[2]
    kern = _expert_kernel
    grid_spec = pltpu.PrefetchScalarGridSpec(
        num_scalar_prefetch=3,
        grid=(n_blocks,),
        in_specs=[pl.BlockSpec((1, 1, bm), lambda i, be, fi, na: (i, 0, 0)),
                  pl.BlockSpec((None, D_MODEL, 2 * D_EXPERT), lambda i, be, fi, na: (be[i], 0, 0)),
                  pl.BlockSpec((None, 1, 2 * D_EXPERT), lambda i, be, fi, na: (be[i], 0, 0)),
                  pl.BlockSpec((None, D_EXPERT, D_MODEL), lambda i, be, fi, na: (be[i], 0, 0)),
                  pl.BlockSpec((None, 1, D_MODEL), lambda i, be, fi, na: (be[i], 0, 0)),
                  pl.BlockSpec(memory_space=pl.ANY)],
        out_specs=pl.BlockSpec(memory_space=pl.ANY),
        scratch_shapes=[pltpu.SMEM((1, bm), jnp.int32),
                        pltpu.VMEM((bm, 8, LANES), F32), pltpu.VMEM((bm, 8, LANES), F32),
                        pltpu.VMEM((D_MODEL, 2 * D_EXPERT), BF16), pltpu.VMEM((D_EXPERT, D_MODEL), BF16),
                        pltpu.SemaphoreType.DMA((3,))],
    )
    return pl.pallas_call(
        kern,
        grid_spec=grid_spec,
        out_shape=jax.ShapeDtypeStruct((n_slots, 8, LANES), F32),
        compiler_params=_params(("arbitrary",)),
        name="experts",
    )(block_e, first, nval, row_slot3, wgu, bgu, wd, bd, x3)


def _combine_kernel(xm_ref, tv_ref, y_ref, o_ref):
    tv = tv_ref[...]
    for s in range(D_MODEL // LANES):
        acc = xm_ref[:, s * LANES:(s + 1) * LANES]
        for k in range(TOP_K):
            acc = acc + tv[:, k:k + 1] * y_ref[:, k * 8 + s, :]
        o_ref[:, s * LANES:(s + 1) * LANES] = acc


def _combine(xm, tv, yslots, tok_off):
    n = xm.shape[0]
    tm = min(TOKEN_TILE, n)
    off = tok_off // tm
    y4 = yslots.reshape(yslots.shape[0] // TOP_K, TOP_K * 8, LANES)
    return pl.pallas_call(
        _combine_kernel,
        grid=(n // tm,),
        in_specs=[pl.BlockSpec((tm, D_MODEL), lambda i: (i, 0)),
                  pl.BlockSpec((tm, LANES), lambda i: (i, 0)),
                  pl.BlockSpec((tm, TOP_K * 8, LANES), lambda i: (i + off, 0, 0))],
        out_specs=pl.BlockSpec((tm, D_MODEL), lambda i: (i, 0)),
        out_shape=jax.ShapeDtypeStruct((n, D_MODEL), F32),
        compiler_params=_params(("parallel",)),
        name="combine",
    )(xm, tv, y4)


def _rope_tables(pos, rows):
    pos = pos.astype(F32)[:, None]
    lane = jnp.arange(LANES)

    def ang(half):
        inv = jnp.power(ROPE_THETA, -jnp.arange(half, dtype=F32) / half)
        return pos * inv[None, :]

    a64 = ang(32)[:, lane % 32]
    first = (lane % 64) < 32
    c64 = jnp.cos(a64)
    sa64 = jnp.where(first, -jnp.sin(a64), 0.0)
    sb64 = jnp.where(first, 0.0, jnp.sin(a64))
    a32 = ang(16)[:, lane % 16]
    in_a = (lane >= 64) & (lane < 80)
    in_b = (lane >= 80) & (lane < 96)
    c32 = jnp.where(in_a | in_b, jnp.cos(a32), 1.0)
    sa32 = jnp.where(in_a, -jnp.sin(a32), 0.0)
    sb32 = jnp.where(in_b, jnp.sin(a32), 0.0)
    tabs = [c64, sa64, sb64, c32, sa32, sb32]
    reps = rows // pos.shape[0]
    return [jnp.tile(t, (reps, 1)) if reps > 1 else t for t in tabs]


def _tied_pad(g, scale):
    blk = jnp.concatenate([g[:MLA_NOPE], g[MLA_NOPE:], g[MLA_NOPE:], jnp.zeros((32,), F32)]) * scale
    return jnp.tile(blk, MLA_HEADS)[None, :]


def _layer_weights(lp):
    w_in = lp["w_in"]
    kr_blk = jnp.zeros((D_MODEL, LANES), F32).at[:, 64:96].set(w_in[:, _OFF_KR:_OFF_MQ])
    wa = jnp.concatenate([w_in[:, _OFF_Q:_OFF_KR], kr_blk, w_in[:, _OFF_MQ:_OFF_G]], axis=1).astype(BF16)
    wuq = jnp.pad(lp["w_mla_uq"], ((0, 0), (0, 0), (0, LANES - MLA_QK))).reshape(MLA_Q_LORA, MLA_HEADS * LANES).astype(BF16)
    wuk_pad = jnp.pad(lp["w_mla_uk"], ((0, 0), (0, 0), (0, LANES - MLA_NOPE))).reshape(MLA_KV_LORA, MLA_HEADS * LANES)
    lane = jnp.arange(LANES)
    rope_eye = jnp.where(((lane >= 64) & (lane < 96))[:, None], jnp.eye(LANES, dtype=F32), 0.0)
    wk = jnp.concatenate([wuk_pad, jnp.tile(rope_eye, (1, MLA_HEADS))], axis=0).astype(BF16)
    g64 = jnp.kron(jnp.eye(4, dtype=F32), jnp.ones((64, 64), F32)).astype(BF16)
    g128 = jnp.kron(jnp.eye(2, dtype=F32), jnp.ones((128, 128), F32)).astype(BF16)
    gk_pad = _tied_pad(lp["mla_k_norm"], 1.0)
    inproj_w = [lp["norm_mix"][None, :], wa, wuq, wk, g64, g128,
                jnp.tile(lp["diff_q_norm"], 8)[None, :] * DIFF_SCALE, jnp.tile(lp["diff_k_norm"], 4)[None, :],
                lp["mla_q_a_norm"][None, :], lp["mla_kv_a_norm"][None, :],
                _tied_pad(lp["mla_q_norm"], MLA_SCALE), gk_pad,
                jnp.tile(lp["mem_q_norm"], MEM_HEADS)[None, :] * MEM_SCALE]
    wuv = lp["w_mla_uv"]
    wuv2 = jnp.einsum("rhd,hg->hrgd", wuv, jnp.eye(MLA_HEADS, dtype=F32)).reshape(MLA_HEADS, MLA_KV_LORA, 512).astype(BF16)
    wuk_t = jnp.transpose(lp["w_mla_uk"], (1, 2, 0))
    aq = jnp.zeros((MLA_HEADS, LANES, 2 * LANES), F32).at[:, 0:MLA_NOPE, 0:LANES].set(wuk_t)
    aq = aq.at[:, :, LANES:].add(rope_eye[None]).astype(BF16)
    wukt = wuk_t.reshape(MLA_HEADS * MLA_NOPE, MLA_KV_LORA).astype(BF16)
    wr = jnp.pad(lp["w_router"], ((0, 0), (0, LANES - N_EXPERTS)))
    wrh = wr.astype(BF16)
    wrl = (wr - wrh.astype(F32)).astype(BF16)
    merge_w = [lp["norm_mix"][None, :], w_in[:, _OFF_G:].astype(BF16), lp["b_gate"][None, :],
               lp["w_branch"].reshape(3, 512, D_MODEL).astype(BF16), lp["w_out"].astype(BF16),
               lp["norm_ffn"][None, :], wrh, wrl, jnp.pad(lp["b_router"], (0, LANES - N_EXPERTS))[None, :]]
    return dict(inproj=inproj_w, g128=g128, wuv2=wuv2, aq=aq, wukt=wukt, gk_pad=gk_pad, merge=merge_w)


def _moe_plan(ti, bm):
    n = ti.shape[0]
    a = n * TOP_K
    flat_e = ti[:, :TOP_K].reshape(a)
    onehot = (flat_e[:, None] == jnp.arange(N_EXPERTS, dtype=jnp.int32)[None, :]).astype(jnp.int32)
    csum = jnp.cumsum(onehot, axis=0)
    counts = csum[-1]
    rank = jnp.sum(onehot * csum, axis=1) - 1
    padded = (counts + bm - 1) // bm * bm
    pend = jnp.cumsum(padded)
    pstart = pend - padded
    dest = jnp.sum(onehot * pstart[None, :], axis=1) + rank
    n_rows = (a + N_EXPERTS * (bm - 1) + bm - 1) // bm * bm
    n_blocks = n_rows // bm
    row_slot = jnp.zeros((n_rows,), jnp.int32).at[dest].set(jnp.arange(a, dtype=jnp.int32))
    blk_start = jnp.arange(n_blocks, dtype=jnp.int32) * bm
    block_e = jnp.minimum(jnp.sum((blk_start[:, None] >= pend[None, :]).astype(jnp.int32), axis=1), N_EXPERTS - 1)
    first = jnp.concatenate([jnp.ones((1,), jnp.int32), (block_e[1:] != block_e[:-1]).astype(jnp.int32)])
    eh = (block_e[:, None] == jnp.arange(N_EXPERTS, dtype=jnp.int32)[None, :]).astype(jnp.int32)
    valid_end = jnp.sum(eh * (pstart + counts)[None, :], axis=1)
    nval = jnp.where(blk_start < pend[-1], jnp.clip(valid_end - blk_start, 0, bm), 0).astype(jnp.int32)
    return block_e, first, nval, row_slot.reshape(n_blocks, 1, bm), a


def _qbd(dq_s, b, nq):
    q = dq_s.reshape(b, nq, 2, 2, 2, DIFF_HEAD_DIM)
    q = jnp.transpose(q, (0, 2, 4, 3, 1, 5))
    eye = jnp.eye(4, dtype=q.dtype).reshape(2, 2, 2, 2)
    out = jnp.einsum("bncgqd,ncmk->bncgqmkd", q, eye)
    return out.reshape(b, 64, 256)


def kernel(x_prompt, x_sample, mem_prompt, cache_diff_k, cache_diff_v, cache_mla_ckv, cache_mla_krope, cache_mem_k, cache_mem_v, page_table, norm_mix, norm_mem, w_in, b_gate, diff_q_norm, diff_k_norm, diff_lambda, diff_subln, mla_q_a_norm, w_mla_uq, mla_kv_a_norm, w_mla_uk, w_mla_uv, mla_q_norm, mla_k_norm, w_mem_kv, mem_q_norm, mem_k_norm, w_branch, w_out, norm_ffn, w_router, b_router, w_gate_up, b_gate_up, w_down, b_down):
    depth = w_in.shape[0]
    bp, t, _ = x_prompt.shape
    bs, nq, _ = x_sample.shape
    n_pool, n_pages = cache_diff_k.shape[1], page_table.shape[1]
    past_len = n_pages * PAGE
    n_p, n_s = bp * t, bs * nq

    tm_p, tm_s = min(TOKEN_TILE, n_p), min(TOKEN_TILE, n_s)
    tabs_p = _rope_tables(jnp.arange(t, dtype=jnp.int32), t)
    tabs_s = _rope_tables(past_len + jnp.arange(nq, dtype=jnp.int32), tm_s)
    del tm_p

    kc = jnp.transpose(cache_diff_k, (0, 1, 3, 4, 5, 2)).reshape(depth * n_pool, 256, PAGE)
    vc = cache_diff_v.reshape(depth * n_pool, 2 * PAGE, DIFF_V_DIM)
    cc = cache_mla_ckv.reshape(depth * n_pool, PAGE, MLA_KV_LORA)
    rc = jnp.transpose(cache_mla_krope, (0, 1, 3, 2)).reshape(depth * n_pool, MLA_ROPE, PAGE)
    mkc = cache_mem_k.reshape(depth * bs, N_MEM * MEM_HEADS, MEM_HEAD_DIM)
    mvc = cache_mem_v.reshape(depth * bs, N_MEM * MEM_HEADS, MEM_HEAD_DIM)

    xp = x_prompt.reshape(n_p, D_MODEL)
    xs = x_sample.reshape(n_s, D_MODEL)
    outs = [[] for _ in range(10)]
    for layer in range(depth):
        lp = dict(norm_mix=norm_mix[layer], w_in=w_in[layer], b_gate=b_gate[layer], diff_q_norm=diff_q_norm[layer],
                  diff_k_norm=diff_k_norm[layer], mla_q_a_norm=mla_q_a_norm[layer], w_mla_uq=w_mla_uq[layer],
                  mla_kv_a_norm=mla_kv_a_norm[layer], w_mla_uk=w_mla_uk[layer], w_mla_uv=w_mla_uv[layer],
                  mla_q_norm=mla_q_norm[layer], mla_k_norm=mla_k_norm[layer], mem_q_norm=mem_q_norm[layer],
                  w_branch=w_branch[layer], w_out=w_out[layer], norm_ffn=norm_ffn[layer],
                  w_router=w_router[layer], b_router=b_router[layer])
        w = _layer_weights(lp)
        lam_init = 0.8 - 0.6 * math.exp(-0.3 * layer)
        lamp = diff_lambda[layer].astype(F32)
        lam = (jnp.exp(jnp.sum(lamp[0] * lamp[1])) - jnp.exp(jnp.sum(lamp[2] * lamp[3])) + lam_init).reshape(1)
        gs = (diff_subln[layer] * (1.0 - lam_init))[None, :]
        pt_flat = (page_table + layer * n_pool).reshape(-1).astype(jnp.int32)

        (dq, dk32, dk16, dv32, dv16, mq, ckv32, ckv16, kr32, kmla, memq) = _inproj(xp, tabs_p, w["inproj"])
        o_diff = _diff_prompt(lam, dq.reshape(bp, t, 512), dk16.reshape(bp, t, 256), dv16.reshape(bp, t, 256), gs)
        o_mla = _mla_prompt(mq.reshape(bp, t, 1024), kmla.reshape(bp, t, 1024), ckv16.reshape(bp, t, LANES), w["wuv2"])
        mk32, mk16, mv32, mv16 = _memkv(mem_prompt.reshape(bp * N_MEM, D_MODEL), norm_mem[layer][None, :],
                                        w_mem_kv[layer].astype(BF16), w["g128"],
                                        jnp.tile(mem_k_norm[layer], MEM_HEADS)[None, :])
        o_mem = _mem_prompt(memq.reshape(bp, t, 512), mk16.reshape(bp, N_MEM, 512), mv16.reshape(bp, N_MEM, 512))
        xm_p, h2_p, tv_p, ti_p = _merge(xp, o_diff.reshape(n_p, 512), o_mla.reshape(n_p, 512),
                                        o_mem.reshape(n_p, 512), w["merge"])
        for lst, val in zip(outs[:6], (dk32.reshape(bp, t, 2, 2, 64), dv32.reshape(bp, t, 2, 128),
                                       ckv32.reshape(bp, t, 128), kr32.reshape(bp, t, 32),
                                       mk32.reshape(bp, N_MEM, 4, 128), mv32.reshape(bp, N_MEM, 4, 128))):
            lst.append(val)

        (dq, dk32, dk16, dv32, dv16, mq, ckv32, ckv16, kr32, kmla, memq) = _inproj(xs, tabs_s, w["inproj"])
        o_diff = _diff_sample(pt_flat, lam, _qbd(dq, bs, nq), dk16.reshape(bs, nq, 256), dv16.reshape(bs, nq, 256),
                              gs, kc, vc, n_pages)
        o_mla = _mla_sample(pt_flat, mq.reshape(bs, nq, 1024), kmla.reshape(bs, nq, 1024),
                            ckv16.reshape(bs, nq, LANES), w["gk_pad"], w["aq"], w["wukt"], w["wuv2"], cc, rc, n_pages)
        o_mem = _mem_sample(memq.reshape(bs, nq, 512), mkc, mvc, layer * bs)
        xm_s, h2_s, tv_s, ti_s = _merge(xs, o_diff.reshape(n_s, 512), o_mla.reshape(n_s, 512),
                                        o_mem.reshape(n_s, 512), w["merge"])
        for lst, val in zip(outs[6:], (dk32.reshape(bs, nq, 2, 2, 64), dv32.reshape(bs, nq, 2, 128),
                                       ckv32.reshape(bs, nq, 128), kr32.reshape(bs, nq, 32))):
            lst.append(val)

        h2 = jnp.concatenate([h2_p, h2_s], axis=0)
        ti = jnp.concatenate([ti_p, ti_s], axis=0)
        block_e, first, nval, row_slot3, n_slots = _moe_plan(ti, MOE_ROWS)
        yslots = _experts(block_e + layer * N_EXPERTS, first, nval, row_slot3,
                          w_gate_up.reshape(depth * N_EXPERTS, D_MODEL, 2 * D_EXPERT),
                          b_gate_up.reshape(depth * N_EXPERTS, 1, 2 * D_EXPERT),
                          w_down.reshape(depth * N_EXPERTS, D_EXPERT, D_MODEL),
                          b_down.reshape(depth * N_EXPERTS, 1, D_MODEL), h2, n_slots)
        xp = _combine(xm_p, tv_p, yslots, 0)
        xs = _combine(xm_s, tv_s, yslots, n_p)

    stack = lambda lst: jnp.stack(lst)
    return (xp.reshape(bp, t, D_MODEL), xs.reshape(bs, nq, D_MODEL)) + tuple(stack(o) for o in outs)
```

```python
import functools
import math

import jax
import jax.numpy as jnp
from jax import lax
from jax.experimental import pallas as pl
from jax.experimental.pallas import tpu as pltpu

F32 = jnp.float32
BF16 = jnp.bfloat16

D_MODEL = 1024
DIFF_HEAD_DIM = 64
DIFF_V_DIM = 128
MLA_HEADS = 8
MLA_Q_LORA = 256
MLA_KV_LORA = 128
MLA_NOPE = 64
MLA_ROPE = 32
MLA_V = 64
MLA_QK = MLA_NOPE + MLA_ROPE
MEM_HEADS = 4
MEM_HEAD_DIM = 128
N_MEM = 256
N_EXPERTS = 32
TOP_K = 4
D_EXPERT = 1024
SWIGLU_ALPHA = 1.702
SWIGLU_LIMIT = 7.0
ROPE_THETA = 10000.0
NORM_EPS = 1e-6
SUBLN_EPS = 1e-5
NEG_INF = -1e30
PAGE = 128

DIFF_SCALE = DIFF_HEAD_DIM ** -0.5
MLA_SCALE = MLA_QK ** -0.5
MEM_SCALE = MEM_HEAD_DIM ** -0.5

LANES = 128
MXU_DIM = 256
VMEM_LIMIT = 52 * 1024 * 1024
TOKEN_TILE = 256
ATTN_TILE = 256
PAGES_PER_STEP = 8
SEQS_PER_STEP = 2
MOE_ROWS = 256
ROW_GROUP = 8

_OFF_Q, _OFF_K, _OFF_V, _OFF_CQ, _OFF_CKV, _OFF_KR, _OFF_MQ, _OFF_G = 0, 512, 768, 1024, 1280, 1408, 1440, 1952


def _params(sem):
    return pltpu.CompilerParams(dimension_semantics=sem, vmem_limit_bytes=VMEM_LIMIT)


def _full(a):
    nd = a.ndim
    return pl.BlockSpec(a.shape, lambda *_: (0,) * nd)


def _dot(a, b):
    return jnp.dot(a, b, preferred_element_type=F32)


def _dot_nt(a, b):
    return lax.dot_general(a, b, (((1,), (1,)), ((), ())), preferred_element_type=F32)


def _rms_rows(x, eps=NORM_EPS):
    return x * lax.rsqrt(jnp.mean(x * x, axis=-1, keepdims=True) + eps)


def _group_sumsq(v, g_ref):
    sq = (v * v).astype(BF16)
    parts = [_dot(sq[:, j * MXU_DIM:(j + 1) * MXU_DIM], g_ref[...]) for j in range(v.shape[1] // MXU_DIM)]
    return parts[0] if len(parts) == 1 else jnp.concatenate(parts, axis=-1)


def _rope_lanes(v, c, sa, sb, half):
    outs = []
    for j in range(v.shape[1] // LANES):
        b = v[:, j * LANES:(j + 1) * LANES]
        outs.append(b * c + pltpu.roll(b, LANES - half, 1) * sa + pltpu.roll(b, half, 1) * sb)
    return outs[0] if len(outs) == 1 else jnp.concatenate(outs, axis=-1)


def _inproj_kernel(x_ref, gmix_ref, wa_ref, wuq_ref, wk_ref, g64_ref, g128_ref,
                   gq_ref, gk_ref, gcq_ref, gckv_ref, gmq_ref, gkm_ref, gmemq_ref,
                   c64_ref, sa64_ref, sb64_ref, c32_ref, sa32_ref, sb32_ref,
                   dq_ref, dk32_ref, dk16_ref, dv32_ref, dv16_ref, mq_ref,
                   ckv32_ref, ckv16_ref, kr32_ref, kmla_ref, memq_ref):
    x = x_ref[...]
    h = (_rms_rows(x) * gmix_ref[...]).astype(BF16)
    z = _dot(h, wa_ref[...])
    c64, sa64, sb64 = c64_ref[...], sa64_ref[...], sb64_ref[...]
    c32, sa32, sb32 = c32_ref[...], sa32_ref[...], sb32_ref[...]

    zq = z[:, 0:512]
    qn = zq * lax.rsqrt(_group_sumsq(zq, g64_ref) * (1.0 / DIFF_HEAD_DIM) + NORM_EPS) * gq_ref[...]
    dq_ref[...] = _rope_lanes(qn, c64, sa64, sb64, 32).astype(BF16)

    zk = z[:, 512:768]
    kn = zk * lax.rsqrt(_group_sumsq(zk, g64_ref) * (1.0 / DIFF_HEAD_DIM) + NORM_EPS) * gk_ref[...]
    dk = _rope_lanes(kn, c64, sa64, sb64, 32)
    dk32_ref[...] = dk
    dk16_ref[...] = dk.astype(BF16)

    dv = z[:, 768:1024]
    dv32_ref[...] = dv
    dv16_ref[...] = dv.astype(BF16)

    cq = (_rms_rows(z[:, 1024:1280]) * gcq_ref[...]).astype(BF16)
    mqr = _dot(cq, wuq_ref[...])
    mqn = mqr * lax.rsqrt(_group_sumsq(mqr, g128_ref) * (1.0 / MLA_QK) + NORM_EPS) * gmq_ref[...]
    mq_ref[...] = _rope_lanes(mqn, c32, sa32, sb32, 16).astype(BF16)

    ckv = _rms_rows(z[:, 1280:1408]) * gckv_ref[...]
    ckv32_ref[...] = ckv
    ckv16 = ckv.astype(BF16)
    ckv16_ref[...] = ckv16

    krb = _rope_lanes(z[:, 1408:1536], c32, sa32, sb32, 16)
    kr32_ref[...] = krb[:, 64:96]
    kin = jnp.concatenate([ckv16, krb.astype(BF16)], axis=-1)
    kraw = _dot(kin, wk_ref[...])
    kmla_ref[...] = (kraw * lax.rsqrt(_group_sumsq(kraw, g128_ref) * (1.0 / MLA_QK) + NORM_EPS)
                     * gkm_ref[...]).astype(BF16)

    zm = z[:, 1536:2048]
    memq_ref[...] = (zm * lax.rsqrt(_group_sumsq(zm, g128_ref) * (1.0 / MEM_HEAD_DIM) + NORM_EPS)
                     * gmemq_ref[...]).astype(BF16)


def _inproj(x2d, tabs, wts):
    n = x2d.shape[0]
    tm = min(TOKEN_TILE, n)
    period = tabs[0].shape[0] // tm
    row = lambda w: pl.BlockSpec((tm, w), lambda i: (i, 0))
    tab = pl.BlockSpec((tm, LANES), lambda i: (i % period, 0))
    out_w = [(512, BF16), (256, F32), (256, BF16), (256, F32), (256, BF16), (1024, BF16),
             (128, F32), (128, BF16), (32, F32), (1024, BF16), (512, BF16)]
    return pl.pallas_call(
        _inproj_kernel,
        grid=(n // tm,),
        in_specs=[row(D_MODEL)] + [_full(a) for a in wts] + [tab] * 6,
        out_specs=[row(w) for w, _ in out_w],
        out_shape=[jax.ShapeDtypeStruct((n, w), dt) for w, dt in out_w],
        compiler_params=_params(("parallel",)),
        name="inproj",
    )(x2d, *wts, *tabs)


def _memkv_kernel(x_ref, g_ref, w_ref, g128_ref, gk_ref, k32_ref, k16_ref, v32_ref, v16_ref):
    h = (_rms_rows(x_ref[...]) * g_ref[...]).astype(BF16)
    kv = _dot(h, w_ref[...])
    k = kv[:, 0:512]
    k = k * lax.rsqrt(_group_sumsq(k, g128_ref) * (1.0 / MEM_HEAD_DIM) + NORM_EPS) * gk_ref[...]
    v = kv[:, 512:1024]
    k32_ref[...] = k
    k16_ref[...] = k.astype(BF16)
    v32_ref[...] = v
    v16_ref[...] = v.astype(BF16)


def _memkv(mem2d, g, w, g128, gk):
    n = mem2d.shape[0]
    tm = min(TOKEN_TILE, n)
    row = lambda w_: pl.BlockSpec((tm, w_), lambda i: (i, 0))
    return pl.pallas_call(
        _memkv_kernel,
        grid=(n // tm,),
        in_specs=[row(D_MODEL), _full(g), _full(w), _full(g128), _full(gk)],
        out_specs=[row(512)] * 4,
        out_shape=[jax.ShapeDtypeStruct((n, 512), dt) for dt in (F32, BF16, F32, BF16)],
        compiler_params=_params(("parallel",)),
        name="memkv",
    )(mem2d, g, w, g128, gk)


def _online_update(s, m_ref, l_ref, rows=None):
    sl = slice(None) if rows is None else rows
    m_old = m_ref[sl, :]
    m_new = jnp.maximum(m_old, jnp.max(s, axis=-1, keepdims=True))
    alpha = jnp.exp(m_old - m_new)
    p = jnp.exp(s - m_new)
    l_ref[sl, :] = alpha * l_ref[sl, :] + jnp.sum(p, axis=-1, keepdims=True)
    m_ref[sl, :] = m_new
    return p, alpha


def _diff_prompt_kernel(lam_ref, q_ref, k_ref, v_ref, gs_ref, o_ref, q4_ref, m_ref, l_ref, acc_ref, *, tq):
    i = pl.program_id(2)
    lane = lax.broadcasted_iota(jnp.int32, (tq, LANES), 1)
    zero = jnp.zeros((tq, LANES), BF16)
    for g in range(2):
        qg = q_ref[:, g * LANES:(g + 1) * LANES]
        q4_ref[g * tq:(g + 1) * tq, :] = jnp.where(lane < DIFF_HEAD_DIM, qg, zero)
        q4_ref[(2 + g) * tq:(3 + g) * tq, :] = jnp.where(lane >= DIFF_HEAD_DIM, qg, zero)
    m_ref[...] = jnp.full(m_ref.shape, NEG_INF, F32)
    l_ref[...] = jnp.zeros(l_ref.shape, F32)
    acc_ref[...] = jnp.zeros(acc_ref.shape, F32)

    def step(j, masked):
        off = pl.multiple_of(j * tq, tq)
        k = k_ref[pl.ds(off, tq), :]
        v = v_ref[pl.ds(off, tq), :]
        s = _dot_nt(q4_ref[...], k)
        if masked:
            qpos = lax.broadcasted_iota(jnp.int32, s.shape, 0) & (tq - 1)
            kpos = lax.broadcasted_iota(jnp.int32, s.shape, 1)
            s = jnp.where(kpos <= qpos, s, NEG_INF)
        p, alpha = _online_update(s, m_ref, l_ref)
        acc_ref[...] = alpha * acc_ref[...] + _dot(p.astype(BF16), v)

    def body(j, carry):
        step(j, False)
        return carry

    lax.fori_loop(0, i, body, 0)
    step(i, True)

    o = acc_ref[...] / l_ref[...]
    od = o[0:2 * tq] - lam_ref[0] * o[2 * tq:4 * tq]
    od = _rms_rows(od, SUBLN_EPS) * gs_ref[...]
    for g in range(2):
        o_ref[:, g * LANES:(g + 1) * LANES] = od[g * tq:(g + 1) * tq].astype(BF16)


def _diff_prompt(lam, dq, dk16, dv16, gs):
    b, t, _ = dq.shape
    tq = min(ATTN_TILE, t)
    kern = functools.partial(_diff_prompt_kernel, tq=tq)
    return pl.pallas_call(
        kern,
        grid=(b, 2, t // tq),
        in_specs=[pl.BlockSpec(memory_space=pltpu.SMEM),
                  pl.BlockSpec((None, tq, 256), lambda bi, n, i: (bi, i, n)),
                  pl.BlockSpec((None, t, LANES), lambda bi, n, i: (bi, 0, n)),
                  pl.BlockSpec((None, t, LANES), lambda bi, n, i: (bi, 0, n)),
                  pl.BlockSpec((1, LANES), lambda bi, n, i: (0, 0))],
        out_specs=pl.BlockSpec((None, tq, 256), lambda bi, n, i: (bi, i, n)),
        out_shape=jax.ShapeDtypeStruct((b, t, 512), BF16),
        scratch_shapes=[pltpu.VMEM((4 * tq, LANES), BF16), pltpu.VMEM((4 * tq, 1), F32),
                        pltpu.VMEM((4 * tq, 1), F32), pltpu.VMEM((4 * tq, LANES), F32)],
        compiler_params=_params(("parallel", "parallel", "parallel")),
        name="diff_prompt",
    )(lam, dq, dk16, dv16, gs)


def _mla_prompt_kernel(q_ref, k_ref, c_ref, wuv_ref, o_ref, p_ref, al_ref, m_ref, l_ref, acc_ref, *, tq):
    i = pl.program_id(1)
    m_ref[...] = jnp.full(m_ref.shape, NEG_INF, F32)
    l_ref[...] = jnp.zeros(l_ref.shape, F32)
    acc_ref[...] = jnp.zeros(acc_ref.shape, F32)

    def step(j, masked):
        off = pl.multiple_of(j * tq, tq)
        for h in range(MLA_HEADS):
            rows = slice(h * tq, (h + 1) * tq)
            s = _dot_nt(q_ref[:, h * LANES:(h + 1) * LANES], k_ref[pl.ds(off, tq), h * LANES:(h + 1) * LANES])
            if masked:
                qpos = lax.broadcasted_iota(jnp.int32, s.shape, 0)
                kpos = lax.broadcasted_iota(jnp.int32, s.shape, 1)
                s = jnp.where(kpos <= qpos, s, NEG_INF)
            p, alpha = _online_update(s, m_ref, l_ref, rows)
            p_ref[rows, :] = p.astype(BF16)
            al_ref[rows, :] = alpha
        acc_ref[...] = al_ref[...] * acc_ref[...] + _dot(p_ref[...], c_ref[pl.ds(off, tq), :])

    def body(j, carry):
        step(j, False)
        return carry

    lax.fori_loop(0, i, body, 0)
    step(i, True)

    lat = (acc_ref[...] / l_ref[...]).astype(BF16)
    o = _dot(lat[0:tq], wuv_ref[0])
    for h in range(1, MLA_HEADS):
        o = o + _dot(lat[h * tq:(h + 1) * tq], wuv_ref[h])
    o_ref[...] = o.astype(BF16)


def _mla_prompt(mq, kmla, ckv16, wuv2):
    b, t, _ = mq.shape
    tq = min(ATTN_TILE, t)
    kern = functools.partial(_mla_prompt_kernel, tq=tq)
    return pl.pallas_call(
        kern,
        grid=(b, t // tq),
        in_specs=[pl.BlockSpec((None, tq, 1024), lambda bi, i: (bi, i, 0)),
                  pl.BlockSpec((None, t, 1024), lambda bi, i: (bi, 0, 0)),
                  pl.BlockSpec((None, t, LANES), lambda bi, i: (bi, 0, 0)),
                  pl.BlockSpec(wuv2.shape, lambda bi, i: (0, 0, 0))],
        out_specs=pl.BlockSpec((None, tq, 512), lambda bi, i: (bi, i, 0)),
        out_shape=jax.ShapeDtypeStruct((b, t, 512), BF16),
        scratch_shapes=[pltpu.VMEM((MLA_HEADS * tq, tq), BF16), pltpu.VMEM((MLA_HEADS * tq, 1), F32),
                        pltpu.VMEM((MLA_HEADS * tq, 1), F32), pltpu.VMEM((MLA_HEADS * tq, 1), F32),
                        pltpu.VMEM((MLA_HEADS * tq, LANES), F32)],
        compiler_params=_params(("parallel", "parallel")),
        name="mla_prompt",
    )(mq, kmla, ckv16, wuv2)


def _softmax_pv(s, v):
    m = jnp.max(s, axis=-1, keepdims=True)
    p = jnp.exp(s - m)
    l = jnp.sum(p, axis=-1, keepdims=True)
    return _dot(p.astype(BF16), v) / l


def _mem_prompt_kernel(q_ref, k_ref, v_ref, o_ref):
    for h in range(MEM_HEADS):
        sl = slice(h * LANES, (h + 1) * LANES)
        s = _dot_nt(q_ref[:, sl], k_ref[:, sl])
        o_ref[:, sl] = _softmax_pv(s, v_ref[:, sl]).astype(BF16)


def _mem_prompt(memq, mk16, mv16):
    b, t, _ = memq.shape
    tq = min(2 * ATTN_TILE, t)
    return pl.pallas_call(
        _mem_prompt_kernel,
        grid=(b, t // tq),
        in_specs=[pl.BlockSpec((None, tq, 512), lambda bi, i: (bi, i, 0)),
                  pl.BlockSpec((None, N_MEM, 512), lambda bi, i: (bi, 0, 0)),
                  pl.BlockSpec((None, N_MEM, 512), lambda bi, i: (bi, 0, 0))],
        out_specs=pl.BlockSpec((None, tq, 512), lambda bi, i: (bi, i, 0)),
        out_shape=jax.ShapeDtypeStruct((b, t, 512), BF16),
        compiler_params=_params(("parallel", "parallel")),
        name="mem_prompt",
    )(memq, mk16, mv16)


def _mem_sample_kernel(q_ref, k_ref, v_ref, o_ref, *, bb):
    for bi in range(bb):
        for h in range(MEM_HEADS):
            sl = slice(h * LANES, (h + 1) * LANES)
            k = k_ref[bi, pl.ds(h, N_MEM, stride=MEM_HEADS), :].astype(BF16)
            v = v_ref[bi, pl.ds(h, N_MEM, stride=MEM_HEADS), :].astype(BF16)
            s = _dot_nt(q_ref[bi, :, sl], k)
            o_ref[bi, :, sl] = _softmax_pv(s, v).astype(BF16)


def _mem_sample(memq, ck, cv, b_off):
    b, t, _ = memq.shape
    bb = math.gcd(b, 4)
    off = b_off // bb
    kern = functools.partial(_mem_sample_kernel, bb=bb)
    rows = N_MEM * MEM_HEADS
    return pl.pallas_call(
        kern,
        grid=(b // bb,),
        in_specs=[pl.BlockSpec((bb, t, 512), lambda i: (i, 0, 0)),
                  pl.BlockSpec((bb, rows, LANES), lambda i: (i + off, 0, 0)),
                  pl.BlockSpec((bb, rows, LANES), lambda i: (i + off, 0, 0))],
        out_specs=pl.BlockSpec((bb, t, 512), lambda i: (i, 0, 0)),
        out_shape=jax.ShapeDtypeStruct((b, t, 512), BF16),
        compiler_params=_params(("parallel",)),
        name="mem_sample",
    )(memq, ck, cv)


def _page_copies(pt_ref, grp, j, slot, bb, ch, nc, streams):
    cps = []
    for bi in range(bb):
        base = ((grp * bb + bi) * nc + j) * ch
        for r in range(ch):
            pg = pt_ref[base + r]
            for src, buf, sem in streams:
                cps.append(pltpu.make_async_copy(src.at[pg], buf.at[slot, bi * ch + r], sem.at[slot]))
    return cps


def _stream_step(pt_ref, bb, ch, nc, streams):
    grp, j = pl.program_id(0), pl.program_id(1)
    s = grp * nc + j
    total = pl.num_programs(0) * nc
    slot = s & 1

    @pl.when(s == 0)
    def _():
        for cp in _page_copies(pt_ref, 0, 0, 0, bb, ch, nc, streams):
            cp.start()

    @pl.when(s + 1 < total)
    def _():
        last = j == nc - 1
        for cp in _page_copies(pt_ref, jnp.where(last, grp + 1, grp), jnp.where(last, 0, j + 1), 1 - slot,
                               bb, ch, nc, streams):
            cp.start()

    for cp in _page_copies(pt_ref, grp, j, slot, bb, ch, nc, streams):
        cp.wait()
    return slot


def _causal_new(s2, nq):
    qpos = lax.broadcasted_iota(jnp.int32, s2.shape, 0) & (nq - 1)
    kpos = lax.broadcasted_iota(jnp.int32, s2.shape, 1)
    return jnp.where(kpos <= qpos, s2, NEG_INF)


def _decode_kernel(pt_ref, lam_ref, qd_ref, kn_ref, vn_ref, gs_ref, mq_ref, kmn_ref, cn_ref, gk_ref, aq_ref,
                   wukt_ref, wuv_ref, kc_hbm, vc_hbm, cc_hbm, rc_hbm, od_ref, om_ref,
                   kbuf, vbuf, cbuf, rbuf, sem, md_ref, ld_ref, accd_ref, lw_ref, qr_ref, mm_ref, lm_ref, accm_ref,
                   *, bb, ch, nc):
    j = pl.program_id(1)
    streams = [(kc_hbm, kbuf, sem.at[0]), (vc_hbm, vbuf, sem.at[1]), (cc_hbm, cbuf, sem.at[2]), (rc_hbm, rbuf, sem.at[3])]
    slot = _stream_step(pt_ref, bb, ch, nc, streams)
    nq = mq_ref.shape[1]
    nr = MLA_HEADS * nq
    nk = MLA_HEADS * MLA_NOPE

    @pl.when(j == 0)
    def _():
        for ref in (md_ref, mm_ref):
            ref[...] = jnp.full(ref.shape, NEG_INF, F32)
        for ref in (ld_ref, lm_ref, accd_ref, accm_ref):
            ref[...] = jnp.zeros(ref.shape, F32)
        for bi in range(bb):
            lw_ref[bi, 0:nk, :] = wukt_ref[...]
            qg = (mq_ref[bi].astype(F32) * gk_ref[...]).astype(BF16)
            for h in range(MLA_HEADS):
                qa = _dot(qg[:, h * LANES:(h + 1) * LANES], aq_ref[h])
                lw_ref[bi, nk + h * nq:nk + (h + 1) * nq, :] = qa[:, 0:LANES].astype(BF16)
                qr_ref[bi, h * nq:(h + 1) * nq, :] = qa[:, LANES + 64:LANES + 96].astype(BF16)

    for bi in range(bb):
        pages = range(bi * ch, (bi + 1) * ch)
        kt = jnp.concatenate([kbuf[slot, r].astype(BF16) for r in pages], axis=-1)
        s = _dot(qd_ref[bi], kt)
        p, alpha = _online_update(s, md_ref.at[bi], ld_ref.at[bi])
        p = p.astype(BF16)
        pv = []
        for n in range(2):
            v = jnp.concatenate([vbuf[slot, r, pl.ds(n, PAGE, stride=2), :].astype(BF16) for r in pages], axis=0)
            pv.append(_dot(p[n * 32:(n + 1) * 32], v))
        accd_ref[bi] = alpha * accd_ref[bi] + jnp.concatenate(pv, axis=0)

        c = jnp.concatenate([cbuf[slot, r] for r in pages], axis=0).astype(BF16)
        krt = jnp.concatenate([rbuf[slot, r] for r in pages], axis=-1)
        big = _dot_nt(lw_ref[bi], c)
        nkeys = big.shape[1]
        knt = big[0:nk]
        ssq = jnp.sum((knt * knt).reshape(MLA_HEADS, MLA_NOPE, nkeys), axis=1)
        ssq = ssq + jnp.sum(krt * krt, axis=0, keepdims=True)
        rn = lax.rsqrt(ssq * (1.0 / MLA_QK) + NORM_EPS)
        sm = big[nk:nk + nr] + _dot(qr_ref[bi], krt.astype(BF16))
        sm = (sm.reshape(MLA_HEADS, nq, nkeys) * rn[:, None, :]).reshape(nr, nkeys)
        pm, alpham = _online_update(sm, mm_ref.at[bi], lm_ref.at[bi])
        accm_ref[bi] = alpham * accm_ref[bi] + _dot(pm.astype(BF16), c)

    @pl.when(j == nc - 1)
    def _():
        for bi in range(bb):
            q = qd_ref[bi]
            vn = vn_ref[bi]
            p2, alpha2 = _online_update(_causal_new(_dot_nt(q, kn_ref[bi]), nq), md_ref.at[bi], ld_ref.at[bi])
            p2 = p2.astype(BF16)
            pv2 = [_dot(p2[n * 32:(n + 1) * 32], vn[:, n * LANES:(n + 1) * LANES]) for n in range(2)]
            o = (alpha2 * accd_ref[bi] + jnp.concatenate(pv2, axis=0)) / ld_ref[bi]
            for n in range(2):
                on = o[n * 32:(n + 1) * 32]
                od = on[0:16] - lam_ref[0] * on[16:32]
                od = _rms_rows(od, SUBLN_EPS) * gs_ref[...]
                for g in range(2):
                    hh = n * 2 + g
                    od_ref[bi, :, hh * LANES:(hh + 1) * LANES] = od[g * nq:(g + 1) * nq].astype(BF16)

            qf = mq_ref[bi]
            kn = kmn_ref[bi]
            s2 = jnp.concatenate([_dot_nt(qf[:, h * LANES:(h + 1) * LANES], kn[:, h * LANES:(h + 1) * LANES])
                                  for h in range(MLA_HEADS)], axis=0)
            p3, alpha3 = _online_update(_causal_new(s2, nq), mm_ref.at[bi], lm_ref.at[bi])
            lat = ((alpha3 * accm_ref[bi] + _dot(p3.astype(BF16), cn_ref[bi])) / lm_ref[bi]).astype(BF16)
            om = _dot(lat[0:nq], wuv_ref[0])
            for h in range(1, MLA_HEADS):
                om = om + _dot(lat[h * nq:(h + 1) * nq], wuv_ref[h])
            om_ref[bi] = om.astype(BF16)


def _decode(pt_flat, lam, qbd, kn16, vn16, gs, mq, kmla, ckv16, gk_pad, aq, wukt, wuv2, kc, vc, cc, rc, n_pages):
    b, nq, _ = mq.shape
    assert nq == 8, "score-row layouts assume 8 new tokens per sequence"
    ch = math.gcd(PAGES_PER_STEP, n_pages)
    bb = math.gcd(SEQS_PER_STEP, b)
    nc = n_pages // ch
    kern = functools.partial(_decode_kernel, bb=bb, ch=ch, nc=nc)
    nr = MLA_HEADS * nq
    per_seq = lambda w, r=nq: pl.BlockSpec((bb, r, w), lambda g, j, pt: (g, 0, 0))
    const = lambda a: pl.BlockSpec(a.shape, lambda g, j, pt: (0,) * a.ndim)
    grid_spec = pltpu.PrefetchScalarGridSpec(
        num_scalar_prefetch=1,
        grid=(b // bb, nc),
        in_specs=[pl.BlockSpec(memory_space=pltpu.SMEM), per_seq(256, 64), per_seq(256), per_seq(256), const(gs),
                  per_seq(1024), per_seq(1024), per_seq(LANES), const(gk_pad), const(aq), const(wukt), const(wuv2)]
                 + [pl.BlockSpec(memory_space=pl.ANY)] * 4,
        out_specs=[per_seq(512), per_seq(512)],
        scratch_shapes=[pltpu.VMEM((2, bb * ch, 256, LANES), F32), pltpu.VMEM((2, bb * ch, 256, LANES), F32),
                        pltpu.VMEM((2, bb * ch, PAGE, LANES), F32), pltpu.VMEM((2, bb * ch, MLA_ROPE, PAGE), F32),
                        pltpu.SemaphoreType.DMA((4, 2)),
                        pltpu.VMEM((bb, 64, 1), F32), pltpu.VMEM((bb, 64, 1), F32), pltpu.VMEM((bb, 64, LANES), F32),
                        pltpu.VMEM((bb, MLA_HEADS * MLA_NOPE + nr, LANES), BF16), pltpu.VMEM((bb, nr, MLA_ROPE), BF16),
                        pltpu.VMEM((bb, nr, 1), F32), pltpu.VMEM((bb, nr, 1), F32), pltpu.VMEM((bb, nr, LANES), F32)],
    )
    return pl.pallas_call(
        kern,
        grid_spec=grid_spec,
        out_shape=[jax.ShapeDtypeStruct((b, nq, 512), BF16)] * 2,
        compiler_params=_params(("arbitrary", "arbitrary")),
        name="decode",
    )(pt_flat, lam, qbd, kn16, vn16, gs, mq, kmla, ckv16, gk_pad, aq, wukt, wuv2, kc, vc, cc, rc)


def _merge_kernel(x_ref, od_ref, om_ref, oc_ref, gmix_ref, wg_ref, bg_ref, wbr_ref, wout_ref, gffn_ref,
                  wrh_ref, wrl_ref, br_ref, xm_ref, h2_ref, tv_ref, ti_ref):
    x = x_ref[...]
    h = (_rms_rows(x) * gmix_ref[...]).astype(BF16)
    gates = jax.nn.sigmoid(_dot(h, wg_ref[...]) + bg_ref[...])
    merged = gates[:, 0:D_MODEL] * _dot(od_ref[...], wbr_ref[0])
    merged = merged + gates[:, D_MODEL:2 * D_MODEL] * _dot(om_ref[...], wbr_ref[1])
    merged = merged + gates[:, 2 * D_MODEL:3 * D_MODEL] * _dot(oc_ref[...], wbr_ref[2])
    xm = x + _dot(merged.astype(BF16), wout_ref[...])
    xm_ref[...] = xm
    h2 = _rms_rows(xm) * gffn_ref[...]
    for s in range(D_MODEL // LANES):
        h2_ref[:, s, :] = h2[:, s * LANES:(s + 1) * LANES]
    hh = h2.astype(BF16)
    hl = (h2 - hh.astype(F32)).astype(BF16)
    logits = _dot(hh, wrh_ref[...]) + _dot(hl, wrh_ref[...]) + _dot(hh, wrl_ref[...]) + br_ref[...]
    lane = lax.broadcasted_iota(jnp.int32, logits.shape, 1)
    logits = jnp.where(lane < N_EXPERTS, logits, -jnp.inf)
    tv = jnp.zeros(logits.shape, F32)
    ti = jnp.zeros(logits.shape, jnp.int32)
    vals = []
    for k in range(TOP_K):
        mx = jnp.max(logits, axis=-1, keepdims=True)
        idx = jnp.min(jnp.where(logits == mx, lane, LANES), axis=-1, keepdims=True)
        vals.append(mx)
        ti = jnp.where(lane == k, idx, ti)
        logits = jnp.where(lane == idx, -jnp.inf, logits)
    es = [jnp.exp(v - vals[0]) for v in vals]
    den = es[0] + es[1] + es[2] + es[3]
    for k in range(TOP_K):
        tv = jnp.where(lane == k, es[k] / den, tv)
    tv_ref[...] = tv
    ti_ref[...] = ti


def _merge(x2d, od, om, oc, wts):
    n = x2d.shape[0]
    tm = min(TOKEN_TILE, n)
    row = lambda w: pl.BlockSpec((tm, w), lambda i: (i, 0))
    return pl.pallas_call(
        _merge_kernel,
        grid=(n // tm,),
        in_specs=[row(D_MODEL), row(512), row(512), row(512)] + [_full(a) for a in wts],
        out_specs=[row(D_MODEL), pl.BlockSpec((tm, 8, LANES), lambda i: (i, 0, 0)), row(LANES), row(LANES)],
        out_shape=[jax.ShapeDtypeStruct((n, D_MODEL), F32), jax.ShapeDtypeStruct((n, 8, LANES), F32),
                   jax.ShapeDtypeStruct((n, LANES), F32), jax.ShapeDtypeStruct((n, LANES), jnp.int32)],
        compiler_params=_params(("parallel",)),
        name="merge",
    )(x2d, od, om, oc, *wts)


def _expert_kernel(be_ref, first_ref, nval_ref, slot_ref, wgu_ref, bgu_ref, wd_ref, bd_ref, x_hbm, y_hbm,
                   slot_smem, xbuf, ybuf, zbuf, wgu16, wd16, gsem, ssem, isem, *, n_blocks, n_tok, n_real):
    i = pl.program_id(0)
    nv = nval_ref[i]
    groups = lambda k: lax.shift_right_logical(nval_ref[k] + (ROW_GROUP - 1), ROW_GROUP.bit_length() - 1)

    def idx_copy(k):
        ring = lax.rem(k, 3)
        return pltpu.make_async_copy(slot_ref.at[k], slot_smem.at[ring], isem.at[ring])

    def issue_gather(k):
        ring, xb = lax.rem(k, 3), k & 1

        def body(g, carry):
            for u in range(ROW_GROUP):
                r = g * ROW_GROUP + u
                tok = jnp.minimum(lax.shift_right_logical(slot_smem[ring, 0, r], 2), n_tok - 1)
                pltpu.make_async_copy(x_hbm.at[tok], xbuf.at[xb, r], gsem.at[xb]).start()
            return carry

        lax.fori_loop(0, groups(k), body, 0)

    def wait_gather(k):
        xb = k & 1

        def body(g, carry):
            r0 = pl.multiple_of(g * ROW_GROUP, ROW_GROUP)
            pltpu.make_async_copy(x_hbm.at[pl.ds(0, ROW_GROUP)], xbuf.at[xb, pl.ds(r0, ROW_GROUP)], gsem.at[xb]).wait()
            return carry

        lax.fori_loop(0, groups(k), body, 0)

    def issue_scatter(k):
        ring = lax.rem(k, 3)

        def body(g, carry):
            for u in range(ROW_GROUP):
                r = g * ROW_GROUP + u
                pltpu.make_async_copy(ybuf.at[r], y_hbm.at[slot_smem[ring, 0, r]], ssem.at[0]).start()
            return carry

        lax.fori_loop(0, groups(k), body, 0)

    def wait_scatter(k):
        def body(g, carry):
            r0 = pl.multiple_of(g * ROW_GROUP, ROW_GROUP)
            pltpu.make_async_copy(ybuf.at[pl.ds(r0, ROW_GROUP)], y_hbm.at[pl.ds(0, ROW_GROUP)], ssem.at[0]).wait()
            return carry

        lax.fori_loop(0, groups(k), body, 0)

    @pl.when(i == 0)
    def _():
        xbuf[...] = jnp.zeros(xbuf.shape, F32)
        zbuf[...] = jnp.zeros(zbuf.shape, F32)
        spare = pltpu.make_async_copy(zbuf, y_hbm.at[pl.ds(n_real, ROW_GROUP)], ssem.at[0])
        spare.start()
        spare.wait()
        first_idx = idx_copy(0)
        first_idx.start()
        first_idx.wait()
        issue_gather(0)
        if n_blocks > 1:
            idx_copy(1).start()

    @pl.when(i + 1 < n_blocks)
    def _():
        idx_copy(i + 1).wait()
        issue_gather(i + 1)

    @pl.when(i + 2 < n_blocks)
    def _():
        idx_copy(i + 2).start()

    @pl.when(nv > 0)
    def _():
        @pl.when(first_ref[i] == 1)
        def _():
            wgu16[...] = wgu_ref[...].astype(BF16)
            wd16[...] = wd_ref[...].astype(BF16)

        wait_gather(i)
        xb = i & 1
        x = jnp.concatenate([xbuf[xb, :, s, :] for s in range(D_MODEL // LANES)], axis=-1).astype(BF16)
        gu = _dot(x, wgu16[...]) + bgu_ref[...]
        gate = jnp.minimum(gu[:, 0:D_EXPERT], SWIGLU_LIMIT)
        up = jnp.clip(gu[:, D_EXPERT:2 * D_EXPERT], -SWIGLU_LIMIT, SWIGLU_LIMIT)
        act = (up + 1.0) * gate * jax.nn.sigmoid(SWIGLU_ALPHA * gate)
        y = _dot(act.astype(BF16), wd16[...]) + bd_ref[...]

        @pl.when(i > 0)
        def _():
            wait_scatter(i - 1)

        for s in range(D_MODEL // LANES):
            ybuf[:, s, :] = y[:, s * LANES:(s + 1) * LANES]
        issue_scatter(i)

    @pl.when((nv == 0) & (i > 0))
    def _():
        wait_scatter(i - 1)

    @pl.when(i == n_blocks - 1)
    def _():
        wait_scatter(i)


def _experts(block_e, first, nval, row_slot3, wgu, bgu, wd, bd, x3, n_real):
    n_blocks = row_slot3.shape[0]
    bm = row_slot3.shape[2]
    n_slots = n_real + ROW_GROUP
    kern = functools.partial(_expert_kernel, n_blocks=n_blocks, n_tok=x3.shape[0], n_real=n_real)
    grid_spec = pltpu.PrefetchScalarGridSpec(
        num_scalar_prefetch=3,
        grid=(n_blocks,),
        in_specs=[pl.BlockSpec(row_slot3.shape, lambda i, be, fi, na: (0, 0, 0)),
                  pl.BlockSpec((None, D_MODEL, 2 * D_EXPERT), lambda i, be, fi, na: (be[i], 0, 0)),
                  pl.BlockSpec((None, 1, 2 * D_EXPERT), lambda i, be, fi, na: (be[i], 0, 0)),
                  pl.BlockSpec((None, D_EXPERT, D_MODEL), lambda i, be, fi, na: (be[i], 0, 0)),
                  pl.BlockSpec((None, 1, D_MODEL), lambda i, be, fi, na: (be[i], 0, 0)),
                  pl.BlockSpec(memory_space=pl.ANY)],
        out_specs=pl.BlockSpec(memory_space=pl.ANY),
        scratch_shapes=[pltpu.SMEM((3, 1, bm), jnp.int32),
                        pltpu.VMEM((2, bm, 8, LANES), F32), pltpu.VMEM((bm, 8, LANES), F32),
                        pltpu.VMEM((ROW_GROUP, 8, LANES), F32),
                        pltpu.VMEM((D_MODEL, 2 * D_EXPERT), BF16), pltpu.VMEM((D_EXPERT, D_MODEL), BF16),
                        pltpu.SemaphoreType.DMA((2,)), pltpu.SemaphoreType.DMA((1,)), pltpu.SemaphoreType.DMA((3,))],
    )
    return pl.pallas_call(
        kern,
        grid_spec=grid_spec,
        out_shape=jax.ShapeDtypeStruct((n_slots, 8, LANES), F32),
        compiler_params=_params(("arbitrary",)),
        name="experts",
    )(block_e, first, nval, row_slot3, wgu, bgu, wd, bd, x3)


def _combine_kernel(xm_ref, tv_ref, y_ref, o_ref):
    tv = tv_ref[...]
    for s in range(D_MODEL // LANES):
        acc = xm_ref[:, s * LANES:(s + 1) * LANES]
        for k in range(TOP_K):
            acc = acc + tv[:, k:k + 1] * y_ref[:, k * 8 + s, :]
        o_ref[:, s * LANES:(s + 1) * LANES] = acc


def _combine(xm, tv, yslots, tok_off):
    n = xm.shape[0]
    tm = min(TOKEN_TILE, n)
    off = tok_off // tm
    y4 = yslots.reshape(yslots.shape[0] // TOP_K, TOP_K * 8, LANES)
    return pl.pallas_call(
        _combine_kernel,
        grid=(n // tm,),
        in_specs=[pl.BlockSpec((tm, D_MODEL), lambda i: (i, 0)),
                  pl.BlockSpec((tm, LANES), lambda i: (i, 0)),
                  pl.BlockSpec((tm, TOP_K * 8, LANES), lambda i: (i + off, 0, 0))],
        out_specs=pl.BlockSpec((tm, D_MODEL), lambda i: (i, 0)),
        out_shape=jax.ShapeDtypeStruct((n, D_MODEL), F32),
        compiler_params=_params(("parallel",)),
        name="combine",
    )(xm, tv, y4)


def _rope_tables(pos, rows):
    pos = pos.astype(F32)[:, None]
    lane = jnp.arange(LANES)

    def ang(half):
        inv = jnp.power(ROPE_THETA, -jnp.arange(half, dtype=F32) / half)
        return pos * inv[None, :]

    a64 = ang(32)[:, lane % 32]
    first = (lane % 64) < 32
    c64 = jnp.cos(a64)
    sa64 = jnp.where(first, -jnp.sin(a64), 0.0)
    sb64 = jnp.where(first, 0.0, jnp.sin(a64))
    a32 = ang(16)[:, lane % 16]
    in_a = (lane >= 64) & (lane < 80)
    in_b = (lane >= 80) & (lane < 96)
    c32 = jnp.where(in_a | in_b, jnp.cos(a32), 1.0)
    sa32 = jnp.where(in_a, -jnp.sin(a32), 0.0)
    sb32 = jnp.where(in_b, jnp.sin(a32), 0.0)
    tabs = [c64, sa64, sb64, c32, sa32, sb32]
    reps = rows // pos.shape[0]
    return [jnp.tile(t, (reps, 1)) if reps > 1 else t for t in tabs]


def _tied_pad(g, scale):
    blk = jnp.concatenate([g[:MLA_NOPE], g[MLA_NOPE:], g[MLA_NOPE:], jnp.zeros((32,), F32)]) * scale
    return jnp.tile(blk, MLA_HEADS)[None, :]


def _layer_weights(lp):
    w_in = lp["w_in"]
    kr_blk = jnp.zeros((D_MODEL, LANES), F32).at[:, 64:96].set(w_in[:, _OFF_KR:_OFF_MQ])
    wa = jnp.concatenate([w_in[:, _OFF_Q:_OFF_KR], kr_blk, w_in[:, _OFF_MQ:_OFF_G]], axis=1).astype(BF16)
    wuq = jnp.pad(lp["w_mla_uq"], ((0, 0), (0, 0), (0, LANES - MLA_QK))).reshape(MLA_Q_LORA, MLA_HEADS * LANES).astype(BF16)
    wuk_pad = jnp.pad(lp["w_mla_uk"], ((0, 0), (0, 0), (0, LANES - MLA_NOPE))).reshape(MLA_KV_LORA, MLA_HEADS * LANES)
    lane = jnp.arange(LANES)
    rope_eye = jnp.where(((lane >= 64) & (lane < 96))[:, None], jnp.eye(LANES, dtype=F32), 0.0)
    wk = jnp.concatenate([wuk_pad, jnp.tile(rope_eye, (1, MLA_HEADS))], axis=0).astype(BF16)
    g64 = jnp.kron(jnp.eye(4, dtype=F32), jnp.ones((64, 64), F32)).astype(BF16)
    g128 = jnp.kron(jnp.eye(2, dtype=F32), jnp.ones((128, 128), F32)).astype(BF16)
    gk_pad = _tied_pad(lp["mla_k_norm"], 1.0)
    inproj_w = [lp["norm_mix"][None, :], wa, wuq, wk, g64, g128,
                jnp.tile(lp["diff_q_norm"], 8)[None, :] * DIFF_SCALE, jnp.tile(lp["diff_k_norm"], 4)[None, :],
                lp["mla_q_a_norm"][None, :], lp["mla_kv_a_norm"][None, :],
                _tied_pad(lp["mla_q_norm"], MLA_SCALE), gk_pad,
                jnp.tile(lp["mem_q_norm"], MEM_HEADS)[None, :] * MEM_SCALE]
    wuv = lp["w_mla_uv"]
    wuv2 = jnp.einsum("rhd,hg->hrgd", wuv, jnp.eye(MLA_HEADS, dtype=F32)).reshape(MLA_HEADS, MLA_KV_LORA, 512).astype(BF16)
    wuk_t = jnp.transpose(lp["w_mla_uk"], (1, 2, 0))
    aq = jnp.zeros((MLA_HEADS, LANES, 2 * LANES), F32).at[:, 0:MLA_NOPE, 0:LANES].set(wuk_t)
    aq = aq.at[:, :, LANES:].add(rope_eye[None]).astype(BF16)
    wukt = wuk_t.reshape(MLA_HEADS * MLA_NOPE, MLA_KV_LORA).astype(BF16)
    wr = jnp.pad(lp["w_router"], ((0, 0), (0, LANES - N_EXPERTS)))
    wrh = wr.astype(BF16)
    wrl = (wr - wrh.astype(F32)).astype(BF16)
    merge_w = [lp["norm_mix"][None, :], w_in[:, _OFF_G:].astype(BF16), lp["b_gate"][None, :],
               lp["w_branch"].reshape(3, 512, D_MODEL).astype(BF16), lp["w_out"].astype(BF16),
               lp["norm_ffn"][None, :], wrh, wrl, jnp.pad(lp["b_router"], (0, LANES - N_EXPERTS))[None, :]]
    return dict(inproj=inproj_w, g128=g128, wuv2=wuv2, aq=aq, wukt=wukt, gk_pad=gk_pad, merge=merge_w)


def _moe_plan(ti, bm):
    n = ti.shape[0]
    a = n * TOP_K
    flat_e = ti[:, :TOP_K].reshape(a)
    onehot = (flat_e[:, None] == jnp.arange(N_EXPERTS, dtype=jnp.int32)[None, :]).astype(jnp.int32)
    csum = jnp.cumsum(onehot, axis=0)
    counts = csum[-1]
    rank = jnp.sum(onehot * csum, axis=1) - 1
    padded = (counts + bm - 1) // bm * bm
    pend = jnp.cumsum(padded)
    pstart = pend - padded
    dest = jnp.sum(onehot * pstart[None, :], axis=1) + rank
    n_rows = (a + N_EXPERTS * (bm - 1) + bm - 1) // bm * bm
    n_blocks = n_rows // bm
    spare = a + (jnp.arange(n_rows, dtype=jnp.int32) % ROW_GROUP)
    row_slot = spare.at[dest].set(jnp.arange(a, dtype=jnp.int32))
    blk_start = jnp.arange(n_blocks, dtype=jnp.int32) * bm
    block_e = jnp.minimum(jnp.sum((blk_start[:, None] >= pend[None, :]).astype(jnp.int32), axis=1), N_EXPERTS - 1)
    first = jnp.concatenate([jnp.ones((1,), jnp.int32), (block_e[1:] != block_e[:-1]).astype(jnp.int32)])
    eh = (block_e[:, None] == jnp.arange(N_EXPERTS, dtype=jnp.int32)[None, :]).astype(jnp.int32)
    valid_end = jnp.sum(eh * (pstart + counts)[None, :], axis=1)
    nval = jnp.where(blk_start < pend[-1], jnp.clip(valid_end - blk_start, 0, bm), 0).astype(jnp.int32)
    return block_e, first, nval, row_slot.reshape(n_blocks, 1, bm), a


def _qbd(dq_s, b, nq):
    q = dq_s.reshape(b, nq, 2, 2, 2, DIFF_HEAD_DIM)
    q = jnp.transpose(q, (0, 2, 4, 3, 1, 5))
    eye = jnp.eye(4, dtype=q.dtype).reshape(2, 2, 2, 2)
    out = jnp.einsum("bncgqd,ncmk->bncgqmkd", q, eye)
    return out.reshape(b, 64, 256)


def kernel(x_prompt, x_sample, mem_prompt, cache_diff_k, cache_diff_v, cache_mla_ckv, cache_mla_krope, cache_mem_k, cache_mem_v, page_table, norm_mix, norm_mem, w_in, b_gate, diff_q_norm, diff_k_norm, diff_lambda, diff_subln, mla_q_a_norm, w_mla_uq, mla_kv_a_norm, w_mla_uk, w_mla_uv, mla_q_norm, mla_k_norm, w_mem_kv, mem_q_norm, mem_k_norm, w_branch, w_out, norm_ffn, w_router, b_router, w_gate_up, b_gate_up, w_down, b_down):
    depth = w_in.shape[0]
    bp, t, _ = x_prompt.shape
    bs, nq, _ = x_sample.shape
    n_pool, n_pages = cache_diff_k.shape[1], page_table.shape[1]
    past_len = n_pages * PAGE
    n_p, n_s = bp * t, bs * nq

    tm_p, tm_s = min(TOKEN_TILE, n_p), min(TOKEN_TILE, n_s)
    tabs_p = _rope_tables(jnp.arange(t, dtype=jnp.int32), t)
    tabs_s = _rope_tables(past_len + jnp.arange(nq, dtype=jnp.int32), tm_s)
    del tm_p

    kc = jnp.transpose(cache_diff_k, (0, 1, 3, 4, 5, 2)).reshape(depth * n_pool, 256, PAGE)
    vc = cache_diff_v.reshape(depth * n_pool, 2 * PAGE, DIFF_V_DIM)
    cc = cache_mla_ckv.reshape(depth * n_pool, PAGE, MLA_KV_LORA)
    rc = jnp.transpose(cache_mla_krope, (0, 1, 3, 2)).reshape(depth * n_pool, MLA_ROPE, PAGE)
    mkc = cache_mem_k.reshape(depth * bs, N_MEM * MEM_HEADS, MEM_HEAD_DIM)
    mvc = cache_mem_v.reshape(depth * bs, N_MEM * MEM_HEADS, MEM_HEAD_DIM)

    xp = x_prompt.reshape(n_p, D_MODEL)
    xs = x_sample.reshape(n_s, D_MODEL)
    outs = [[] for _ in range(10)]
    for layer in range(depth):
        lp = dict(norm_mix=norm_mix[layer], w_in=w_in[layer], b_gate=b_gate[layer], diff_q_norm=diff_q_norm[layer],
                  diff_k_norm=diff_k_norm[layer], mla_q_a_norm=mla_q_a_norm[layer], w_mla_uq=w_mla_uq[layer],
                  mla_kv_a_norm=mla_kv_a_norm[layer], w_mla_uk=w_mla_uk[layer], w_mla_uv=w_mla_uv[layer],
                  mla_q_norm=mla_q_norm[layer], mla_k_norm=mla_k_norm[layer], mem_q_norm=mem_q_norm[layer],
                  w_branch=w_branch[layer], w_out=w_out[layer], norm_ffn=norm_ffn[layer],
                  w_router=w_router[layer], b_router=b_router[layer])
        w = _layer_weights(lp)
        lam_init = 0.8 - 0.6 * math.exp(-0.3 * layer)
        lamp = diff_lambda[layer].astype(F32)
        lam = (jnp.exp(jnp.sum(lamp[0] * lamp[1])) - jnp.exp(jnp.sum(lamp[2] * lamp[3])) + lam_init).reshape(1)
        gs = (diff_subln[layer] * (1.0 - lam_init))[None, :]
        pt_flat = (page_table + layer * n_pool).reshape(-1).astype(jnp.int32)

        (dq, dk32, dk16, dv32, dv16, mq, ckv32, ckv16, kr32, kmla, memq) = _inproj(xp, tabs_p, w["inproj"])
        o_diff = _diff_prompt(lam, dq.reshape(bp, t, 512), dk16.reshape(bp, t, 256), dv16.reshape(bp, t, 256), gs)
        o_mla = _mla_prompt(mq.reshape(bp, t, 1024), kmla.reshape(bp, t, 1024), ckv16.reshape(bp, t, LANES), w["wuv2"])
        mk32, mk16, mv32, mv16 = _memkv(mem_prompt.reshape(bp * N_MEM, D_MODEL), norm_mem[layer][None, :],
                                        w_mem_kv[layer].astype(BF16), w["g128"],
                                        jnp.tile(mem_k_norm[layer], MEM_HEADS)[None, :])
        o_mem = _mem_prompt(memq.reshape(bp, t, 512), mk16.reshape(bp, N_MEM, 512), mv16.reshape(bp, N_MEM, 512))
        xm_p, h2_p, tv_p, ti_p = _merge(xp, o_diff.reshape(n_p, 512), o_mla.reshape(n_p, 512),
                                        o_mem.reshape(n_p, 512), w["merge"])
        for lst, val in zip(outs[:6], (dk32.reshape(bp, t, 2, 2, 64), dv32.reshape(bp, t, 2, 128),
                                       ckv32.reshape(bp, t, 128), kr32.reshape(bp, t, 32),
                                       mk32.reshape(bp, N_MEM, 4, 128), mv32.reshape(bp, N_MEM, 4, 128))):
            lst.append(val)

        (dq, dk32, dk16, dv32, dv16, mq, ckv32, ckv16, kr32, kmla, memq) = _inproj(xs, tabs_s, w["inproj"])
        o_diff, o_mla = _decode(pt_flat, lam, _qbd(dq, bs, nq), dk16.reshape(bs, nq, 256), dv16.reshape(bs, nq, 256), gs,
                                mq.reshape(bs, nq, 1024), kmla.reshape(bs, nq, 1024), ckv16.reshape(bs, nq, LANES),
                                w["gk_pad"], w["aq"], w["wukt"], w["wuv2"], kc, vc, cc, rc, n_pages)
        o_mem = _mem_sample(memq.reshape(bs, nq, 512), mkc, mvc, layer * bs)
        xm_s, h2_s, tv_s, ti_s = _merge(xs, o_diff.reshape(n_s, 512), o_mla.reshape(n_s, 512),
                                        o_mem.reshape(n_s, 512), w["merge"])
        for lst, val in zip(outs[6:], (dk32.reshape(bs, nq, 2, 2, 64), dv32.reshape(bs, nq, 2, 128),
                                       ckv32.reshape(bs, nq, 128), kr32.reshape(bs, nq, 32))):
            lst.append(val)

        h2 = jnp.concatenate([h2_p, h2_s], axis=0)
        ti = jnp.concatenate([ti_p, ti_s], axis=0)
        block_e, first, nval, row_slot3, n_slots = _moe_plan(ti, MOE_ROWS)
        yslots = _experts(block_e + layer * N_EXPERTS, first, nval, row_slot3,
                          w_gate_up.reshape(depth * N_EXPERTS, D_MODEL, 2 * D_EXPERT),
                          b_gate_up.reshape(depth * N_EXPERTS, 1, 2 * D_EXPERT),
                          w_down.reshape(depth * N_EXPERTS, D_EXPERT, D_MODEL),
                          b_down.reshape(depth * N_EXPERTS, 1, D_MODEL), h2, n_slots)
        xp = _combine(xm_p, tv_p, yslots, 0)
        xs = _combine(xm_s, tv_s, yslots, n_p)

    stack = lambda lst: jnp.stack(lst)
    return (xp.reshape(bp, t, D_MODEL), xs.reshape(bs, nq, D_MODEL)) + tuple(stack(o) for o in outs)
```

```python
import functools
import math

import jax
import jax.numpy as jnp
from jax import lax
from jax.experimental import pallas as pl
from jax.experimental.pallas import tpu as pltpu

F32 = jnp.float32
BF16 = jnp.bfloat16

D_MODEL = 1024
DIFF_HEAD_DIM = 64
DIFF_V_DIM = 128
MLA_HEADS = 8
MLA_Q_LORA = 256
MLA_KV_LORA = 128
MLA_NOPE = 64
MLA_ROPE = 32
MLA_V = 64
MLA_QK = MLA_NOPE + MLA_ROPE
MEM_HEADS = 4
MEM_HEAD_DIM = 128
N_MEM = 256
N_EXPERTS = 32
TOP_K = 4
D_EXPERT = 1024
SWIGLU_ALPHA = 1.702
SWIGLU_LIMIT = 7.0
ROPE_THETA = 10000.0
NORM_EPS = 1e-6
SUBLN_EPS = 1e-5
NEG_INF = -1e30
PAGE = 128

DIFF_SCALE = DIFF_HEAD_DIM ** -0.5
MLA_SCALE = MLA_QK ** -0.5
MEM_SCALE = MEM_HEAD_DIM ** -0.5

LANES = 128
MXU_DIM = 256
VMEM_LIMIT = 52 * 1024 * 1024
TOKEN_TILE = 256
ATTN_TILE = 256
PROMPT_Q_TILE = 256
PAGES_PER_STEP = 8
SEQS_PER_STEP = 4
MOE_ROWS = 256
ROW_GROUP = 8

_OFF_Q, _OFF_K, _OFF_V, _OFF_CQ, _OFF_CKV, _OFF_KR, _OFF_MQ, _OFF_G = 0, 512, 768, 1024, 1280, 1408, 1440, 1952


def _params(sem):
    return pltpu.CompilerParams(dimension_semantics=sem, vmem_limit_bytes=VMEM_LIMIT)


def _full(a):
    nd = a.ndim
    return pl.BlockSpec(a.shape, lambda *_: (0,) * nd)


def _dot(a, b):
    return jnp.dot(a, b, preferred_element_type=F32)


def _dot_nt(a, b):
    return lax.dot_general(a, b, (((1,), (1,)), ((), ())), preferred_element_type=F32)


def _rms_rows(x, eps=NORM_EPS):
    return x * lax.rsqrt(jnp.mean(x * x, axis=-1, keepdims=True) + eps)


def _group_sumsq(v, g_ref):
    sq = (v * v).astype(BF16)
    parts = [_dot(sq[:, j * MXU_DIM:(j + 1) * MXU_DIM], g_ref[...]) for j in range(v.shape[1] // MXU_DIM)]
    return parts[0] if len(parts) == 1 else jnp.concatenate(parts, axis=-1)


def _rope_lanes(v, c, sa, sb, half):
    outs = []
    for j in range(v.shape[1] // LANES):
        b = v[:, j * LANES:(j + 1) * LANES]
        outs.append(b * c + pltpu.roll(b, LANES - half, 1) * sa + pltpu.roll(b, half, 1) * sb)
    return outs[0] if len(outs) == 1 else jnp.concatenate(outs, axis=-1)


def _inproj_kernel(x_ref, gmix_ref, wa_ref, wuq_ref, wk_ref, g64_ref, g128_ref,
                   gq_ref, gk_ref, gcq_ref, gckv_ref, gmq_ref, gkm_ref, gmemq_ref,
                   c64_ref, sa64_ref, sb64_ref, c32_ref, sa32_ref, sb32_ref,
                   dq_ref, dk32_ref, dk16_ref, dv32_ref, dv16_ref, mq_ref,
                   ckv32_ref, ckv16_ref, kr32_ref, kmla_ref, memq_ref, dvt_ref, ckvt_ref):
    x = x_ref[...]
    h = (_rms_rows(x) * gmix_ref[...]).astype(BF16)
    z = _dot(h, wa_ref[...])
    c64, sa64, sb64 = c64_ref[...], sa64_ref[...], sb64_ref[...]
    c32, sa32, sb32 = c32_ref[...], sa32_ref[...], sb32_ref[...]

    zq = z[:, 0:512]
    qn = zq * lax.rsqrt(_group_sumsq(zq, g64_ref) * (1.0 / DIFF_HEAD_DIM) + NORM_EPS) * gq_ref[...]
    dq_ref[...] = _rope_lanes(qn, c64, sa64, sb64, 32).astype(BF16)

    zk = z[:, 512:768]
    kn = zk * lax.rsqrt(_group_sumsq(zk, g64_ref) * (1.0 / DIFF_HEAD_DIM) + NORM_EPS) * gk_ref[...]
    dk = _rope_lanes(kn, c64, sa64, sb64, 32)
    dk32_ref[...] = dk
    dk16_ref[...] = dk.astype(BF16)

    dv = z[:, 768:1024]
    dv32_ref[...] = dv
    dv16_ref[...] = dv.astype(BF16)
    dvt_ref[...] = dv.T.astype(BF16)

    cq = (_rms_rows(z[:, 1024:1280]) * gcq_ref[...]).astype(BF16)
    mqr = _dot(cq, wuq_ref[...])
    mqn = mqr * lax.rsqrt(_group_sumsq(mqr, g128_ref) * (1.0 / MLA_QK) + NORM_EPS) * gmq_ref[...]
    mq_ref[...] = _rope_lanes(mqn, c32, sa32, sb32, 16).astype(BF16)

    ckv = _rms_rows(z[:, 1280:1408]) * gckv_ref[...]
    ckv32_ref[...] = ckv
    ckv16 = ckv.astype(BF16)
    ckv16_ref[...] = ckv16
    ckvt_ref[...] = ckv.T.astype(BF16)

    krb = _rope_lanes(z[:, 1408:1536], c32, sa32, sb32, 16)
    kr32_ref[...] = krb[:, 64:96]
    kin = jnp.concatenate([ckv16, krb.astype(BF16)], axis=-1)
    kraw = _dot(kin, wk_ref[...])
    kmla_ref[...] = (kraw * lax.rsqrt(_group_sumsq(kraw, g128_ref) * (1.0 / MLA_QK) + NORM_EPS)
                     * gkm_ref[...]).astype(BF16)

    zm = z[:, 1536:2048]
    memq_ref[...] = (zm * lax.rsqrt(_group_sumsq(zm, g128_ref) * (1.0 / MEM_HEAD_DIM) + NORM_EPS)
                     * gmemq_ref[...]).astype(BF16)


def _inproj(x2d, tabs, wts):
    n = x2d.shape[0]
    tm = min(TOKEN_TILE, n)
    period = tabs[0].shape[0] // tm
    row = lambda w: pl.BlockSpec((tm, w), lambda i: (i, 0))
    tab = pl.BlockSpec((tm, LANES), lambda i: (i % period, 0))
    out_w = [(512, BF16), (256, F32), (256, BF16), (256, F32), (256, BF16), (1024, BF16),
             (128, F32), (128, BF16), (32, F32), (1024, BF16), (512, BF16)]
    tile_t = lambda d: pl.BlockSpec((None, d, tm), lambda i: (i, 0, 0))
    return pl.pallas_call(
        _inproj_kernel,
        grid=(n // tm,),
        in_specs=[row(D_MODEL)] + [_full(a) for a in wts] + [tab] * 6,
        out_specs=[row(w) for w, _ in out_w] + [tile_t(256), tile_t(LANES)],
        out_shape=[jax.ShapeDtypeStruct((n, w), dt) for w, dt in out_w]
                  + [jax.ShapeDtypeStruct((n // tm, 256, tm), BF16), jax.ShapeDtypeStruct((n // tm, LANES, tm), BF16)],
        compiler_params=_params(("parallel",)),
        name="inproj",
    )(x2d, *wts, *tabs)


def _memkv_kernel(x_ref, g_ref, w_ref, g128_ref, gk_ref, k32_ref, k16_ref, v32_ref, v16_ref):
    h = (_rms_rows(x_ref[...]) * g_ref[...]).astype(BF16)
    kv = _dot(h, w_ref[...])
    k = kv[:, 0:512]
    k = k * lax.rsqrt(_group_sumsq(k, g128_ref) * (1.0 / MEM_HEAD_DIM) + NORM_EPS) * gk_ref[...]
    v = kv[:, 512:1024]
    k32_ref[...] = k
    k16_ref[...] = k.astype(BF16)
    v32_ref[...] = v
    v16_ref[...] = v.astype(BF16)


def _memkv(mem2d, g, w, g128, gk):
    n = mem2d.shape[0]
    tm = min(TOKEN_TILE, n)
    row = lambda w_: pl.BlockSpec((tm, w_), lambda i: (i, 0))
    return pl.pallas_call(
        _memkv_kernel,
        grid=(n // tm,),
        in_specs=[row(D_MODEL), _full(g), _full(w), _full(g128), _full(gk)],
        out_specs=[row(512)] * 4,
        out_shape=[jax.ShapeDtypeStruct((n, 512), dt) for dt in (F32, BF16, F32, BF16)],
        compiler_params=_params(("parallel",)),
        name="memkv",
    )(mem2d, g, w, g128, gk)


def _online_update(s, m_ref, l_ref, rows=None):
    sl = slice(None) if rows is None else rows
    m_old = m_ref[sl, :]
    m_new = jnp.maximum(m_old, jnp.max(s, axis=-1, keepdims=True))
    alpha = jnp.exp(m_old - m_new)
    p = jnp.exp(s - m_new)
    l_ref[sl, :] = alpha * l_ref[sl, :] + jnp.sum(p, axis=-1, keepdims=True)
    m_ref[sl, :] = m_new
    return p, alpha


def _cols_softmax_step(s, m_ref, l_ref, cols):
    m_old = m_ref[:, cols]
    m_new = jnp.maximum(m_old, jnp.max(s, axis=0, keepdims=True))
    alpha = jnp.exp(m_old - m_new)
    p = jnp.exp(s - m_new)
    l_ref[:, cols] = alpha * l_ref[:, cols] + jnp.sum(p, axis=0, keepdims=True)
    m_ref[:, cols] = m_new
    return p, alpha


def _causal_cols(s, tq, delta):
    row = lax.broadcasted_iota(jnp.int32, s.shape, 0)
    col = lax.broadcasted_iota(jnp.int32, s.shape, 1) & (tq - 1)
    return jnp.where(row <= col + delta, s, NEG_INF)


def _cols_attention(i, tq, tk, n_cols, score_fn, vt_ref, m_ref, l_ref, acc_ref):
    m_ref[...] = jnp.full(m_ref.shape, NEG_INF, F32)
    l_ref[...] = jnp.zeros(l_ref.shape, F32)
    acc_ref[...] = jnp.zeros(acc_ref.shape, F32)
    j_last = (i * tq) // tk
    delta = i * tq - j_last * tk

    n_chunks = n_cols // MXU_DIM

    def scores(j):
        return tuple(score_fn(j, cc) for cc in range(n_chunks))

    def finish(j, s_all, masked):
        vt = vt_ref[j]
        for cc in range(n_chunks):
            cols = slice(cc * MXU_DIM, (cc + 1) * MXU_DIM)
            s = _causal_cols(s_all[cc], tq, delta) if masked else s_all[cc]
            p, alpha = _cols_softmax_step(s, m_ref, l_ref, cols)
            acc_ref[:, cols] = alpha * acc_ref[:, cols] + _dot(vt, p.astype(BF16))

    def body(j, s_cur):
        s_next = scores(j + 1)
        finish(j, s_cur, False)
        return s_next

    s_last = lax.fori_loop(0, j_last, body, scores(0))
    finish(j_last, s_last, True)


def _diff_prompt_kernel(lam_ref, q_ref, k_ref, vt_ref, gs_ref, o_ref, qt_ref, m_ref, l_ref, acc_ref, *, tq, tk):
    i = pl.program_id(2)
    dim = lax.broadcasted_iota(jnp.int32, (LANES, tq), 0)
    for g in range(2):
        qgt = q_ref[:, g * LANES:(g + 1) * LANES].astype(F32).T
        qt_ref[:, g * tq:(g + 1) * tq] = jnp.where(dim < DIFF_HEAD_DIM, qgt, 0.0).astype(BF16)
        qt_ref[:, (2 + g) * tq:(3 + g) * tq] = jnp.where(dim >= DIFF_HEAD_DIM, qgt, 0.0).astype(BF16)

    def scores(j, cc):
        off = pl.multiple_of(j * tk, tk)
        return _dot(k_ref[pl.ds(off, tk), :], qt_ref[:, cc * MXU_DIM:(cc + 1) * MXU_DIM])

    _cols_attention(i, tq, tk, 4 * tq, scores, vt_ref, m_ref, l_ref, acc_ref)

    o = acc_ref[...] / l_ref[...]
    odt = o[:, 0:2 * tq] - lam_ref[0] * o[:, 2 * tq:4 * tq]
    for g in range(2):
        od = odt[:, g * tq:(g + 1) * tq].T
        od = _rms_rows(od, SUBLN_EPS) * gs_ref[...]
        o_ref[:, g * LANES:(g + 1) * LANES] = od.astype(BF16)


def _diff_prompt(lam, dq, dk16, dvt, gs):
    b, t, _ = dq.shape
    tq, tk = PROMPT_Q_TILE, dvt.shape[-1]
    kern = functools.partial(_diff_prompt_kernel, tq=tq, tk=tk)
    return pl.pallas_call(
        kern,
        grid=(b, 2, t // tq),
        in_specs=[pl.BlockSpec(memory_space=pltpu.SMEM),
                  pl.BlockSpec((None, tq, 256), lambda bi, n, i: (bi, i, n)),
                  pl.BlockSpec((None, t, LANES), lambda bi, n, i: (bi, 0, n)),
                  pl.BlockSpec((None, t // tk, LANES, tk), lambda bi, n, i: (bi, 0, n, 0)),
                  pl.BlockSpec((1, LANES), lambda bi, n, i: (0, 0))],
        out_specs=pl.BlockSpec((None, tq, 256), lambda bi, n, i: (bi, i, n)),
        out_shape=jax.ShapeDtypeStruct((b, t, 512), BF16),
        scratch_shapes=[pltpu.VMEM((LANES, 4 * tq), BF16), pltpu.VMEM((1, 4 * tq), F32),
                        pltpu.VMEM((1, 4 * tq), F32), pltpu.VMEM((LANES, 4 * tq), F32)],
        compiler_params=_params(("parallel", "parallel", "parallel")),
        name="diff_prompt",
    )(lam, dq, dk16, dvt.reshape(b, t // tk, 256, tk), gs)


def _mla_prompt_kernel(q_ref, k_ref, ct_ref, wuv_ref, o_ref, qt_ref, m_ref, l_ref, acc_ref, *, tq, tk):
    i = pl.program_id(1)
    for h in range(MLA_HEADS):
        qt_ref[:, h * tq:(h + 1) * tq] = q_ref[:, h * LANES:(h + 1) * LANES].astype(F32).T.astype(BF16)

    def scores(j, h):
        off = pl.multiple_of(j * tk, tk)
        return _dot(k_ref[pl.ds(off, tk), h * LANES:(h + 1) * LANES], qt_ref[:, h * tq:(h + 1) * tq])

    _cols_attention(i, tq, tk, MLA_HEADS * tq, scores, ct_ref, m_ref, l_ref, acc_ref)

    lat_t = acc_ref[...] / l_ref[...]
    o = None
    for h in range(MLA_HEADS):
        part = _dot(lat_t[:, h * tq:(h + 1) * tq].T.astype(BF16), wuv_ref[h])
        o = part if o is None else o + part
    o_ref[...] = o.astype(BF16)


def _mla_prompt(mq, kmla, ckvt, wuv2):
    b, t, _ = mq.shape
    tq, tk = PROMPT_Q_TILE, ckvt.shape[-1]
    assert tq == MXU_DIM, "one head per MXU-wide column chunk"
    kern = functools.partial(_mla_prompt_kernel, tq=tq, tk=tk)
    cols = MLA_HEADS * tq
    return pl.pallas_call(
        kern,
        grid=(b, t // tq),
        in_specs=[pl.BlockSpec((None, tq, 1024), lambda bi, i: (bi, i, 0)),
                  pl.BlockSpec((None, t, 1024), lambda bi, i: (bi, 0, 0)),
                  pl.BlockSpec((None, t // tk, LANES, tk), lambda bi, i: (bi, 0, 0, 0)),
                  pl.BlockSpec(wuv2.shape, lambda bi, i: (0, 0, 0))],
        out_specs=pl.BlockSpec((None, tq, 512), lambda bi, i: (bi, i, 0)),
        out_shape=jax.ShapeDtypeStruct((b, t, 512), BF16),
        scratch_shapes=[pltpu.VMEM((LANES, cols), BF16), pltpu.VMEM((1, cols), F32),
                        pltpu.VMEM((1, cols), F32), pltpu.VMEM((LANES, cols), F32)],
        compiler_params=_params(("parallel", "parallel")),
        name="mla_prompt",
    )(mq, kmla, ckvt.reshape(b, t // tk, LANES, tk), wuv2)


def _softmax_pv(s, v):
    m = jnp.max(s, axis=-1, keepdims=True)
    p = jnp.exp(s - m)
    l = jnp.sum(p, axis=-1, keepdims=True)
    return _dot(p.astype(BF16), v) / l


def _mem_prompt_kernel(q_ref, k_ref, v_ref, o_ref):
    for h in range(MEM_HEADS):
        sl = slice(h * LANES, (h + 1) * LANES)
        s = _dot_nt(q_ref[:, sl], k_ref[:, sl])
        o_ref[:, sl] = _softmax_pv(s, v_ref[:, sl]).astype(BF16)


def _mem_prompt(memq, mk16, mv16):
    b, t, _ = memq.shape
    tq = min(2 * ATTN_TILE, t)
    return pl.pallas_call(
        _mem_prompt_kernel,
        grid=(b, t // tq),
        in_specs=[pl.BlockSpec((None, tq, 512), lambda bi, i: (bi, i, 0)),
                  pl.BlockSpec((None, N_MEM, 512), lambda bi, i: (bi, 0, 0)),
                  pl.BlockSpec((None, N_MEM, 512), lambda bi, i: (bi, 0, 0))],
        out_specs=pl.BlockSpec((None, tq, 512), lambda bi, i: (bi, i, 0)),
        out_shape=jax.ShapeDtypeStruct((b, t, 512), BF16),
        compiler_params=_params(("parallel", "parallel")),
        name="mem_prompt",
    )(memq, mk16, mv16)


def _mem_sample_kernel(q_ref, k_ref, v_ref, o_ref, *, bb):
    for bi in range(bb):
        for h in range(MEM_HEADS):
            sl = slice(h * LANES, (h + 1) * LANES)
            k = k_ref[bi, pl.ds(h, N_MEM, stride=MEM_HEADS), :].astype(BF16)
            v = v_ref[bi, pl.ds(h, N_MEM, stride=MEM_HEADS), :].astype(BF16)
            s = _dot_nt(q_ref[bi, :, sl], k)
            o_ref[bi, :, sl] = _softmax_pv(s, v).astype(BF16)


def _mem_sample(memq, ck, cv, b_off):
    b, t, _ = memq.shape
    bb = math.gcd(b, 4)
    off = b_off // bb
    kern = functools.partial(_mem_sample_kernel, bb=bb)
    rows = N_MEM * MEM_HEADS
    return pl.pallas_call(
        kern,
        grid=(b // bb,),
        in_specs=[pl.BlockSpec((bb, t, 512), lambda i: (i, 0, 0)),
                  pl.BlockSpec((bb, rows, LANES), lambda i: (i + off, 0, 0)),
                  pl.BlockSpec((bb, rows, LANES), lambda i: (i + off, 0, 0))],
        out_specs=pl.BlockSpec((bb, t, 512), lambda i: (i, 0, 0)),
        out_shape=jax.ShapeDtypeStruct((b, t, 512), BF16),
        compiler_params=_params(("parallel",)),
        name="mem_sample",
    )(memq, ck, cv)


def _page_copies(pt_ref, grp, j, slot, bb, ch, nc, streams):
    cps = []
    for bi in range(bb):
        base = ((grp * bb + bi) * nc + j) * ch
        for r in range(ch):
            pg = pt_ref[base + r]
            for src, buf, sem in streams:
                cps.append(pltpu.make_async_copy(src.at[pg], buf.at[slot, bi * ch + r], sem.at[slot]))
    return cps


def _stream_step(pt_ref, bb, ch, nc, streams):
    grp, j = pl.program_id(0), pl.program_id(1)
    s = grp * nc + j
    total = pl.num_programs(0) * nc
    slot = s & 1

    @pl.when(s == 0)
    def _():
        for cp in _page_copies(pt_ref, 0, 0, 0, bb, ch, nc, streams):
            cp.start()

    @pl.when(s + 1 < total)
    def _():
        last = j == nc - 1
        for cp in _page_copies(pt_ref, jnp.where(last, grp + 1, grp), jnp.where(last, 0, j + 1), 1 - slot,
                               bb, ch, nc, streams):
            cp.start()

    for cp in _page_copies(pt_ref, grp, j, slot, bb, ch, nc, streams):
        cp.wait()
    return slot


def _causal_new(s2, nq):
    qpos = lax.broadcasted_iota(jnp.int32, s2.shape, 0) & (nq - 1)
    kpos = lax.broadcasted_iota(jnp.int32, s2.shape, 1)
    return jnp.where(kpos <= qpos, s2, NEG_INF)


def _decode_kernel(pt_ref, lam_ref, qd_ref, kn_ref, vn_ref, gs_ref, mq_ref, kmn_ref, cn_ref, gk_ref, aq_ref,
                   wukt_ref, wuv_ref, kc_hbm, vc_hbm, cc_hbm, rc_hbm, od_ref, om_ref,
                   kbuf, vbuf, cbuf, rbuf, sem, md_ref, ld_ref, accd_ref, lw_ref, qr_ref, mm_ref, lm_ref, accm_ref,
                   *, bb, ch, nc):
    j = pl.program_id(1)
    streams = [(kc_hbm, kbuf, sem.at[0]), (vc_hbm, vbuf, sem.at[1]), (cc_hbm, cbuf, sem.at[2]), (rc_hbm, rbuf, sem.at[3])]
    slot = _stream_step(pt_ref, bb, ch, nc, streams)
    nq = mq_ref.shape[1]
    nr = MLA_HEADS * nq
    nk = MLA_HEADS * MLA_NOPE

    @pl.when(j == 0)
    def _():
        for ref in (md_ref, mm_ref):
            ref[...] = jnp.full(ref.shape, NEG_INF, F32)
        for ref in (ld_ref, lm_ref, accd_ref, accm_ref):
            ref[...] = jnp.zeros(ref.shape, F32)
        for bi in range(bb):
            lw_ref[bi, 0:nk, :] = wukt_ref[...]
            qg = (mq_ref[bi].astype(F32) * gk_ref[...]).astype(BF16)
            for h in range(MLA_HEADS):
                qa = _dot(qg[:, h * LANES:(h + 1) * LANES], aq_ref[h])
                lw_ref[bi, nk + h * nq:nk + (h + 1) * nq, :] = qa[:, 0:LANES].astype(BF16)
                qr_ref[bi, h * nq:(h + 1) * nq, :] = qa[:, LANES + 64:LANES + 96].astype(BF16)

    scores_d, scores_m, lat_in = [], [], []
    for bi in range(bb):
        pages = range(bi * ch, (bi + 1) * ch)
        kt = jnp.concatenate([kbuf[slot, r].astype(BF16) for r in pages], axis=-1)
        scores_d.append(_dot(qd_ref[bi], kt))
        c = jnp.concatenate([cbuf[slot, r] for r in pages], axis=0).astype(BF16)
        krt = jnp.concatenate([rbuf[slot, r] for r in pages], axis=-1)
        big = _dot_nt(lw_ref[bi], c)
        nkeys = big.shape[1]
        knt = big[0:nk]
        ssq = jnp.sum((knt * knt).reshape(MLA_HEADS, MLA_NOPE, nkeys), axis=1)
        ssq = ssq + jnp.sum(krt * krt, axis=0, keepdims=True)
        rn = lax.rsqrt(ssq * (1.0 / MLA_QK) + NORM_EPS)
        sm = big[nk:nk + nr] + _dot(qr_ref[bi], krt.astype(BF16))
        scores_m.append((sm.reshape(MLA_HEADS, nq, nkeys) * rn[:, None, :]).reshape(nr, nkeys))
        lat_in.append(c)

    for bi in range(bb):
        pages = range(bi * ch, (bi + 1) * ch)
        p, alpha = _online_update(scores_d[bi], md_ref.at[bi], ld_ref.at[bi])
        p = p.astype(BF16)
        pv = []
        for n in range(2):
            v = jnp.concatenate([vbuf[slot, r, pl.ds(n, PAGE, stride=2), :].astype(BF16) for r in pages], axis=0)
            pv.append(_dot(p[n * 32:(n + 1) * 32], v))
        accd_ref[bi] = alpha * accd_ref[bi] + jnp.concatenate(pv, axis=0)
        pm, alpham = _online_update(scores_m[bi], mm_ref.at[bi], lm_ref.at[bi])
        accm_ref[bi] = alpham * accm_ref[bi] + _dot(pm.astype(BF16), lat_in[bi])

    @pl.when(j == nc - 1)
    def _():
        for bi in range(bb):
            q = qd_ref[bi]
            vn = vn_ref[bi]
            p2, alpha2 = _online_update(_causal_new(_dot_nt(q, kn_ref[bi]), nq), md_ref.at[bi], ld_ref.at[bi])
            p2 = p2.astype(BF16)
            pv2 = [_dot(p2[n * 32:(n + 1) * 32], vn[:, n * LANES:(n + 1) * LANES]) for n in range(2)]
            o = (alpha2 * accd_ref[bi] + jnp.concatenate(pv2, axis=0)) / ld_ref[bi]
            for n in range(2):
                on = o[n * 32:(n + 1) * 32]
                od = on[0:16] - lam_ref[0] * on[16:32]
                od = _rms_rows(od, SUBLN_EPS) * gs_ref[...]
                for g in range(2):
                    hh = n * 2 + g
                    od_ref[bi, :, hh * LANES:(hh + 1) * LANES] = od[g * nq:(g + 1) * nq].astype(BF16)

            qf = mq_ref[bi]
            kn = kmn_ref[bi]
            s2 = jnp.concatenate([_dot_nt(qf[:, h * LANES:(h + 1) * LANES], kn[:, h * LANES:(h + 1) * LANES])
                                  for h in range(MLA_HEADS)], axis=0)
            p3, alpha3 = _online_update(_causal_new(s2, nq), mm_ref.at[bi], lm_ref.at[bi])
            lat = ((alpha3 * accm_ref[bi] + _dot(p3.astype(BF16), cn_ref[bi])) / lm_ref[bi]).astype(BF16)
            om = _dot(lat[0:nq], wuv_ref[0])
            for h in range(1, MLA_HEADS):
                om = om + _dot(lat[h * nq:(h + 1) * nq], wuv_ref[h])
            om_ref[bi] = om.astype(BF16)


def _decode(pt_flat, lam, qbd, kn16, vn16, gs, mq, kmla, ckv16, gk_pad, aq, wukt, wuv2, kc, vc, cc, rc, n_pages):
    b, nq, _ = mq.shape
    assert nq == 8, "score-row layouts assume 8 new tokens per sequence"
    ch = math.gcd(PAGES_PER_STEP, n_pages)
    bb = math.gcd(SEQS_PER_STEP, b)
    nc = n_pages // ch
    kern = functools.partial(_decode_kernel, bb=bb, ch=ch, nc=nc)
    nr = MLA_HEADS * nq
    per_seq = lambda w, r=nq: pl.BlockSpec((bb, r, w), lambda g, j, pt: (g, 0, 0))
    const = lambda a: pl.BlockSpec(a.shape, lambda g, j, pt: (0,) * a.ndim)
    grid_spec = pltpu.PrefetchScalarGridSpec(
        num_scalar_prefetch=1,
        grid=(b // bb, nc),
        in_specs=[pl.BlockSpec(memory_space=pltpu.SMEM), per_seq(256, 64), per_seq(256), per_seq(256), const(gs),
                  per_seq(1024), per_seq(1024), per_seq(LANES), const(gk_pad), const(aq), const(wukt), const(wuv2)]
                 + [pl.BlockSpec(memory_space=pl.ANY)] * 4,
        out_specs=[per_seq(512), per_seq(512)],
        scratch_shapes=[pltpu.VMEM((2, bb * ch, 256, LANES), F32), pltpu.VMEM((2, bb * ch, 256, LANES), F32),
                        pltpu.VMEM((2, bb * ch, PAGE, LANES), F32), pltpu.VMEM((2, bb * ch, MLA_ROPE, PAGE), F32),
                        pltpu.SemaphoreType.DMA((4, 2)),
                        pltpu.VMEM((bb, 64, 1), F32), pltpu.VMEM((bb, 64, 1), F32), pltpu.VMEM((bb, 64, LANES), F32),
                        pltpu.VMEM((bb, MLA_HEADS * MLA_NOPE + nr, LANES), BF16), pltpu.VMEM((bb, nr, MLA_ROPE), BF16),
                        pltpu.VMEM((bb, nr, 1), F32), pltpu.VMEM((bb, nr, 1), F32), pltpu.VMEM((bb, nr, LANES), F32)],
    )
    return pl.pallas_call(
        kern,
        grid_spec=grid_spec,
        out_shape=[jax.ShapeDtypeStruct((b, nq, 512), BF16)] * 2,
        compiler_params=_params(("arbitrary", "arbitrary")),
        name="decode",
    )(pt_flat, lam, qbd, kn16, vn16, gs, mq, kmla, ckv16, gk_pad, aq, wukt, wuv2, kc, vc, cc, rc)


def _merge_kernel(x_ref, od_ref, om_ref, oc_ref, gmix_ref, wg_ref, bg_ref, wbr_ref, wout_ref, gffn_ref,
                  wrh_ref, wrl_ref, br_ref, xm_ref, h2_ref, tv_ref, ti_ref):
    x = x_ref[...]
    h = (_rms_rows(x) * gmix_ref[...]).astype(BF16)
    gates = jax.nn.sigmoid(_dot(h, wg_ref[...]) + bg_ref[...])
    merged = gates[:, 0:D_MODEL] * _dot(od_ref[...], wbr_ref[0])
    merged = merged + gates[:, D_MODEL:2 * D_MODEL] * _dot(om_ref[...], wbr_ref[1])
    merged = merged + gates[:, 2 * D_MODEL:3 * D_MODEL] * _dot(oc_ref[...], wbr_ref[2])
    xm = x + _dot(merged.astype(BF16), wout_ref[...])
    xm_ref[...] = xm
    h2 = _rms_rows(xm) * gffn_ref[...]
    for s in range(D_MODEL // LANES):
        h2_ref[:, s, :] = h2[:, s * LANES:(s + 1) * LANES]
    hh = h2.astype(BF16)
    hl = (h2 - hh.astype(F32)).astype(BF16)
    logits = _dot(hh, wrh_ref[...]) + _dot(hl, wrh_ref[...]) + _dot(hh, wrl_ref[...]) + br_ref[...]
    lane = lax.broadcasted_iota(jnp.int32, logits.shape, 1)
    logits = jnp.where(lane < N_EXPERTS, logits, -jnp.inf)
    tv = jnp.zeros(logits.shape, F32)
    ti = jnp.zeros(logits.shape, jnp.int32)
    vals = []
    for k in range(TOP_K):
        mx = jnp.max(logits, axis=-1, keepdims=True)
        idx = jnp.min(jnp.where(logits == mx, lane, LANES), axis=-1, keepdims=True)
        vals.append(mx)
        ti = jnp.where(lane == k, idx, ti)
        logits = jnp.where(lane == idx, -jnp.inf, logits)
    es = [jnp.exp(v - vals[0]) for v in vals]
    den = es[0] + es[1] + es[2] + es[3]
    for k in range(TOP_K):
        tv = jnp.where(lane == k, es[k] / den, tv)
    tv_ref[...] = tv
    ti_ref[...] = ti


def _merge(x2d, od, om, oc, wts):
    n = x2d.shape[0]
    tm = min(TOKEN_TILE, n)
    row = lambda w: pl.BlockSpec((tm, w), lambda i: (i, 0))
    return pl.pallas_call(
        _merge_kernel,
        grid=(n // tm,),
        in_specs=[row(D_MODEL), row(512), row(512), row(512)] + [_full(a) for a in wts],
        out_specs=[row(D_MODEL), pl.BlockSpec((tm, 8, LANES), lambda i: (i, 0, 0)), row(LANES), row(LANES)],
        out_shape=[jax.ShapeDtypeStruct((n, D_MODEL), F32), jax.ShapeDtypeStruct((n, 8, LANES), F32),
                   jax.ShapeDtypeStruct((n, LANES), F32), jax.ShapeDtypeStruct((n, LANES), jnp.int32)],
        compiler_params=_params(("parallel",)),
        name="merge",
    )(x2d, od, om, oc, *wts)


def _expert_kernel(be_ref, first_ref, nval_ref, slot_ref, wgu_ref, bgu_ref, wd_ref, bd_ref, x_hbm, y_hbm,
                   slot_smem, xbuf, ybuf, zbuf, wgu16, wd16, gsem, ssem, isem, *, n_blocks, n_tok, n_real):
    i = pl.program_id(0)
    nv = nval_ref[i]
    groups = lambda k: lax.shift_right_logical(nval_ref[k] + (ROW_GROUP - 1), ROW_GROUP.bit_length() - 1)

    def idx_copy(k):
        ring = lax.rem(k, 3)
        return pltpu.make_async_copy(slot_ref.at[k], slot_smem.at[ring], isem.at[ring])

    def issue_gather(k):
        ring, xb = lax.rem(k, 3), k & 1

        def body(g, carry):
            for u in range(ROW_GROUP):
                r = g * ROW_GROUP + u
                tok = jnp.minimum(lax.shift_right_logical(slot_smem[ring, 0, r], 2), n_tok - 1)
                pltpu.make_async_copy(x_hbm.at[tok], xbuf.at[xb, r], gsem.at[xb]).start()
            return carry

        lax.fori_loop(0, groups(k), body, 0)

    def wait_gather(k):
        xb = k & 1

        def body(g, carry):
            r0 = pl.multiple_of(g * ROW_GROUP, ROW_GROUP)
            pltpu.make_async_copy(x_hbm.at[pl.ds(0, ROW_GROUP)], xbuf.at[xb, pl.ds(r0, ROW_GROUP)], gsem.at[xb]).wait()
            return carry

        lax.fori_loop(0, groups(k), body, 0)

    def issue_scatter(k):
        ring = lax.rem(k, 3)

        def body(g, carry):
            for u in range(ROW_GROUP):
                r = g * ROW_GROUP + u
                pltpu.make_async_copy(ybuf.at[r], y_hbm.at[slot_smem[ring, 0, r]], ssem.at[0]).start()
            return carry

        lax.fori_loop(0, groups(k), body, 0)

    def wait_scatter(k):
        def body(g, carry):
            r0 = pl.multiple_of(g * ROW_GROUP, ROW_GROUP)
            pltpu.make_async_copy(ybuf.at[pl.ds(r0, ROW_GROUP)], y_hbm.at[pl.ds(0, ROW_GROUP)], ssem.at[0]).wait()
            return carry

        lax.fori_loop(0, groups(k), body, 0)

    @pl.when(i == 0)
    def _():
        xbuf[...] = jnp.zeros(xbuf.shape, F32)
        zbuf[...] = jnp.zeros(zbuf.shape, F32)
        spare = pltpu.make_async_copy(zbuf, y_hbm.at[pl.ds(n_real, ROW_GROUP)], ssem.at[0])
        spare.start()
        spare.wait()
        first_idx = idx_copy(0)
        first_idx.start()
        first_idx.wait()
        issue_gather(0)
        if n_blocks > 1:
            idx_copy(1).start()

    @pl.when(i + 1 < n_blocks)
    def _():
        idx_copy(i + 1).wait()
        issue_gather(i + 1)

    @pl.when(i + 2 < n_blocks)
    def _():
        idx_copy(i + 2).start()

    @pl.when(nv > 0)
    def _():
        @pl.when(first_ref[i] == 1)
        def _():
            wgu16[...] = wgu_ref[...].astype(BF16)
            wd16[...] = wd_ref[...].astype(BF16)

        wait_gather(i)
        xb = i & 1
        x = jnp.concatenate([xbuf[xb, :, s, :] for s in range(D_MODEL // LANES)], axis=-1).astype(BF16)
        gu = _dot(x, wgu16[...]) + bgu_ref[...]
        gate = jnp.minimum(gu[:, 0:D_EXPERT], SWIGLU_LIMIT)
        up = jnp.clip(gu[:, D_EXPERT:2 * D_EXPERT], -SWIGLU_LIMIT, SWIGLU_LIMIT)
        act = (up + 1.0) * gate * jax.nn.sigmoid(SWIGLU_ALPHA * gate)
        y = _dot(act.astype(BF16), wd16[...]) + bd_ref[...]

        @pl.when(i > 0)
        def _():
            wait_scatter(i - 1)

        for s in range(D_MODEL // LANES):
            ybuf[:, s, :] = y[:, s * LANES:(s + 1) * LANES]
        issue_scatter(i)

    @pl.when((nv == 0) & (i > 0))
    def _():
        wait_scatter(i - 1)

    @pl.when(i == n_blocks - 1)
    def _():
        wait_scatter(i)


def _experts(block_e, first, nval, row_slot3, wgu, bgu, wd, bd, x3, n_real):
    n_blocks = row_slot3.shape[0]
    bm = row_slot3.shape[2]
    n_slots = n_real + ROW_GROUP
    kern = functools.partial(_expert_kernel, n_blocks=n_blocks, n_tok=x3.shape[0], n_real=n_real)
    grid_spec = pltpu.PrefetchScalarGridSpec(
        num_scalar_prefetch=3,
        grid=(n_blocks,),
        in_specs=[pl.BlockSpec(row_slot3.shape, lambda i, be, fi, na: (0, 0, 0)),
                  pl.BlockSpec((None, D_MODEL, 2 * D_EXPERT), lambda i, be, fi, na: (be[i], 0, 0)),
                  pl.BlockSpec((None, 1, 2 * D_EXPERT), lambda i, be, fi, na: (be[i], 0, 0)),
                  pl.BlockSpec((None, D_EXPERT, D_MODEL), lambda i, be, fi, na: (be[i], 0, 0)),
                  pl.BlockSpec((None, 1, D_MODEL), lambda i, be, fi, na: (be[i], 0, 0)),
                  pl.BlockSpec(memory_space=pl.ANY)],
        out_specs=pl.BlockSpec(memory_space=pl.ANY),
        scratch_shapes=[pltpu.SMEM((3, 1, bm), jnp.int32),
                        pltpu.VMEM((2, bm, 8, LANES), F32), pltpu.VMEM((bm, 8, LANES), F32),
                        pltpu.VMEM((ROW_GROUP, 8, LANES), F32),
                        pltpu.VMEM((D_MODEL, 2 * D_EXPERT), BF16), pltpu.VMEM((D_EXPERT, D_MODEL), BF16),
                        pltpu.SemaphoreType.DMA((2,)), pltpu.SemaphoreType.DMA((1,)), pltpu.SemaphoreType.DMA((3,))],
    )
    return pl.pallas_call(
        kern,
        grid_spec=grid_spec,
        out_shape=jax.ShapeDtypeStruct((n_slots, 8, LANES), F32),
        compiler_params=_params(("arbitrary",)),
        name="experts",
    )(block_e, first, nval, row_slot3, wgu, bgu, wd, bd, x3)


def _combine_kernel(xm_ref, tv_ref, y_ref, o_ref):
    tv = tv_ref[...]
    for s in range(D_MODEL // LANES):
        acc = xm_ref[:, s * LANES:(s + 1) * LANES]
        for k in range(TOP_K):
            acc = acc + tv[:, k:k + 1] * y_ref[:, k * 8 + s, :]
        o_ref[:, s * LANES:(s + 1) * LANES] = acc


def _combine(xm, tv, yslots, tok_off):
    n = xm.shape[0]
    tm = min(TOKEN_TILE, n)
    off = tok_off // tm
    y4 = yslots.reshape(yslots.shape[0] // TOP_K, TOP_K * 8, LANES)
    return pl.pallas_call(
        _combine_kernel,
        grid=(n // tm,),
        in_specs=[pl.BlockSpec((tm, D_MODEL), lambda i: (i, 0)),
                  pl.BlockSpec((tm, LANES), lambda i: (i, 0)),
                  pl.BlockSpec((tm, TOP_K * 8, LANES), lambda i: (i + off, 0, 0))],
        out_specs=pl.BlockSpec((tm, D_MODEL), lambda i: (i, 0)),
        out_shape=jax.ShapeDtypeStruct((n, D_MODEL), F32),
        compiler_params=_params(("parallel",)),
        name="combine",
    )(xm, tv, y4)


def _rope_tables(pos, rows):
    pos = pos.astype(F32)[:, None]
    lane = jnp.arange(LANES)

    def ang(half):
        inv = jnp.power(ROPE_THETA, -jnp.arange(half, dtype=F32) / half)
        return pos * inv[None, :]

    a64 = ang(32)[:, lane % 32]
    first = (lane % 64) < 32
    c64 = jnp.cos(a64)
    sa64 = jnp.where(first, -jnp.sin(a64), 0.0)
    sb64 = jnp.where(first, 0.0, jnp.sin(a64))
    a32 = ang(16)[:, lane % 16]
    in_a = (lane >= 64) & (lane < 80)
    in_b = (lane >= 80) & (lane < 96)
    c32 = jnp.where(in_a | in_b, jnp.cos(a32), 1.0)
    sa32 = jnp.where(in_a, -jnp.sin(a32), 0.0)
    sb32 = jnp.where(in_b, jnp.sin(a32), 0.0)
    tabs = [c64, sa64, sb64, c32, sa32, sb32]
    reps = rows // pos.shape[0]
    return [jnp.tile(t, (reps, 1)) if reps > 1 else t for t in tabs]


def _tied_pad(g, scale):
    blk = jnp.concatenate([g[:MLA_NOPE], g[MLA_NOPE:], g[MLA_NOPE:], jnp.zeros((32,), F32)]) * scale
    return jnp.tile(blk, MLA_HEADS)[None, :]


def _layer_weights(lp):
    w_in = lp["w_in"]
    kr_blk = jnp.zeros((D_MODEL, LANES), F32).at[:, 64:96].set(w_in[:, _OFF_KR:_OFF_MQ])
    wa = jnp.concatenate([w_in[:, _OFF_Q:_OFF_KR], kr_blk, w_in[:, _OFF_MQ:_OFF_G]], axis=1).astype(BF16)
    wuq = jnp.pad(lp["w_mla_uq"], ((0, 0), (0, 0), (0, LANES - MLA_QK))).reshape(MLA_Q_LORA, MLA_HEADS * LANES).astype(BF16)
    wuk_pad = jnp.pad(lp["w_mla_uk"], ((0, 0), (0, 0), (0, LANES - MLA_NOPE))).reshape(MLA_KV_LORA, MLA_HEADS * LANES)
    lane = jnp.arange(LANES)
    rope_eye = jnp.where(((lane >= 64) & (lane < 96))[:, None], jnp.eye(LANES, dtype=F32), 0.0)
    wk = jnp.concatenate([wuk_pad, jnp.tile(rope_eye, (1, MLA_HEADS))], axis=0).astype(BF16)
    g64 = jnp.kron(jnp.eye(4, dtype=F32), jnp.ones((64, 64), F32)).astype(BF16)
    g128 = jnp.kron(jnp.eye(2, dtype=F32), jnp.ones((128, 128), F32)).astype(BF16)
    gk_pad = _tied_pad(lp["mla_k_norm"], 1.0)
    inproj_w = [lp["norm_mix"][None, :], wa, wuq, wk, g64, g128,
                jnp.tile(lp["diff_q_norm"], 8)[None, :] * DIFF_SCALE, jnp.tile(lp["diff_k_norm"], 4)[None, :],
                lp["mla_q_a_norm"][None, :], lp["mla_kv_a_norm"][None, :],
                _tied_pad(lp["mla_q_norm"], MLA_SCALE), gk_pad,
                jnp.tile(lp["mem_q_norm"], MEM_HEADS)[None, :] * MEM_SCALE]
    wuv = lp["w_mla_uv"]
    wuv2 = jnp.einsum("rhd,hg->hrgd", wuv, jnp.eye(MLA_HEADS, dtype=F32)).reshape(MLA_HEADS, MLA_KV_LORA, 512).astype(BF16)
    wuk_t = jnp.transpose(lp["w_mla_uk"], (1, 2, 0))
    aq = jnp.zeros((MLA_HEADS, LANES, 2 * LANES), F32).at[:, 0:MLA_NOPE, 0:LANES].set(wuk_t)
    aq = aq.at[:, :, LANES:].add(rope_eye[None]).astype(BF16)
    wukt = wuk_t.reshape(MLA_HEADS * MLA_NOPE, MLA_KV_LORA).astype(BF16)
    wr = jnp.pad(lp["w_router"], ((0, 0), (0, LANES - N_EXPERTS)))
    wrh = wr.astype(BF16)
    wrl = (wr - wrh.astype(F32)).astype(BF16)
    merge_w = [lp["norm_mix"][None, :], w_in[:, _OFF_G:].astype(BF16), lp["b_gate"][None, :],
               lp["w_branch"].reshape(3, 512, D_MODEL).astype(BF16), lp["w_out"].astype(BF16),
               lp["norm_ffn"][None, :], wrh, wrl, jnp.pad(lp["b_router"], (0, LANES - N_EXPERTS))[None, :]]
    return dict(inproj=inproj_w, g128=g128, wuv2=wuv2, aq=aq, wukt=wukt, gk_pad=gk_pad, merge=merge_w)


def _moe_plan(ti, bm):
    n = ti.shape[0]
    a = n * TOP_K
    flat_e = ti[:, :TOP_K].reshape(a)
    onehot = (flat_e[:, None] == jnp.arange(N_EXPERTS, dtype=jnp.int32)[None, :]).astype(jnp.int32)
    csum = jnp.cumsum(onehot, axis=0)
    counts = csum[-1]
    rank = jnp.sum(onehot * csum, axis=1) - 1
    padded = (counts + bm - 1) // bm * bm
    pend = jnp.cumsum(padded)
    pstart = pend - padded
    dest = jnp.sum(onehot * pstart[None, :], axis=1) + rank
    n_rows = (a + N_EXPERTS * (bm - 1) + bm - 1) // bm * bm
    n_blocks = n_rows // bm
    spare = a + (jnp.arange(n_rows, dtype=jnp.int32) % ROW_GROUP)
    row_slot = spare.at[dest].set(jnp.arange(a, dtype=jnp.int32))
    blk_start = jnp.arange(n_blocks, dtype=jnp.int32) * bm
    block_e = jnp.minimum(jnp.sum((blk_start[:, None] >= pend[None, :]).astype(jnp.int32), axis=1), N_EXPERTS - 1)
    first = jnp.concatenate([jnp.ones((1,), jnp.int32), (block_e[1:] != block_e[:-1]).astype(jnp.int32)])
    eh = (block_e[:, None] == jnp.arange(N_EXPERTS, dtype=jnp.int32)[None, :]).astype(jnp.int32)
    valid_end = jnp.sum(eh * (pstart + counts)[None, :], axis=1)
    nval = jnp.where(blk_start < pend[-1], jnp.clip(valid_end - blk_start, 0, bm), 0).astype(jnp.int32)
    return block_e, first, nval, row_slot.reshape(n_blocks, 1, bm), a


def _qbd(dq_s, b, nq):
    q = dq_s.reshape(b, nq, 2, 2, 2, DIFF_HEAD_DIM)
    q = jnp.transpose(q, (0, 2, 4, 3, 1, 5))
    eye = jnp.eye(4, dtype=q.dtype).reshape(2, 2, 2, 2)
    out = jnp.einsum("bncgqd,ncmk->bncgqmkd", q, eye)
    return out.reshape(b, 64, 256)


def kernel(x_prompt, x_sample, mem_prompt, cache_diff_k, cache_diff_v, cache_mla_ckv, cache_mla_krope, cache_mem_k, cache_mem_v, page_table, norm_mix, norm_mem, w_in, b_gate, diff_q_norm, diff_k_norm, diff_lambda, diff_subln, mla_q_a_norm, w_mla_uq, mla_kv_a_norm, w_mla_uk, w_mla_uv, mla_q_norm, mla_k_norm, w_mem_kv, mem_q_norm, mem_k_norm, w_branch, w_out, norm_ffn, w_router, b_router, w_gate_up, b_gate_up, w_down, b_down):
    depth = w_in.shape[0]
    bp, t, _ = x_prompt.shape
    bs, nq, _ = x_sample.shape
    n_pool, n_pages = cache_diff_k.shape[1], page_table.shape[1]
    past_len = n_pages * PAGE
    n_p, n_s = bp * t, bs * nq

    tm_p, tm_s = min(TOKEN_TILE, n_p), min(TOKEN_TILE, n_s)
    tabs_p = _rope_tables(jnp.arange(t, dtype=jnp.int32), t)
    tabs_s = _rope_tables(past_len + jnp.arange(nq, dtype=jnp.int32), tm_s)
    del tm_p

    kc = jnp.transpose(cache_diff_k, (0, 1, 3, 4, 5, 2)).reshape(depth * n_pool, 256, PAGE)
    vc = cache_diff_v.reshape(depth * n_pool, 2 * PAGE, DIFF_V_DIM)
    cc = cache_mla_ckv.reshape(depth * n_pool, PAGE, MLA_KV_LORA)
    rc = jnp.transpose(cache_mla_krope, (0, 1, 3, 2)).reshape(depth * n_pool, MLA_ROPE, PAGE)
    mkc = cache_mem_k.reshape(depth * bs, N_MEM * MEM_HEADS, MEM_HEAD_DIM)
    mvc = cache_mem_v.reshape(depth * bs, N_MEM * MEM_HEADS, MEM_HEAD_DIM)

    xp = x_prompt.reshape(n_p, D_MODEL)
    xs = x_sample.reshape(n_s, D_MODEL)
    outs = [[] for _ in range(10)]
    for layer in range(depth):
        lp = dict(norm_mix=norm_mix[layer], w_in=w_in[layer], b_gate=b_gate[layer], diff_q_norm=diff_q_norm[layer],
                  diff_k_norm=diff_k_norm[layer], mla_q_a_norm=mla_q_a_norm[layer], w_mla_uq=w_mla_uq[layer],
                  mla_kv_a_norm=mla_kv_a_norm[layer], w_mla_uk=w_mla_uk[layer], w_mla_uv=w_mla_uv[layer],
                  mla_q_norm=mla_q_norm[layer], mla_k_norm=mla_k_norm[layer], mem_q_norm=mem_q_norm[layer],
                  w_branch=w_branch[layer], w_out=w_out[layer], norm_ffn=norm_ffn[layer],
                  w_router=w_router[layer], b_router=b_router[layer])
        w = _layer_weights(lp)
        lam_init = 0.8 - 0.6 * math.exp(-0.3 * layer)
        lamp = diff_lambda[layer].astype(F32)
        lam = (jnp.exp(jnp.sum(lamp[0] * lamp[1])) - jnp.exp(jnp.sum(lamp[2] * lamp[3])) + lam_init).reshape(1)
        gs = (diff_subln[layer] * (1.0 - lam_init))[None, :]
        pt_flat = (page_table + layer * n_pool).reshape(-1).astype(jnp.int32)

        (dq, dk32, dk16, dv32, dv16, mq, ckv32, ckv16, kr32, kmla, memq, dvt, ckvt) = _inproj(xp, tabs_p, w["inproj"])
        o_diff = _diff_prompt(lam, dq.reshape(bp, t, 512), dk16.reshape(bp, t, 256), dvt, gs)
        o_mla = _mla_prompt(mq.reshape(bp, t, 1024), kmla.reshape(bp, t, 1024), ckvt, w["wuv2"])
        mk32, mk16, mv32, mv16 = _memkv(mem_prompt.reshape(bp * N_MEM, D_MODEL), norm_mem[layer][None, :],
                                        w_mem_kv[layer].astype(BF16), w["g128"],
                                        jnp.tile(mem_k_norm[layer], MEM_HEADS)[None, :])
        o_mem = _mem_prompt(memq.reshape(bp, t, 512), mk16.reshape(bp, N_MEM, 512), mv16.reshape(bp, N_MEM, 512))
        xm_p, h2_p, tv_p, ti_p = _merge(xp, o_diff.reshape(n_p, 512), o_mla.reshape(n_p, 512),
                                        o_mem.reshape(n_p, 512), w["merge"])
        for lst, val in zip(outs[:6], (dk32.reshape(bp, t, 2, 2, 64), dv32.reshape(bp, t, 2, 128),
                                       ckv32.reshape(bp, t, 128), kr32.reshape(bp, t, 32),
                                       mk32.reshape(bp, N_MEM, 4, 128), mv32.reshape(bp, N_MEM, 4, 128))):
            lst.append(val)

        (dq, dk32, dk16, dv32, dv16, mq, ckv32, ckv16, kr32, kmla, memq, _, _) = _inproj(xs, tabs_s, w["inproj"])
        o_diff, o_mla = _decode(pt_flat, lam, _qbd(dq, bs, nq), dk16.reshape(bs, nq, 256), dv16.reshape(bs, nq, 256), gs,
                                mq.reshape(bs, nq, 1024), kmla.reshape(bs, nq, 1024), ckv16.reshape(bs, nq, LANES),
                                w["gk_pad"], w["aq"], w["wukt"], w["wuv2"], kc, vc, cc, rc, n_pages)
        o_mem = _mem_sample(memq.reshape(bs, nq, 512), mkc, mvc, layer * bs)
        xm_s, h2_s, tv_s, ti_s = _merge(xs, o_diff.reshape(n_s, 512), o_mla.reshape(n_s, 512),
                                        o_mem.reshape(n_s, 512), w["merge"])
        for lst, val in zip(outs[6:], (dk32.reshape(bs, nq, 2, 2, 64), dv32.reshape(bs, nq, 2, 128),
                                       ckv32.reshape(bs, nq, 128), kr32.reshape(bs, nq, 32))):
            lst.append(val)

        h2 = jnp.concatenate([h2_p, h2_s], axis=0)
        ti = jnp.concatenate([ti_p, ti_s], axis=0)
        block_e, first, nval, row_slot3, n_slots = _moe_plan(ti, MOE_ROWS)
        yslots = _experts(block_e + layer * N_EXPERTS, first, nval, row_slot3,
                          w_gate_up.reshape(depth * N_EXPERTS, D_MODEL, 2 * D_EXPERT),
                          b_gate_up.reshape(depth * N_EXPERTS, 1, 2 * D_EXPERT),
                          w_down.reshape(depth * N_EXPERTS, D_EXPERT, D_MODEL),
                          b_down.reshape(depth * N_EXPERTS, 1, D_MODEL), h2, n_slots)
        xp = _combine(xm_p, tv_p, yslots, 0)
        xs = _combine(xm_s, tv_s, yslots, n_p)

    stack = lambda lst: jnp.stack(lst)
    return (xp.reshape(bp, t, D_MODEL), xs.reshape(bs, nq, D_MODEL)) + tuple(stack(o) for o in outs)
```

```python
import functools
import math

import jax
import jax.numpy as jnp
from jax import lax
from jax.experimental import pallas as pl
from jax.experimental.pallas import tpu as pltpu

F32 = jnp.float32
BF16 = jnp.bfloat16

D_MODEL = 1024
DIFF_HEAD_DIM = 64
DIFF_V_DIM = 128
MLA_HEADS = 8
MLA_Q_LORA = 256
MLA_KV_LORA = 128
MLA_NOPE = 64
MLA_ROPE = 32
MLA_V = 64
MLA_QK = MLA_NOPE + MLA_ROPE
MEM_HEADS = 4
MEM_HEAD_DIM = 128
N_MEM = 256
N_EXPERTS = 32
TOP_K = 4
D_EXPERT = 1024
SWIGLU_ALPHA = 1.702
SWIGLU_LIMIT = 7.0
ROPE_THETA = 10000.0
NORM_EPS = 1e-6
SUBLN_EPS = 1e-5
NEG_INF = -1e30
PAGE = 128

DIFF_SCALE = DIFF_HEAD_DIM ** -0.5
MLA_SCALE = MLA_QK ** -0.5
MEM_SCALE = MEM_HEAD_DIM ** -0.5

LANES = 128
MXU_DIM = 256
VMEM_LIMIT = 52 * 1024 * 1024
TOKEN_TILE = 256
ATTN_TILE = 256
PROMPT_Q_TILE = 256
PAGES_PER_STEP = 8
SEQS_PER_STEP = 4
MOE_ROWS = 256
ROW_GROUP = 32

_OFF_Q, _OFF_K, _OFF_V, _OFF_CQ, _OFF_CKV, _OFF_KR, _OFF_MQ, _OFF_G = 0, 512, 768, 1024, 1280, 1408, 1440, 1952


def _params(sem):
    return pltpu.CompilerParams(dimension_semantics=sem, vmem_limit_bytes=VMEM_LIMIT)


def _full(a):
    nd = a.ndim
    return pl.BlockSpec(a.shape, lambda *_: (0,) * nd)


def _dot(a, b):
    return jnp.dot(a, b, preferred_element_type=F32)


def _dot_nt(a, b):
    return lax.dot_general(a, b, (((1,), (1,)), ((), ())), preferred_element_type=F32)


def _rms_rows(x, eps=NORM_EPS):
    return x * lax.rsqrt(jnp.mean(x * x, axis=-1, keepdims=True) + eps)


def _group_sumsq(v, g_ref):
    sq = (v * v).astype(BF16)
    parts = [_dot(sq[:, j * MXU_DIM:(j + 1) * MXU_DIM], g_ref[...]) for j in range(v.shape[1] // MXU_DIM)]
    return parts[0] if len(parts) == 1 else jnp.concatenate(parts, axis=-1)


def _rope_lanes(v, c, sa, sb, half):
    outs = []
    for j in range(v.shape[1] // LANES):
        b = v[:, j * LANES:(j + 1) * LANES]
        outs.append(b * c + pltpu.roll(b, LANES - half, 1) * sa + pltpu.roll(b, half, 1) * sb)
    return outs[0] if len(outs) == 1 else jnp.concatenate(outs, axis=-1)


def _inproj_kernel(x_ref, gmix_ref, wa_ref, wuq_ref, wk_ref, g64_ref, g128_ref,
                   gq_ref, gk_ref, gcq_ref, gckv_ref, gmq_ref, gkm_ref, gmemq_ref,
                   c64_ref, sa64_ref, sb64_ref, c32_ref, sa32_ref, sb32_ref,
                   dq_ref, dk32_ref, dk16_ref, dv32_ref, dv16_ref, mq_ref,
                   ckv32_ref, ckv16_ref, kr32_ref, kmla_ref, memq_ref, dvt_ref, ckvt_ref):
    x = x_ref[...]
    h = (_rms_rows(x) * gmix_ref[...]).astype(BF16)
    z = _dot(h, wa_ref[...])
    c64, sa64, sb64 = c64_ref[...], sa64_ref[...], sb64_ref[...]
    c32, sa32, sb32 = c32_ref[...], sa32_ref[...], sb32_ref[...]

    zq = z[:, 0:512]
    qn = zq * lax.rsqrt(_group_sumsq(zq, g64_ref) * (1.0 / DIFF_HEAD_DIM) + NORM_EPS) * gq_ref[...]
    dq_ref[...] = _rope_lanes(qn, c64, sa64, sb64, 32).astype(BF16)

    zk = z[:, 512:768]
    kn = zk * lax.rsqrt(_group_sumsq(zk, g64_ref) * (1.0 / DIFF_HEAD_DIM) + NORM_EPS) * gk_ref[...]
    dk = _rope_lanes(kn, c64, sa64, sb64, 32)
    dk32_ref[...] = dk
    dk16_ref[...] = dk.astype(BF16)

    dv = z[:, 768:1024]
    dv32_ref[...] = dv
    dv16_ref[...] = dv.astype(BF16)
    dvt_ref[...] = dv.T.astype(BF16)

    cq = (_rms_rows(z[:, 1024:1280]) * gcq_ref[...]).astype(BF16)
    mqr = _dot(cq, wuq_ref[...])
    mqn = mqr * lax.rsqrt(_group_sumsq(mqr, g128_ref) * (1.0 / MLA_QK) + NORM_EPS) * gmq_ref[...]
    mq_ref[...] = _rope_lanes(mqn, c32, sa32, sb32, 16).astype(BF16)

    ckv = _rms_rows(z[:, 1280:1408]) * gckv_ref[...]
    ckv32_ref[...] = ckv
    ckv16 = ckv.astype(BF16)
    ckv16_ref[...] = ckv16
    ckvt_ref[...] = ckv.T.astype(BF16)

    krb = _rope_lanes(z[:, 1408:1536], c32, sa32, sb32, 16)
    kr32_ref[...] = krb[:, 64:96]
    kin = jnp.concatenate([ckv16, krb.astype(BF16)], axis=-1)
    kraw = _dot(kin, wk_ref[...])
    kmla_ref[...] = (kraw * lax.rsqrt(_group_sumsq(kraw, g128_ref) * (1.0 / MLA_QK) + NORM_EPS)
                     * gkm_ref[...]).astype(BF16)

    zm = z[:, 1536:2048]
    memq_ref[...] = (zm * lax.rsqrt(_group_sumsq(zm, g128_ref) * (1.0 / MEM_HEAD_DIM) + NORM_EPS)
                     * gmemq_ref[...]).astype(BF16)


def _inproj(x2d, tabs, wts):
    n = x2d.shape[0]
    tm = min(TOKEN_TILE, n)
    period = tabs[0].shape[0] // tm
    row = lambda w: pl.BlockSpec((tm, w), lambda i: (i, 0))
    tab = pl.BlockSpec((tm, LANES), lambda i: (i % period, 0))
    out_w = [(512, BF16), (256, F32), (256, BF16), (256, F32), (256, BF16), (1024, BF16),
             (128, F32), (128, BF16), (32, F32), (1024, BF16), (512, BF16)]
    tile_t = lambda d: pl.BlockSpec((None, d, tm), lambda i: (i, 0, 0))
    return pl.pallas_call(
        _inproj_kernel,
        grid=(n // tm,),
        in_specs=[row(D_MODEL)] + [_full(a) for a in wts] + [tab] * 6,
        out_specs=[row(w) for w, _ in out_w] + [tile_t(256), tile_t(LANES)],
        out_shape=[jax.ShapeDtypeStruct((n, w), dt) for w, dt in out_w]
                  + [jax.ShapeDtypeStruct((n // tm, 256, tm), BF16), jax.ShapeDtypeStruct((n // tm, LANES, tm), BF16)],
        compiler_params=_params(("parallel",)),
        name="inproj",
    )(x2d, *wts, *tabs)


def _memkv_kernel(x_ref, g_ref, w_ref, g128_ref, gk_ref, k32_ref, k16_ref, v32_ref, v16_ref):
    h = (_rms_rows(x_ref[...]) * g_ref[...]).astype(BF16)
    kv = _dot(h, w_ref[...])
    k = kv[:, 0:512]
    k = k * lax.rsqrt(_group_sumsq(k, g128_ref) * (1.0 / MEM_HEAD_DIM) + NORM_EPS) * gk_ref[...]
    v = kv[:, 512:1024]
    k32_ref[...] = k
    k16_ref[...] = k.astype(BF16)
    v32_ref[...] = v
    v16_ref[...] = v.astype(BF16)


def _memkv(mem2d, g, w, g128, gk):
    n = mem2d.shape[0]
    tm = min(TOKEN_TILE, n)
    row = lambda w_: pl.BlockSpec((tm, w_), lambda i: (i, 0))
    return pl.pallas_call(
        _memkv_kernel,
        grid=(n // tm,),
        in_specs=[row(D_MODEL), _full(g), _full(w), _full(g128), _full(gk)],
        out_specs=[row(512)] * 4,
        out_shape=[jax.ShapeDtypeStruct((n, 512), dt) for dt in (F32, BF16, F32, BF16)],
        compiler_params=_params(("parallel",)),
        name="memkv",
    )(mem2d, g, w, g128, gk)


def _online_update(s, m_ref, l_ref, rows=None):
    sl = slice(None) if rows is None else rows
    m_old = m_ref[sl, :]
    m_new = jnp.maximum(m_old, jnp.max(s, axis=-1, keepdims=True))
    alpha = jnp.exp(m_old - m_new)
    p = jnp.exp(s - m_new)
    l_ref[sl, :] = alpha * l_ref[sl, :] + jnp.sum(p, axis=-1, keepdims=True)
    m_ref[sl, :] = m_new
    return p, alpha


def _cols_softmax_step(s, m_ref, l_ref, cols):
    m_old = m_ref[:, cols]
    m_new = jnp.maximum(m_old, jnp.max(s, axis=0, keepdims=True))
    alpha = jnp.exp(m_old - m_new)
    p = jnp.exp(s - m_new)
    l_ref[:, cols] = alpha * l_ref[:, cols] + jnp.sum(p, axis=0, keepdims=True)
    m_ref[:, cols] = m_new
    return p, alpha


def _causal_cols(s, tq, delta):
    row = lax.broadcasted_iota(jnp.int32, s.shape, 0)
    col = lax.broadcasted_iota(jnp.int32, s.shape, 1) & (tq - 1)
    return jnp.where(row <= col + delta, s, NEG_INF)


def _cols_attention(i, tq, tk, n_cols, score_fn, vt_ref, m_ref, l_ref, acc_ref):
    m_ref[...] = jnp.full(m_ref.shape, NEG_INF, F32)
    l_ref[...] = jnp.zeros(l_ref.shape, F32)
    acc_ref[...] = jnp.zeros(acc_ref.shape, F32)
    j_last = (i * tq) // tk
    delta = i * tq - j_last * tk

    n_chunks = n_cols // MXU_DIM

    def scores(j):
        return tuple(score_fn(j, cc) for cc in range(n_chunks))

    def finish(j, s_all, masked):
        vt = vt_ref[j]
        for cc in range(n_chunks):
            cols = slice(cc * MXU_DIM, (cc + 1) * MXU_DIM)
            s = _causal_cols(s_all[cc], tq, delta) if masked else s_all[cc]
            p, alpha = _cols_softmax_step(s, m_ref, l_ref, cols)
            acc_ref[:, cols] = alpha * acc_ref[:, cols] + _dot(vt, p.astype(BF16))

    def body(j, s_cur):
        s_next = scores(j + 1)
        finish(j, s_cur, False)
        return s_next

    s_last = lax.fori_loop(0, j_last, body, scores(0))
    finish(j_last, s_last, True)


def _diff_prompt_kernel(lam_ref, q_ref, k_ref, vt_ref, gs_ref, o_ref, qt_ref, m_ref, l_ref, acc_ref, *, tq, tk):
    i = pl.program_id(2)
    dim = lax.broadcasted_iota(jnp.int32, (LANES, tq), 0)
    for g in range(2):
        qgt = q_ref[:, g * LANES:(g + 1) * LANES].astype(F32).T
        qt_ref[:, g * tq:(g + 1) * tq] = jnp.where(dim < DIFF_HEAD_DIM, qgt, 0.0).astype(BF16)
        qt_ref[:, (2 + g) * tq:(3 + g) * tq] = jnp.where(dim >= DIFF_HEAD_DIM, qgt, 0.0).astype(BF16)

    def scores(j, cc):
        off = pl.multiple_of(j * tk, tk)
        return _dot(k_ref[pl.ds(off, tk), :], qt_ref[:, cc * MXU_DIM:(cc + 1) * MXU_DIM])

    _cols_attention(i, tq, tk, 4 * tq, scores, vt_ref, m_ref, l_ref, acc_ref)

    o = acc_ref[...] / l_ref[...]
    odt = o[:, 0:2 * tq] - lam_ref[0] * o[:, 2 * tq:4 * tq]
    for g in range(2):
        od = odt[:, g * tq:(g + 1) * tq].T
        od = _rms_rows(od, SUBLN_EPS) * gs_ref[...]
        o_ref[:, g * LANES:(g + 1) * LANES] = od.astype(BF16)


def _diff_prompt(lam, dq, dk16, dvt, gs):
    b, t, _ = dq.shape
    tq, tk = PROMPT_Q_TILE, dvt.shape[-1]
    kern = functools.partial(_diff_prompt_kernel, tq=tq, tk=tk)
    return pl.pallas_call(
        kern,
        grid=(b, 2, t // tq),
        in_specs=[pl.BlockSpec(memory_space=pltpu.SMEM),
                  pl.BlockSpec((None, tq, 256), lambda bi, n, i: (bi, i, n)),
                  pl.BlockSpec((None, t, LANES), lambda bi, n, i: (bi, 0, n)),
                  pl.BlockSpec((None, t // tk, LANES, tk), lambda bi, n, i: (bi, 0, n, 0)),
                  pl.BlockSpec((1, LANES), lambda bi, n, i: (0, 0))],
        out_specs=pl.BlockSpec((None, tq, 256), lambda bi, n, i: (bi, i, n)),
        out_shape=jax.ShapeDtypeStruct((b, t, 512), BF16),
        scratch_shapes=[pltpu.VMEM((LANES, 4 * tq), BF16), pltpu.VMEM((1, 4 * tq), F32),
                        pltpu.VMEM((1, 4 * tq), F32), pltpu.VMEM((LANES, 4 * tq), F32)],
        compiler_params=_params(("parallel", "parallel", "parallel")),
        name="diff_prompt",
    )(lam, dq, dk16, dvt.reshape(b, t // tk, 256, tk), gs)


def _mla_prompt_kernel(q_ref, k_ref, ct_ref, wuv_ref, o_ref, qt_ref, m_ref, l_ref, acc_ref, *, tq, tk):
    i = pl.program_id(1)
    for h in range(MLA_HEADS):
        qt_ref[:, h * tq:(h + 1) * tq] = q_ref[:, h * LANES:(h + 1) * LANES].astype(F32).T.astype(BF16)

    def scores(j, h):
        off = pl.multiple_of(j * tk, tk)
        return _dot(k_ref[pl.ds(off, tk), h * LANES:(h + 1) * LANES], qt_ref[:, h * tq:(h + 1) * tq])

    _cols_attention(i, tq, tk, MLA_HEADS * tq, scores, ct_ref, m_ref, l_ref, acc_ref)

    lat_t = acc_ref[...] / l_ref[...]
    o = None
    for h in range(MLA_HEADS):
        part = _dot(lat_t[:, h * tq:(h + 1) * tq].T.astype(BF16), wuv_ref[h])
        o = part if o is None else o + part
    o_ref[...] = o.astype(BF16)


def _mla_prompt(mq, kmla, ckvt, wuv2):
    b, t, _ = mq.shape
    tq, tk = PROMPT_Q_TILE, ckvt.shape[-1]
    assert tq == MXU_DIM, "one head per MXU-wide column chunk"
    kern = functools.partial(_mla_prompt_kernel, tq=tq, tk=tk)
    cols = MLA_HEADS * tq
    return pl.pallas_call(
        kern,
        grid=(b, t // tq),
        in_specs=[pl.BlockSpec((None, tq, 1024), lambda bi, i: (bi, i, 0)),
                  pl.BlockSpec((None, t, 1024), lambda bi, i: (bi, 0, 0)),
                  pl.BlockSpec((None, t // tk, LANES, tk), lambda bi, i: (bi, 0, 0, 0)),
                  pl.BlockSpec(wuv2.shape, lambda bi, i: (0, 0, 0))],
        out_specs=pl.BlockSpec((None, tq, 512), lambda bi, i: (bi, i, 0)),
        out_shape=jax.ShapeDtypeStruct((b, t, 512), BF16),
        scratch_shapes=[pltpu.VMEM((LANES, cols), BF16), pltpu.VMEM((1, cols), F32),
                        pltpu.VMEM((1, cols), F32), pltpu.VMEM((LANES, cols), F32)],
        compiler_params=_params(("parallel", "parallel")),
        name="mla_prompt",
    )(mq, kmla, ckvt.reshape(b, t // tk, LANES, tk), wuv2)


def _softmax_pv(s, v):
    m = jnp.max(s, axis=-1, keepdims=True)
    p = jnp.exp(s - m)
    l = jnp.sum(p, axis=-1, keepdims=True)
    return _dot(p.astype(BF16), v) / l


def _mem_prompt_kernel(q_ref, k_ref, v_ref, o_ref):
    for h in range(MEM_HEADS):
        sl = slice(h * LANES, (h + 1) * LANES)
        s = _dot_nt(q_ref[:, sl], k_ref[:, sl])
        o_ref[:, sl] = _softmax_pv(s, v_ref[:, sl]).astype(BF16)


def _mem_prompt(memq, mk16, mv16):
    b, t, _ = memq.shape
    tq = min(2 * ATTN_TILE, t)
    return pl.pallas_call(
        _mem_prompt_kernel,
        grid=(b, t // tq),
        in_specs=[pl.BlockSpec((None, tq, 512), lambda bi, i: (bi, i, 0)),
                  pl.BlockSpec((None, N_MEM, 512), lambda bi, i: (bi, 0, 0)),
                  pl.BlockSpec((None, N_MEM, 512), lambda bi, i: (bi, 0, 0))],
        out_specs=pl.BlockSpec((None, tq, 512), lambda bi, i: (bi, i, 0)),
        out_shape=jax.ShapeDtypeStruct((b, t, 512), BF16),
        compiler_params=_params(("parallel", "parallel")),
        name="mem_prompt",
    )(memq, mk16, mv16)


def _mem_sample_kernel(q_ref, k_ref, v_ref, o_ref, *, bb):
    heads = [(bi, h) for bi in range(bb) for h in range(MEM_HEADS)]
    scores = []
    for bi, h in heads:
        k = k_ref[bi, pl.ds(h, N_MEM, stride=MEM_HEADS), :].astype(BF16)
        scores.append(_dot_nt(q_ref[bi, :, h * LANES:(h + 1) * LANES], k))
    for (bi, h), s in zip(heads, scores):
        v = v_ref[bi, pl.ds(h, N_MEM, stride=MEM_HEADS), :].astype(BF16)
        o_ref[bi, :, h * LANES:(h + 1) * LANES] = _softmax_pv(s, v).astype(BF16)


def _mem_sample(memq, ck, cv, b_off):
    b, t, _ = memq.shape
    bb = math.gcd(b, 4)
    off = b_off // bb
    kern = functools.partial(_mem_sample_kernel, bb=bb)
    rows = N_MEM * MEM_HEADS
    return pl.pallas_call(
        kern,
        grid=(b // bb,),
        in_specs=[pl.BlockSpec((bb, t, 512), lambda i: (i, 0, 0)),
                  pl.BlockSpec((bb, rows, LANES), lambda i: (i + off, 0, 0)),
                  pl.BlockSpec((bb, rows, LANES), lambda i: (i + off, 0, 0))],
        out_specs=pl.BlockSpec((bb, t, 512), lambda i: (i, 0, 0)),
        out_shape=jax.ShapeDtypeStruct((b, t, 512), BF16),
        compiler_params=_params(("parallel",)),
        name="mem_sample",
    )(memq, ck, cv)


def _page_copies(pt_ref, grp, j, slot, bb, ch, nc, streams):
    cps = []
    for bi in range(bb):
        base = ((grp * bb + bi) * nc + j) * ch
        for r in range(ch):
            pg = pt_ref[base + r]
            for src, buf, sem in streams:
                cps.append(pltpu.make_async_copy(src.at[pg], buf.at[slot, bi * ch + r], sem.at[slot]))
    return cps


def _stream_step(pt_ref, bb, ch, nc, streams):
    grp, j = pl.program_id(0), pl.program_id(1)
    s = grp * nc + j
    total = pl.num_programs(0) * nc
    slot = s & 1

    @pl.when(s == 0)
    def _():
        for cp in _page_copies(pt_ref, 0, 0, 0, bb, ch, nc, streams):
            cp.start()

    @pl.when(s + 1 < total)
    def _():
        last = j == nc - 1
        for cp in _page_copies(pt_ref, jnp.where(last, grp + 1, grp), jnp.where(last, 0, j + 1), 1 - slot,
                               bb, ch, nc, streams):
            cp.start()

    for cp in _page_copies(pt_ref, grp, j, slot, bb, ch, nc, streams):
        cp.wait()
    return slot


def _causal_new(s2, nq):
    qpos = lax.broadcasted_iota(jnp.int32, s2.shape, 0) & (nq - 1)
    kpos = lax.broadcasted_iota(jnp.int32, s2.shape, 1)
    return jnp.where(kpos <= qpos, s2, NEG_INF)


def _decode_kernel(pt_ref, lam_ref, qd_ref, kn_ref, vn_ref, gs_ref, mq_ref, kmn_ref, cn_ref, gk_ref, aq_ref,
                   wukt_ref, wuv_ref, kc_hbm, vc_hbm, cc_hbm, rc_hbm, od_ref, om_ref,
                   kbuf, vbuf, cbuf, rbuf, sem, md_ref, ld_ref, accd_ref, lw_ref, qr_ref, mm_ref, lm_ref, accm_ref,
                   *, bb, ch, nc):
    j = pl.program_id(1)
    streams = [(kc_hbm, kbuf, sem.at[0]), (vc_hbm, vbuf, sem.at[1]), (cc_hbm, cbuf, sem.at[2]), (rc_hbm, rbuf, sem.at[3])]
    slot = _stream_step(pt_ref, bb, ch, nc, streams)
    nq = mq_ref.shape[1]
    nr = MLA_HEADS * nq
    nk = MLA_HEADS * MLA_NOPE

    @pl.when(j == 0)
    def _():
        for ref in (md_ref, mm_ref):
            ref[...] = jnp.full(ref.shape, NEG_INF, F32)
        for ref in (ld_ref, lm_ref, accd_ref, accm_ref):
            ref[...] = jnp.zeros(ref.shape, F32)
        for bi in range(bb):
            lw_ref[bi, 0:nk, :] = wukt_ref[...]
            qg = (mq_ref[bi].astype(F32) * gk_ref[...]).astype(BF16)
            for h in range(MLA_HEADS):
                qa = _dot(qg[:, h * LANES:(h + 1) * LANES], aq_ref[h])
                lw_ref[bi, nk + h * nq:nk + (h + 1) * nq, :] = qa[:, 0:LANES].astype(BF16)
                qr_ref[bi, h * nq:(h + 1) * nq, :] = qa[:, LANES + 64:LANES + 96].astype(BF16)

    scores_d, scores_m, lat_in = [], [], []
    for bi in range(bb):
        pages = range(bi * ch, (bi + 1) * ch)
        kt = jnp.concatenate([kbuf[slot, r].astype(BF16) for r in pages], axis=-1)
        scores_d.append(_dot(qd_ref[bi], kt))
        c = jnp.concatenate([cbuf[slot, r] for r in pages], axis=0).astype(BF16)
        krt = jnp.concatenate([rbuf[slot, r] for r in pages], axis=-1)
        big = _dot_nt(lw_ref[bi], c)
        nkeys = big.shape[1]
        knt = big[0:nk]
        ssq = jnp.sum((knt * knt).reshape(MLA_HEADS, MLA_NOPE, nkeys), axis=1)
        ssq = ssq + jnp.sum(krt * krt, axis=0, keepdims=True)
        rn = lax.rsqrt(ssq * (1.0 / MLA_QK) + NORM_EPS)
        sm = big[nk:nk + nr] + _dot(qr_ref[bi], krt.astype(BF16))
        scores_m.append((sm.reshape(MLA_HEADS, nq, nkeys) * rn[:, None, :]).reshape(nr, nkeys))
        lat_in.append(c)

    for bi in range(bb):
        pages = range(bi * ch, (bi + 1) * ch)
        p, alpha = _online_update(scores_d[bi], md_ref.at[bi], ld_ref.at[bi])
        p = p.astype(BF16)
        pv = []
        for n in range(2):
            v = jnp.concatenate([vbuf[slot, r, pl.ds(n, PAGE, stride=2), :].astype(BF16) for r in pages], axis=0)
            pv.append(_dot(p[n * 32:(n + 1) * 32], v))
        accd_ref[bi] = alpha * accd_ref[bi] + jnp.concatenate(pv, axis=0)
        pm, alpham = _online_update(scores_m[bi], mm_ref.at[bi], lm_ref.at[bi])
        accm_ref[bi] = alpham * accm_ref[bi] + _dot(pm.astype(BF16), lat_in[bi])

    @pl.when(j == nc - 1)
    def _():
        for bi in range(bb):
            q = qd_ref[bi]
            vn = vn_ref[bi]
            p2, alpha2 = _online_update(_causal_new(_dot_nt(q, kn_ref[bi]), nq), md_ref.at[bi], ld_ref.at[bi])
            p2 = p2.astype(BF16)
            pv2 = [_dot(p2[n * 32:(n + 1) * 32], vn[:, n * LANES:(n + 1) * LANES]) for n in range(2)]
            o = (alpha2 * accd_ref[bi] + jnp.concatenate(pv2, axis=0)) / ld_ref[bi]
            for n in range(2):
                on = o[n * 32:(n + 1) * 32]
                od = on[0:16] - lam_ref[0] * on[16:32]
                od = _rms_rows(od, SUBLN_EPS) * gs_ref[...]
                for g in range(2):
                    hh = n * 2 + g
                    od_ref[bi, :, hh * LANES:(hh + 1) * LANES] = od[g * nq:(g + 1) * nq].astype(BF16)

            qf = mq_ref[bi]
            kn = kmn_ref[bi]
            s2 = jnp.concatenate([_dot_nt(qf[:, h * LANES:(h + 1) * LANES], kn[:, h * LANES:(h + 1) * LANES])
                                  for h in range(MLA_HEADS)], axis=0)
            p3, alpha3 = _online_update(_causal_new(s2, nq), mm_ref.at[bi], lm_ref.at[bi])
            lat = ((alpha3 * accm_ref[bi] + _dot(p3.astype(BF16), cn_ref[bi])) / lm_ref[bi]).astype(BF16)
            om = _dot(lat[0:nq], wuv_ref[0])
            for h in range(1, MLA_HEADS):
                om = om + _dot(lat[h * nq:(h + 1) * nq], wuv_ref[h])
            om_ref[bi] = om.astype(BF16)


def _decode(pt_flat, lam, qbd, kn16, vn16, gs, mq, kmla, ckv16, gk_pad, aq, wukt, wuv2, kc, vc, cc, rc, n_pages):
    b, nq, _ = mq.shape
    assert nq == 8, "score-row layouts assume 8 new tokens per sequence"
    ch = math.gcd(PAGES_PER_STEP, n_pages)
    bb = math.gcd(SEQS_PER_STEP, b)
    nc = n_pages // ch
    kern = functools.partial(_decode_kernel, bb=bb, ch=ch, nc=nc)
    nr = MLA_HEADS * nq
    per_seq = lambda w, r=nq: pl.BlockSpec((bb, r, w), lambda g, j, pt: (g, 0, 0))
    const = lambda a: pl.BlockSpec(a.shape, lambda g, j, pt: (0,) * a.ndim)
    grid_spec = pltpu.PrefetchScalarGridSpec(
        num_scalar_prefetch=1,
        grid=(b // bb, nc),
        in_specs=[pl.BlockSpec(memory_space=pltpu.SMEM), per_seq(256, 64), per_seq(256), per_seq(256), const(gs),
                  per_seq(1024), per_seq(1024), per_seq(LANES), const(gk_pad), const(aq), const(wukt), const(wuv2)]
                 + [pl.BlockSpec(memory_space=pl.ANY)] * 4,
        out_specs=[per_seq(512), per_seq(512)],
        scratch_shapes=[pltpu.VMEM((2, bb * ch, 256, LANES), F32), pltpu.VMEM((2, bb * ch, 256, LANES), F32),
                        pltpu.VMEM((2, bb * ch, PAGE, LANES), F32), pltpu.VMEM((2, bb * ch, MLA_ROPE, PAGE), F32),
                        pltpu.SemaphoreType.DMA((4, 2)),
                        pltpu.VMEM((bb, 64, 1), F32), pltpu.VMEM((bb, 64, 1), F32), pltpu.VMEM((bb, 64, LANES), F32),
                        pltpu.VMEM((bb, MLA_HEADS * MLA_NOPE + nr, LANES), BF16), pltpu.VMEM((bb, nr, MLA_ROPE), BF16),
                        pltpu.VMEM((bb, nr, 1), F32), pltpu.VMEM((bb, nr, 1), F32), pltpu.VMEM((bb, nr, LANES), F32)],
    )
    return pl.pallas_call(
        kern,
        grid_spec=grid_spec,
        out_shape=[jax.ShapeDtypeStruct((b, nq, 512), BF16)] * 2,
        compiler_params=_params(("arbitrary", "arbitrary")),
        name="decode",
    )(pt_flat, lam, qbd, kn16, vn16, gs, mq, kmla, ckv16, gk_pad, aq, wukt, wuv2, kc, vc, cc, rc)


def _merge_kernel(xp_ref, odp_ref, omp_ref, ocp_ref, xs_ref, ods_ref, oms_ref, ocs_ref,
                  gmix_ref, wg_ref, bg_ref, wbr_ref, wout_ref, gffn_ref,
                  wrh_ref, wrl_ref, br_ref, xm_ref, h2_ref, tv_ref, ti_ref, *, p_tiles):
    is_p = pl.program_id(0) < p_tiles
    pick = lambda a_ref, b_ref: jnp.where(is_p, a_ref[...], b_ref[...])
    x = pick(xp_ref, xs_ref)
    h = (_rms_rows(x) * gmix_ref[...]).astype(BF16)
    gates = jax.nn.sigmoid(_dot(h, wg_ref[...]) + bg_ref[...])
    merged = gates[:, 0:D_MODEL] * _dot(pick(odp_ref, ods_ref), wbr_ref[0])
    merged = merged + gates[:, D_MODEL:2 * D_MODEL] * _dot(pick(omp_ref, oms_ref), wbr_ref[1])
    merged = merged + gates[:, 2 * D_MODEL:3 * D_MODEL] * _dot(pick(ocp_ref, ocs_ref), wbr_ref[2])
    xm = x + _dot(merged.astype(BF16), wout_ref[...])
    xm_ref[...] = xm
    h2 = _rms_rows(xm) * gffn_ref[...]
    for s in range(D_MODEL // LANES):
        h2_ref[:, s, :] = h2[:, s * LANES:(s + 1) * LANES]
    hh = h2.astype(BF16)
    hl = (h2 - hh.astype(F32)).astype(BF16)
    logits = _dot(hh, wrh_ref[...]) + _dot(hl, wrh_ref[...]) + _dot(hh, wrl_ref[...]) + br_ref[...]
    lane = lax.broadcasted_iota(jnp.int32, logits.shape, 1)
    logits = jnp.where(lane < N_EXPERTS, logits, -jnp.inf)
    tv = jnp.zeros(logits.shape, F32)
    ti = jnp.zeros(logits.shape, jnp.int32)
    vals = []
    for k in range(TOP_K):
        mx = jnp.max(logits, axis=-1, keepdims=True)
        idx = jnp.min(jnp.where(logits == mx, lane, LANES), axis=-1, keepdims=True)
        vals.append(mx)
        ti = jnp.where(lane == k, idx, ti)
        logits = jnp.where(lane == idx, -jnp.inf, logits)
    es = [jnp.exp(v - vals[0]) for v in vals]
    den = es[0] + es[1] + es[2] + es[3]
    for k in range(TOP_K):
        tv = jnp.where(lane == k, es[k] / den, tv)
    tv_ref[...] = tv
    ti_ref[...] = ti


def _merge(group_p, group_s, wts):
    n_p, n_s = group_p[0].shape[0], group_s[0].shape[0]
    tm = math.gcd(TOKEN_TILE, math.gcd(n_p, n_s))
    p_tiles, n = n_p // tm, n_p + n_s
    row = lambda w: pl.BlockSpec((tm, w), lambda i: (i, 0))
    row_p = lambda w: pl.BlockSpec((tm, w), lambda i: (jnp.minimum(i, p_tiles - 1), 0))
    row_s = lambda w: pl.BlockSpec((tm, w), lambda i: (jnp.maximum(i - p_tiles, 0), 0))
    widths = (D_MODEL, 512, 512, 512)
    return pl.pallas_call(
        functools.partial(_merge_kernel, p_tiles=p_tiles),
        grid=(n // tm,),
        in_specs=[row_p(w) for w in widths] + [row_s(w) for w in widths] + [_full(a) for a in wts],
        out_specs=[row(D_MODEL), pl.BlockSpec((tm, 8, LANES), lambda i: (i, 0, 0)), row(LANES), row(LANES)],
        out_shape=[jax.ShapeDtypeStruct((n, D_MODEL), F32), jax.ShapeDtypeStruct((n, 8, LANES), F32),
                   jax.ShapeDtypeStruct((n, LANES), F32), jax.ShapeDtypeStruct((n, LANES), jnp.int32)],
        compiler_params=_params(("parallel",)),
        name="merge",
    )(*group_p, *group_s, *wts)


def _expert_kernel(be_ref, first_ref, nval_ref, slot_ref, wgu_ref, bgu_ref, wd_ref, bd_ref, x_hbm, y_hbm,
                   slot_smem, xbuf, ybuf, zbuf, wgu16, wd16, gsem, ssem, isem, *, n_blocks, n_real):
    i = pl.program_id(0)
    nv = nval_ref[i]
    groups = lambda k: lax.shift_right_logical(nval_ref[k] + (ROW_GROUP - 1), ROW_GROUP.bit_length() - 1)

    def idx_copy(k):
        ring = lax.rem(k, 3)
        return pltpu.make_async_copy(slot_ref.at[k], slot_smem.at[ring], isem.at[ring])

    halves = slot_smem.shape[1] // 2
    bm = halves * LANES

    def for_groups(k, fn):
        used = groups(k)
        for g in range(bm // ROW_GROUP):
            pl.when(g < used)(functools.partial(fn, g * ROW_GROUP))

    def issue_gather(k):
        ring, xb = lax.rem(k, 3), k & 1

        def rows(r0):
            for r in range(r0, r0 + ROW_GROUP):
                tok = slot_smem[ring, halves + r // LANES, r % LANES]
                pltpu.make_async_copy(x_hbm.at[tok], xbuf.at[xb, r], gsem.at[xb]).start()

        for_groups(k, rows)

    def wait_gather(k):
        xb = k & 1
        for_groups(k, lambda r0: pltpu.make_async_copy(
            x_hbm.at[pl.ds(0, ROW_GROUP)], xbuf.at[xb, pl.ds(r0, ROW_GROUP)], gsem.at[xb]).wait())

    def issue_scatter(k):
        ring = lax.rem(k, 3)

        def rows(r0):
            for r in range(r0, r0 + ROW_GROUP):
                slot = slot_smem[ring, r // LANES, r % LANES]
                pltpu.make_async_copy(ybuf.at[r], y_hbm.at[slot], ssem.at[0]).start()

        for_groups(k, rows)

    def wait_scatter(k):
        for_groups(k, lambda r0: pltpu.make_async_copy(
            ybuf.at[pl.ds(r0, ROW_GROUP)], y_hbm.at[pl.ds(0, ROW_GROUP)], ssem.at[0]).wait())

    @pl.when(i == 0)
    def _():
        xbuf[...] = jnp.zeros(xbuf.shape, F32)
        zbuf[...] = jnp.zeros(zbuf.shape, F32)
        spare = pltpu.make_async_copy(zbuf, y_hbm.at[pl.ds(n_real, ROW_GROUP)], ssem.at[0])
        spare.start()
        spare.wait()
        first_idx = idx_copy(0)
        first_idx.start()
        first_idx.wait()
        issue_gather(0)
        if n_blocks > 1:
            idx_copy(1).start()

    @pl.when(i + 1 < n_blocks)
    def _():
        idx_copy(i + 1).wait()
        issue_gather(i + 1)

    @pl.when(i + 2 < n_blocks)
    def _():
        idx_copy(i + 2).start()

    @pl.when(nv > 0)
    def _():
        @pl.when(first_ref[i] == 1)
        def _():
            wgu16[...] = wgu_ref[...].astype(BF16)
            wd16[...] = wd_ref[...].astype(BF16)

        wait_gather(i)
        xb = i & 1
        x = jnp.concatenate([xbuf[xb, :, s, :] for s in range(D_MODEL // LANES)], axis=-1).astype(BF16)
        n_chunks = D_EXPERT // MXU_DIM

        def gate_up(c):
            g0, u0 = c * MXU_DIM, D_EXPERT + c * MXU_DIM
            return (_dot(x, wgu16[:, g0:g0 + MXU_DIM]) + bgu_ref[:, g0:g0 + MXU_DIM],
                    _dot(x, wgu16[:, u0:u0 + MXU_DIM]) + bgu_ref[:, u0:u0 + MXU_DIM])

        y = jnp.broadcast_to(bd_ref[...], (bm, D_MODEL))
        nxt = gate_up(0)
        for c in range(n_chunks):
            gate, up = nxt
            if c + 1 < n_chunks:
                nxt = gate_up(c + 1)
            gate = jnp.minimum(gate, SWIGLU_LIMIT)
            up = jnp.clip(up, -SWIGLU_LIMIT, SWIGLU_LIMIT)
            act = (up + 1.0) * gate * jax.nn.sigmoid(SWIGLU_ALPHA * gate)
            y = y + _dot(act.astype(BF16), wd16[c * MXU_DIM:(c + 1) * MXU_DIM, :])

        @pl.when(i > 0)
        def _():
            wait_scatter(i - 1)

        for s in range(D_MODEL // LANES):
            ybuf[:, s, :] = y[:, s * LANES:(s + 1) * LANES]
        issue_scatter(i)

    @pl.when((nv == 0) & (i > 0))
    def _():
        wait_scatter(i - 1)

    @pl.when(i == n_blocks - 1)
    def _():
        wait_scatter(i)


def _experts(block_e, first, nval, row_slot3, wgu, bgu, wd, bd, x3, n_real):
    n_blocks = row_slot3.shape[0]
    bm = row_slot3.shape[1] // 2 * LANES
    n_slots = n_real + ROW_GROUP
    kern = functools.partial(_expert_kernel, n_blocks=n_blocks, n_real=n_real)
    grid_spec = pltpu.PrefetchScalarGridSpec(
        num_scalar_prefetch=3,
        grid=(n_blocks,),
        in_specs=[pl.BlockSpec(row_slot3.shape, lambda i, be, fi, na: (0, 0, 0)),
                  pl.BlockSpec((None, D_MODEL, 2 * D_EXPERT), lambda i, be, fi, na: (be[i], 0, 0)),
                  pl.BlockSpec((None, 1, 2 * D_EXPERT), lambda i, be, fi, na: (be[i], 0, 0)),
                  pl.BlockSpec((None, D_EXPERT, D_MODEL), lambda i, be, fi, na: (be[i], 0, 0)),
                  pl.BlockSpec((None, 1, D_MODEL), lambda i, be, fi, na: (be[i], 0, 0)),
                  pl.BlockSpec(memory_space=pl.ANY)],
        out_specs=pl.BlockSpec(memory_space=pl.ANY),
        scratch_shapes=[pltpu.SMEM((3,) + row_slot3.shape[1:], jnp.int32),
                        pltpu.VMEM((2, bm, 8, LANES), F32), pltpu.VMEM((bm, 8, LANES), F32),
                        pltpu.VMEM((ROW_GROUP, 8, LANES), F32),
                        pltpu.VMEM((D_MODEL, 2 * D_EXPERT), BF16), pltpu.VMEM((D_EXPERT, D_MODEL), BF16),
                        pltpu.SemaphoreType.DMA((2,)), pltpu.SemaphoreType.DMA((1,)), pltpu.SemaphoreType.DMA((3,))],
    )
    return pl.pallas_call(
        kern,
        grid_spec=grid_spec,
        out_shape=jax.ShapeDtypeStruct((n_slots, 8, LANES), F32),
        compiler_params=_params(("arbitrary",)),
        name="experts",
    )(block_e, first, nval, row_slot3, wgu, bgu, wd, bd, x3)


def _combine_kernel(xm_ref, tv_ref, y_ref, o_ref):
    tv = tv_ref[...]
    gate = [jnp.broadcast_to(tv[:, k:k + 1], (tv.shape[0], LANES)) for k in range(TOP_K)]
    for s in range(D_MODEL // LANES):
        acc = xm_ref[:, s * LANES:(s + 1) * LANES]
        for k in range(TOP_K):
            acc = acc + gate[k] * y_ref[:, k * 8 + s, :]
        o_ref[:, s * LANES:(s + 1) * LANES] = acc


def _combine(xm, tv, yslots, tok_off, n):
    tm = math.gcd(TOKEN_TILE, math.gcd(n, tok_off)) if tok_off else min(TOKEN_TILE, n)
    off = tok_off // tm
    y4 = yslots.reshape(yslots.shape[0] // TOP_K, TOP_K * 8, LANES)
    return pl.pallas_call(
        _combine_kernel,
        grid=(n // tm,),
        in_specs=[pl.BlockSpec((tm, D_MODEL), lambda i: (i + off, 0)),
                  pl.BlockSpec((tm, LANES), lambda i: (i + off, 0)),
                  pl.BlockSpec((tm, TOP_K * 8, LANES), lambda i: (i + off, 0, 0))],
        out_specs=pl.BlockSpec((tm, D_MODEL), lambda i: (i, 0)),
        out_shape=jax.ShapeDtypeStruct((n, D_MODEL), F32),
        compiler_params=_params(("parallel",)),
        name="combine",
    )(xm, tv, y4)


def _rope_tables(pos, rows):
    pos = pos.astype(F32)[:, None]
    lane = jnp.arange(LANES)

    def ang(half):
        inv = jnp.power(ROPE_THETA, -jnp.arange(half, dtype=F32) / half)
        return pos * inv[None, :]

    a64 = ang(32)[:, lane % 32]
    first = (lane % 64) < 32
    c64 = jnp.cos(a64)
    sa64 = jnp.where(first, -jnp.sin(a64), 0.0)
    sb64 = jnp.where(first, 0.0, jnp.sin(a64))
    a32 = ang(16)[:, lane % 16]
    in_a = (lane >= 64) & (lane < 80)
    in_b = (lane >= 80) & (lane < 96)
    c32 = jnp.where(in_a | in_b, jnp.cos(a32), 1.0)
    sa32 = jnp.where(in_a, -jnp.sin(a32), 0.0)
    sb32 = jnp.where(in_b, jnp.sin(a32), 0.0)
    tabs = [c64, sa64, sb64, c32, sa32, sb32]
    reps = rows // pos.shape[0]
    return [jnp.tile(t, (reps, 1)) if reps > 1 else t for t in tabs]


def _tied_pad(g, scale):
    blk = jnp.concatenate([g[:MLA_NOPE], g[MLA_NOPE:], g[MLA_NOPE:], jnp.zeros((32,), F32)]) * scale
    return jnp.tile(blk, MLA_HEADS)[None, :]


def _layer_weights(lp):
    w_in = lp["w_in"]
    kr_blk = jnp.zeros((D_MODEL, LANES), F32).at[:, 64:96].set(w_in[:, _OFF_KR:_OFF_MQ])
    wa = jnp.concatenate([w_in[:, _OFF_Q:_OFF_KR], kr_blk, w_in[:, _OFF_MQ:_OFF_G]], axis=1).astype(BF16)
    wuq = jnp.pad(lp["w_mla_uq"], ((0, 0), (0, 0), (0, LANES - MLA_QK))).reshape(MLA_Q_LORA, MLA_HEADS * LANES).astype(BF16)
    wuk_pad = jnp.pad(lp["w_mla_uk"], ((0, 0), (0, 0), (0, LANES - MLA_NOPE))).reshape(MLA_KV_LORA, MLA_HEADS * LANES)
    lane = jnp.arange(LANES)
    rope_eye = jnp.where(((lane >= 64) & (lane < 96))[:, None], jnp.eye(LANES, dtype=F32), 0.0)
    wk = jnp.concatenate([wuk_pad, jnp.tile(rope_eye, (1, MLA_HEADS))], axis=0).astype(BF16)
    g64 = jnp.kron(jnp.eye(4, dtype=F32), jnp.ones((64, 64), F32)).astype(BF16)
    g128 = jnp.kron(jnp.eye(2, dtype=F32), jnp.ones((128, 128), F32)).astype(BF16)
    gk_pad = _tied_pad(lp["mla_k_norm"], 1.0)
    inproj_w = [lp["norm_mix"][None, :], wa, wuq, wk, g64, g128,
                jnp.tile(lp["diff_q_norm"], 8)[None, :] * DIFF_SCALE, jnp.tile(lp["diff_k_norm"], 4)[None, :],
                lp["mla_q_a_norm"][None, :], lp["mla_kv_a_norm"][None, :],
                _tied_pad(lp["mla_q_norm"], MLA_SCALE), gk_pad,
                jnp.tile(lp["mem_q_norm"], MEM_HEADS)[None, :] * MEM_SCALE]
    wuv = lp["w_mla_uv"]
    wuv2 = jnp.einsum("rhd,hg->hrgd", wuv, jnp.eye(MLA_HEADS, dtype=F32)).reshape(MLA_HEADS, MLA_KV_LORA, 512).astype(BF16)
    wuk_t = jnp.transpose(lp["w_mla_uk"], (1, 2, 0))
    aq = jnp.zeros((MLA_HEADS, LANES, 2 * LANES), F32).at[:, 0:MLA_NOPE, 0:LANES].set(wuk_t)
    aq = aq.at[:, :, LANES:].add(rope_eye[None]).astype(BF16)
    wukt = wuk_t.reshape(MLA_HEADS * MLA_NOPE, MLA_KV_LORA).astype(BF16)
    wr = jnp.pad(lp["w_router"], ((0, 0), (0, LANES - N_EXPERTS)))
    wrh = wr.astype(BF16)
    wrl = (wr - wrh.astype(F32)).astype(BF16)
    merge_w = [lp["norm_mix"][None, :], w_in[:, _OFF_G:].astype(BF16), lp["b_gate"][None, :],
               lp["w_branch"].reshape(3, 512, D_MODEL).astype(BF16), lp["w_out"].astype(BF16),
               lp["norm_ffn"][None, :], wrh, wrl, jnp.pad(lp["b_router"], (0, LANES - N_EXPERTS))[None, :]]
    return dict(inproj=inproj_w, g128=g128, wuv2=wuv2, aq=aq, wukt=wukt, gk_pad=gk_pad, merge=merge_w)


def _moe_plan(ti, bm):
    n = ti.shape[0]
    a = n * TOP_K
    flat_e = ti[:, :TOP_K].reshape(a)
    onehot = (flat_e[:, None] == jnp.arange(N_EXPERTS, dtype=jnp.int32)[None, :]).astype(jnp.int32)
    csum = jnp.cumsum(onehot, axis=0)
    counts = csum[-1]
    rank = jnp.sum(onehot * csum, axis=1) - 1
    padded = (counts + bm - 1) // bm * bm
    pend = jnp.cumsum(padded)
    pstart = pend - padded
    dest = jnp.sum(onehot * pstart[None, :], axis=1) + rank
    n_rows = (a + N_EXPERTS * (bm - 1) + bm - 1) // bm * bm
    n_blocks = n_rows // bm
    spare = a + (jnp.arange(n_rows, dtype=jnp.int32) % ROW_GROUP)
    row_slot = spare.at[dest].set(jnp.arange(a, dtype=jnp.int32))
    blk_start = jnp.arange(n_blocks, dtype=jnp.int32) * bm
    block_e = jnp.minimum(jnp.sum((blk_start[:, None] >= pend[None, :]).astype(jnp.int32), axis=1), N_EXPERTS - 1)
    first = jnp.concatenate([jnp.ones((1,), jnp.int32), (block_e[1:] != block_e[:-1]).astype(jnp.int32)])
    eh = (block_e[:, None] == jnp.arange(N_EXPERTS, dtype=jnp.int32)[None, :]).astype(jnp.int32)
    valid_end = jnp.sum(eh * (pstart + counts)[None, :], axis=1)
    nval = jnp.where(blk_start < pend[-1], jnp.clip(valid_end - blk_start, 0, bm), 0).astype(jnp.int32)
    row_tok = jnp.minimum(lax.shift_right_logical(row_slot, 2), n - 1)
    row_idx = jnp.concatenate([row_slot.reshape(n_blocks, bm // LANES, LANES),
                               row_tok.reshape(n_blocks, bm // LANES, LANES)], axis=1)
    return block_e, first, nval, row_idx, a


def _qbd(dq_s, b, nq):
    q = dq_s.reshape(b, nq, 2, 2, 2, DIFF_HEAD_DIM)
    q = jnp.transpose(q, (0, 2, 4, 3, 1, 5))
    eye = jnp.eye(4, dtype=q.dtype).reshape(2, 2, 2, 2)
    out = jnp.einsum("bncgqd,ncmk->bncgqmkd", q, eye)
    return out.reshape(b, 64, 256)


def kernel(x_prompt, x_sample, mem_prompt, cache_diff_k, cache_diff_v, cache_mla_ckv, cache_mla_krope, cache_mem_k, cache_mem_v, page_table, norm_mix, norm_mem, w_in, b_gate, diff_q_norm, diff_k_norm, diff_lambda, diff_subln, mla_q_a_norm, w_mla_uq, mla_kv_a_norm, w_mla_uk, w_mla_uv, mla_q_norm, mla_k_norm, w_mem_kv, mem_q_norm, mem_k_norm, w_branch, w_out, norm_ffn, w_router, b_router, w_gate_up, b_gate_up, w_down, b_down):
    depth = w_in.shape[0]
    bp, t, _ = x_prompt.shape
    bs, nq, _ = x_sample.shape
    n_pool, n_pages = cache_diff_k.shape[1], page_table.shape[1]
    past_len = n_pages * PAGE
    n_p, n_s = bp * t, bs * nq

    tm_p, tm_s = min(TOKEN_TILE, n_p), min(TOKEN_TILE, n_s)
    tabs_p = _rope_tables(jnp.arange(t, dtype=jnp.int32), t)
    tabs_s = _rope_tables(past_len + jnp.arange(nq, dtype=jnp.int32), tm_s)
    del tm_p

    kc = jnp.transpose(cache_diff_k, (0, 1, 3, 4, 5, 2)).reshape(depth * n_pool, 256, PAGE)
    vc = cache_diff_v.reshape(depth * n_pool, 2 * PAGE, DIFF_V_DIM)
    cc = cache_mla_ckv.reshape(depth * n_pool, PAGE, MLA_KV_LORA)
    rc = jnp.transpose(cache_mla_krope, (0, 1, 3, 2)).reshape(depth * n_pool, MLA_ROPE, PAGE)
    mkc = cache_mem_k.reshape(depth * bs, N_MEM * MEM_HEADS, MEM_HEAD_DIM)
    mvc = cache_mem_v.reshape(depth * bs, N_MEM * MEM_HEADS, MEM_HEAD_DIM)

    xp = x_prompt.reshape(n_p, D_MODEL)
    xs = x_sample.reshape(n_s, D_MODEL)
    outs = [[] for _ in range(10)]
    for layer in range(depth):
        lp = dict(norm_mix=norm_mix[layer], w_in=w_in[layer], b_gate=b_gate[layer], diff_q_norm=diff_q_norm[layer],
                  diff_k_norm=diff_k_norm[layer], mla_q_a_norm=mla_q_a_norm[layer], w_mla_uq=w_mla_uq[layer],
                  mla_kv_a_norm=mla_kv_a_norm[layer], w_mla_uk=w_mla_uk[layer], w_mla_uv=w_mla_uv[layer],
                  mla_q_norm=mla_q_norm[layer], mla_k_norm=mla_k_norm[layer], mem_q_norm=mem_q_norm[layer],
                  w_branch=w_branch[layer], w_out=w_out[layer], norm_ffn=norm_ffn[layer],
                  w_router=w_router[layer], b_router=b_router[layer])
        w = _layer_weights(lp)
        lam_init = 0.8 - 0.6 * math.exp(-0.3 * layer)
        lamp = diff_lambda[layer].astype(F32)
        lam = (jnp.exp(jnp.sum(lamp[0] * lamp[1])) - jnp.exp(jnp.sum(lamp[2] * lamp[3])) + lam_init).reshape(1)
        gs = (diff_subln[layer] * (1.0 - lam_init))[None, :]
        pt_flat = (page_table + layer * n_pool).reshape(-1).astype(jnp.int32)

        (dq, dk32, dk16, dv32, dv16, mq, ckv32, ckv16, kr32, kmla, memq, dvt, ckvt) = _inproj(xp, tabs_p, w["inproj"])
        o_diff = _diff_prompt(lam, dq.reshape(bp, t, 512), dk16.reshape(bp, t, 256), dvt, gs)
        o_mla = _mla_prompt(mq.reshape(bp, t, 1024), kmla.reshape(bp, t, 1024), ckvt, w["wuv2"])
        mk32, mk16, mv32, mv16 = _memkv(mem_prompt.reshape(bp * N_MEM, D_MODEL), norm_mem[layer][None, :],
                                        w_mem_kv[layer].astype(BF16), w["g128"],
                                        jnp.tile(mem_k_norm[layer], MEM_HEADS)[None, :])
        o_mem = _mem_prompt(memq.reshape(bp, t, 512), mk16.reshape(bp, N_MEM, 512), mv16.reshape(bp, N_MEM, 512))
        group_p = (xp, o_diff.reshape(n_p, 512), o_mla.reshape(n_p, 512), o_mem.reshape(n_p, 512))
        for lst, val in zip(outs[:6], (dk32.reshape(bp, t, 2, 2, 64), dv32.reshape(bp, t, 2, 128),
                                       ckv32.reshape(bp, t, 128), kr32.reshape(bp, t, 32),
                                       mk32.reshape(bp, N_MEM, 4, 128), mv32.reshape(bp, N_MEM, 4, 128))):
            lst.append(val)

        (dq, dk32, dk16, dv32, dv16, mq, ckv32, ckv16, kr32, kmla, memq, _, _) = _inproj(xs, tabs_s, w["inproj"])
        o_diff, o_mla = _decode(pt_flat, lam, _qbd(dq, bs, nq), dk16.reshape(bs, nq, 256), dv16.reshape(bs, nq, 256), gs,
                                mq.reshape(bs, nq, 1024), kmla.reshape(bs, nq, 1024), ckv16.reshape(bs, nq, LANES),
                                w["gk_pad"], w["aq"], w["wukt"], w["wuv2"], kc, vc, cc, rc, n_pages)
        o_mem = _mem_sample(memq.reshape(bs, nq, 512), mkc, mvc, layer * bs)
        group_s = (xs, o_diff.reshape(n_s, 512), o_mla.reshape(n_s, 512), o_mem.reshape(n_s, 512))
        for lst, val in zip(outs[6:], (dk32.reshape(bs, nq, 2, 2, 64), dv32.reshape(bs, nq, 2, 128),
                                       ckv32.reshape(bs, nq, 128), kr32.reshape(bs, nq, 32))):
            lst.append(val)

        xm, h2, tv, ti = _merge(group_p, group_s, w["merge"])
        block_e, first, nval, row_slot3, n_slots = _moe_plan(ti, MOE_ROWS)
        yslots = _experts(block_e + layer * N_EXPERTS, first, nval, row_slot3,
                          w_gate_up.reshape(depth * N_EXPERTS, D_MODEL, 2 * D_EXPERT),
                          b_gate_up.reshape(depth * N_EXPERTS, 1, 2 * D_EXPERT),
                          w_down.reshape(depth * N_EXPERTS, D_EXPERT, D_MODEL),
                          b_down.reshape(depth * N_EXPERTS, 1, D_MODEL), h2, n_slots)
        xp = _combine(xm, tv, yslots, 0, n_p)
        xs = _combine(xm, tv, yslots, n_p, n_s)

    stack = lambda lst: jnp.stack(lst)
    return (xp.reshape(bp, t, D_MODEL), xs.reshape(bs, nq, D_MODEL)) + tuple(stack(o) for o in outs)
```

```python
import functools
import math

import jax
import jax.numpy as jnp
from jax import lax
from jax.experimental import pallas as pl
from jax.experimental.pallas import tpu as pltpu

F32 = jnp.float32
BF16 = jnp.bfloat16

D_MODEL = 1024
DIFF_HEAD_DIM = 64
DIFF_V_DIM = 128
MLA_HEADS = 8
MLA_Q_LORA = 256
MLA_KV_LORA = 128
MLA_NOPE = 64
MLA_ROPE = 32
MLA_V = 64
MLA_QK = MLA_NOPE + MLA_ROPE
MEM_HEADS = 4
MEM_HEAD_DIM = 128
N_MEM = 256
N_EXPERTS = 32
TOP_K = 4
D_EXPERT = 1024
SWIGLU_ALPHA = 1.702
SWIGLU_LIMIT = 7.0
ROPE_THETA = 10000.0
NORM_EPS = 1e-6
SUBLN_EPS = 1e-5
NEG_INF = -1e30
PAGE = 128

DIFF_SCALE = DIFF_HEAD_DIM ** -0.5
MLA_SCALE = MLA_QK ** -0.5
MEM_SCALE = MEM_HEAD_DIM ** -0.5

LANES = 128
MXU_DIM = 256
VMEM_LIMIT = 52 * 1024 * 1024
TOKEN_TILE = 256
ATTN_TILE = 256
PROMPT_Q_TILE = 256
PAGES_PER_STEP = 8
SEQS_PER_STEP = 4
PAGE_BUFFERS = 3
MOE_ROWS = 256
ROW_GROUP = 32

_OFF_Q, _OFF_K, _OFF_V, _OFF_CQ, _OFF_CKV, _OFF_KR, _OFF_MQ, _OFF_G = 0, 512, 768, 1024, 1280, 1408, 1440, 1952


def _params(sem):
    return pltpu.CompilerParams(dimension_semantics=sem, vmem_limit_bytes=VMEM_LIMIT)


def _full(a):
    nd = a.ndim
    return pl.BlockSpec(a.shape, lambda *_: (0,) * nd)


def _dot(a, b):
    return jnp.dot(a, b, preferred_element_type=F32)


def _dot_nt(a, b):
    return lax.dot_general(a, b, (((1,), (1,)), ((), ())), preferred_element_type=F32)


def _rms_rows(x, eps=NORM_EPS):
    return x * lax.rsqrt(jnp.mean(x * x, axis=-1, keepdims=True) + eps)


def _group_sumsq(v, g_ref):
    sq = (v * v).astype(BF16)
    parts = [_dot(sq[:, j * MXU_DIM:(j + 1) * MXU_DIM], g_ref[...]) for j in range(v.shape[1] // MXU_DIM)]
    return parts[0] if len(parts) == 1 else jnp.concatenate(parts, axis=-1)


def _rope_lanes(v, c, sa, sb, half):
    outs = []
    for j in range(v.shape[1] // LANES):
        b = v[:, j * LANES:(j + 1) * LANES]
        outs.append(b * c + pltpu.roll(b, LANES - half, 1) * sa + pltpu.roll(b, half, 1) * sb)
    return outs[0] if len(outs) == 1 else jnp.concatenate(outs, axis=-1)


def _inproj_kernel(x_ref, gmix_ref, wa_ref, wuq_ref, wk_ref, g64_ref, g128_ref,
                   gq_ref, gk_ref, gcq_ref, gckv_ref, gmq_ref, gkm_ref, gmemq_ref,
                   c64_ref, sa64_ref, sb64_ref, c32_ref, sa32_ref, sb32_ref,
                   dq_ref, dk32_ref, dk16_ref, dv32_ref, dv16_ref, mq_ref,
                   ckv32_ref, ckv16_ref, kr32_ref, kmla_ref, memq_ref, dvt_ref, ckvt_ref, dkt_ref):
    x = x_ref[...]
    h = (_rms_rows(x) * gmix_ref[...]).astype(BF16)
    z = _dot(h, wa_ref[...])
    c64, sa64, sb64 = c64_ref[...], sa64_ref[...], sb64_ref[...]
    c32, sa32, sb32 = c32_ref[...], sa32_ref[...], sb32_ref[...]

    zq = z[:, 0:512]
    qn = zq * lax.rsqrt(_group_sumsq(zq, g64_ref) * (1.0 / DIFF_HEAD_DIM) + NORM_EPS) * gq_ref[...]
    dq_ref[...] = _rope_lanes(qn, c64, sa64, sb64, 32).astype(BF16)

    zk = z[:, 512:768]
    kn = zk * lax.rsqrt(_group_sumsq(zk, g64_ref) * (1.0 / DIFF_HEAD_DIM) + NORM_EPS) * gk_ref[...]
    dk = _rope_lanes(kn, c64, sa64, sb64, 32)
    dk32_ref[...] = dk
    dkt_ref[...] = dk.T
    dk16_ref[...] = dk.astype(BF16)

    dv = z[:, 768:1024]
    dv32_ref[...] = dv
    dv16_ref[...] = dv.astype(BF16)
    dvt_ref[...] = dv.T.astype(BF16)

    cq = (_rms_rows(z[:, 1024:1280]) * gcq_ref[...]).astype(BF16)
    mqr = _dot(cq, wuq_ref[...])
    mqn = mqr * lax.rsqrt(_group_sumsq(mqr, g128_ref) * (1.0 / MLA_QK) + NORM_EPS) * gmq_ref[...]
    mq_ref[...] = _rope_lanes(mqn, c32, sa32, sb32, 16).astype(BF16)

    ckv = _rms_rows(z[:, 1280:1408]) * gckv_ref[...]
    ckv32_ref[...] = ckv
    ckv16 = ckv.astype(BF16)
    ckv16_ref[...] = ckv16
    ckvt_ref[...] = ckv.T.astype(BF16)

    krb = _rope_lanes(z[:, 1408:1536], c32, sa32, sb32, 16)
    kr32_ref[...] = krb[:, 64:96]
    kin = jnp.concatenate([ckv16, krb.astype(BF16)], axis=-1)
    kraw = _dot(kin, wk_ref[...])
    kmla_ref[...] = (kraw * lax.rsqrt(_group_sumsq(kraw, g128_ref) * (1.0 / MLA_QK) + NORM_EPS)
                     * gkm_ref[...]).astype(BF16)

    zm = z[:, 1536:2048]
    memq_ref[...] = (zm * lax.rsqrt(_group_sumsq(zm, g128_ref) * (1.0 / MEM_HEAD_DIM) + NORM_EPS)
                     * gmemq_ref[...]).astype(BF16)


def _inproj(x2d, tabs, wts, seq_len):
    n = x2d.shape[0]
    tm = min(TOKEN_TILE, n)
    period = tabs[0].shape[0] // tm
    per_seq = seq_len // tm
    row = lambda w: pl.BlockSpec((tm, w), lambda i: (i, 0))
    tab = pl.BlockSpec((tm, LANES), lambda i: (i % period, 0))
    out_w = [(512, BF16), (256, F32), (256, BF16), (256, F32), (256, BF16), (1024, BF16),
             (128, F32), (128, BF16), (32, F32), (1024, BF16), (512, BF16)]
    tile_t = lambda d: pl.BlockSpec((None, d, tm), lambda i: (i, 0, 0))
    return pl.pallas_call(
        _inproj_kernel,
        grid=(n // tm,),
        in_specs=[row(D_MODEL)] + [_full(a) for a in wts] + [tab] * 6,
        out_specs=[row(w) for w, _ in out_w] + [tile_t(256), tile_t(LANES),
                                                pl.BlockSpec((None, 256, tm), lambda i: (i // per_seq, 0, i % per_seq))],
        out_shape=[jax.ShapeDtypeStruct((n, w), dt) for w, dt in out_w]
                  + [jax.ShapeDtypeStruct((n // tm, 256, tm), BF16), jax.ShapeDtypeStruct((n // tm, LANES, tm), BF16),
                     jax.ShapeDtypeStruct((n // seq_len, 256, seq_len), F32)],
        compiler_params=_params(("parallel",)),
        name="inproj",
    )(x2d, *wts, *tabs)


def _memkv_kernel(x_ref, g_ref, w_ref, g128_ref, gk_ref, k32_ref, k16_ref, v32_ref, v16_ref):
    h = (_rms_rows(x_ref[...]) * g_ref[...]).astype(BF16)
    kv = _dot(h, w_ref[...])
    k = kv[:, 0:512]
    k = k * lax.rsqrt(_group_sumsq(k, g128_ref) * (1.0 / MEM_HEAD_DIM) + NORM_EPS) * gk_ref[...]
    v = kv[:, 512:1024]
    k32_ref[...] = k
    k16_ref[...] = k.astype(BF16)
    v32_ref[...] = v
    v16_ref[...] = v.astype(BF16)


def _memkv(mem2d, g, w, g128, gk):
    n = mem2d.shape[0]
    tm = min(TOKEN_TILE, n)
    row = lambda w_: pl.BlockSpec((tm, w_), lambda i: (i, 0))
    return pl.pallas_call(
        _memkv_kernel,
        grid=(n // tm,),
        in_specs=[row(D_MODEL), _full(g), _full(w), _full(g128), _full(gk)],
        out_specs=[row(512)] * 4,
        out_shape=[jax.ShapeDtypeStruct((n, 512), dt) for dt in (F32, BF16, F32, BF16)],
        compiler_params=_params(("parallel",)),
        name="memkv",
    )(mem2d, g, w, g128, gk)


def _online_update(s, m_ref, l_ref, rows=None):
    sl = slice(None) if rows is None else rows
    m_old = m_ref[sl, :]
    m_new = jnp.maximum(m_old, jnp.max(s, axis=-1, keepdims=True))
    alpha = jnp.exp(m_old - m_new)
    p = jnp.exp(s - m_new)
    l_ref[sl, :] = alpha * l_ref[sl, :] + jnp.sum(p, axis=-1, keepdims=True)
    m_ref[sl, :] = m_new
    return p, alpha


def _cols_softmax_step(s, m_ref, l_ref, cols):
    m_old = m_ref[:, cols]
    m_new = jnp.maximum(m_old, jnp.max(s, axis=0, keepdims=True))
    alpha = jnp.exp(m_old - m_new)
    p = jnp.exp(s - m_new)
    l_ref[:, cols] = alpha * l_ref[:, cols] + jnp.sum(p, axis=0, keepdims=True)
    m_ref[:, cols] = m_new
    return p, alpha


def _causal_cols(s, tq, delta):
    row = lax.broadcasted_iota(jnp.int32, s.shape, 0)
    col = lax.broadcasted_iota(jnp.int32, s.shape, 1) & (tq - 1)
    return jnp.where(row <= col + delta, s, NEG_INF)


def _cols_attention(i, tq, tk, n_cols, score_fn, vt_ref, m_ref, l_ref, acc_ref):
    m_ref[...] = jnp.full(m_ref.shape, NEG_INF, F32)
    l_ref[...] = jnp.zeros(l_ref.shape, F32)
    acc_ref[...] = jnp.zeros(acc_ref.shape, F32)
    j_last = (i * tq) // tk
    delta = i * tq - j_last * tk

    n_chunks = n_cols // MXU_DIM

    def scores(j):
        return tuple(score_fn(j, cc) for cc in range(n_chunks))

    def finish(j, s_all, masked):
        vt = vt_ref[j]
        for cc in range(n_chunks):
            cols = slice(cc * MXU_DIM, (cc + 1) * MXU_DIM)
            s = _causal_cols(s_all[cc], tq, delta) if masked else s_all[cc]
            p, alpha = _cols_softmax_step(s, m_ref, l_ref, cols)
            acc_ref[:, cols] = alpha * acc_ref[:, cols] + _dot(vt, p.astype(BF16))

    def body(j, s_cur):
        s_next = scores(j + 1)
        finish(j, s_cur, False)
        return s_next

    s_last = lax.fori_loop(0, j_last, body, scores(0))
    finish(j_last, s_last, True)


def _diff_prompt_kernel(lam_ref, q_ref, k_ref, vt_ref, gs_ref, o_ref, qt_ref, m_ref, l_ref, acc_ref, *, tq, tk):
    i = pl.program_id(2)
    dim = lax.broadcasted_iota(jnp.int32, (LANES, tq), 0)
    for g in range(2):
        qgt = q_ref[:, g * LANES:(g + 1) * LANES].astype(F32).T
        qt_ref[:, g * tq:(g + 1) * tq] = jnp.where(dim < DIFF_HEAD_DIM, qgt, 0.0).astype(BF16)
        qt_ref[:, (2 + g) * tq:(3 + g) * tq] = jnp.where(dim >= DIFF_HEAD_DIM, qgt, 0.0).astype(BF16)

    def scores(j, cc):
        off = pl.multiple_of(j * tk, tk)
        return _dot(k_ref[pl.ds(off, tk), :], qt_ref[:, cc * MXU_DIM:(cc + 1) * MXU_DIM])

    _cols_attention(i, tq, tk, 4 * tq, scores, vt_ref, m_ref, l_ref, acc_ref)

    o = acc_ref[...] / l_ref[...]
    odt = o[:, 0:2 * tq] - lam_ref[0] * o[:, 2 * tq:4 * tq]
    for g in range(2):
        od = odt[:, g * tq:(g + 1) * tq].T
        od = _rms_rows(od, SUBLN_EPS) * gs_ref[...]
        o_ref[:, g * LANES:(g + 1) * LANES] = od.astype(BF16)


def _diff_prompt(lam, dq, dk16, dvt, gs):
    b, t, _ = dq.shape
    tq, tk = PROMPT_Q_TILE, dvt.shape[-1]
    kern = functools.partial(_diff_prompt_kernel, tq=tq, tk=tk)
    return pl.pallas_call(
        kern,
        grid=(b, 2, t // tq),
        in_specs=[pl.BlockSpec(memory_space=pltpu.SMEM),
                  pl.BlockSpec((None, tq, 256), lambda bi, n, i: (bi, i, n)),
                  pl.BlockSpec((None, t, LANES), lambda bi, n, i: (bi, 0, n)),
                  pl.BlockSpec((None, t // tk, LANES, tk), lambda bi, n, i: (bi, 0, n, 0)),
                  pl.BlockSpec((1, LANES), lambda bi, n, i: (0, 0))],
        out_specs=pl.BlockSpec((None, tq, 256), lambda bi, n, i: (bi, i, n)),
        out_shape=jax.ShapeDtypeStruct((b, t, 512), BF16),
        scratch_shapes=[pltpu.VMEM((LANES, 4 * tq), BF16), pltpu.VMEM((1, 4 * tq), F32),
                        pltpu.VMEM((1, 4 * tq), F32), pltpu.VMEM((LANES, 4 * tq), F32)],
        compiler_params=_params(("parallel", "parallel", "parallel")),
        name="diff_prompt",
    )(lam, dq, dk16, dvt.reshape(b, t // tk, 256, tk), gs)


def _mla_prompt_kernel(q_ref, k_ref, ct_ref, wuv_ref, o_ref, qt_ref, m_ref, l_ref, acc_ref, *, tq, tk):
    i = pl.program_id(1)
    for h in range(MLA_HEADS):
        qt_ref[:, h * tq:(h + 1) * tq] = q_ref[:, h * LANES:(h + 1) * LANES].astype(F32).T.astype(BF16)

    def scores(j, h):
        off = pl.multiple_of(j * tk, tk)
        return _dot(k_ref[pl.ds(off, tk), h * LANES:(h + 1) * LANES], qt_ref[:, h * tq:(h + 1) * tq])

    _cols_attention(i, tq, tk, MLA_HEADS * tq, scores, ct_ref, m_ref, l_ref, acc_ref)

    lat_t = acc_ref[...] / l_ref[...]
    o = None
    for h in range(MLA_HEADS):
        part = _dot(lat_t[:, h * tq:(h + 1) * tq].T.astype(BF16), wuv_ref[h])
        o = part if o is None else o + part
    o_ref[...] = o.astype(BF16)


def _mla_prompt(mq, kmla, ckvt, wuv2):
    b, t, _ = mq.shape
    tq, tk = PROMPT_Q_TILE, ckvt.shape[-1]
    assert tq == MXU_DIM, "one head per MXU-wide column chunk"
    kern = functools.partial(_mla_prompt_kernel, tq=tq, tk=tk)
    cols = MLA_HEADS * tq
    return pl.pallas_call(
        kern,
        grid=(b, t // tq),
        in_specs=[pl.BlockSpec((None, tq, 1024), lambda bi, i: (bi, i, 0)),
                  pl.BlockSpec((None, t, 1024), lambda bi, i: (bi, 0, 0)),
                  pl.BlockSpec((None, t // tk, LANES, tk), lambda bi, i: (bi, 0, 0, 0)),
                  pl.BlockSpec(wuv2.shape, lambda bi, i: (0, 0, 0))],
        out_specs=pl.BlockSpec((None, tq, 512), lambda bi, i: (bi, i, 0)),
        out_shape=jax.ShapeDtypeStruct((b, t, 512), BF16),
        scratch_shapes=[pltpu.VMEM((LANES, cols), BF16), pltpu.VMEM((1, cols), F32),
                        pltpu.VMEM((1, cols), F32), pltpu.VMEM((LANES, cols), F32)],
        compiler_params=_params(("parallel", "parallel")),
        name="mla_prompt",
    )(mq, kmla, ckvt.reshape(b, t // tk, LANES, tk), wuv2)


def _softmax_pv(s, v):
    m = jnp.max(s, axis=-1, keepdims=True)
    p = jnp.exp(s - m)
    l = jnp.sum(p, axis=-1, keepdims=True)
    return _dot(p.astype(BF16), v) / l


def _mem_prompt_kernel(q_ref, k_ref, v_ref, o_ref):
    for h in range(MEM_HEADS):
        sl = slice(h * LANES, (h + 1) * LANES)
        s = _dot_nt(q_ref[:, sl], k_ref[:, sl])
        o_ref[:, sl] = _softmax_pv(s, v_ref[:, sl]).astype(BF16)


def _mem_prompt(memq, mk16, mv16):
    b, t, _ = memq.shape
    tq = min(2 * ATTN_TILE, t)
    return pl.pallas_call(
        _mem_prompt_kernel,
        grid=(b, t // tq),
        in_specs=[pl.BlockSpec((None, tq, 512), lambda bi, i: (bi, i, 0)),
                  pl.BlockSpec((None, N_MEM, 512), lambda bi, i: (bi, 0, 0)),
                  pl.BlockSpec((None, N_MEM, 512), lambda bi, i: (bi, 0, 0))],
        out_specs=pl.BlockSpec((None, tq, 512), lambda bi, i: (bi, i, 0)),
        out_shape=jax.ShapeDtypeStruct((b, t, 512), BF16),
        compiler_params=_params(("parallel", "parallel")),
        name="mem_prompt",
    )(memq, mk16, mv16)


def _mem_sample_kernel(q_ref, k_ref, v_ref, o_ref, *, bb):
    heads = [(bi, h) for bi in range(bb) for h in range(MEM_HEADS)]
    scores = []
    for bi, h in heads:
        k = k_ref[bi, pl.ds(h, N_MEM, stride=MEM_HEADS), :].astype(BF16)
        scores.append(_dot_nt(q_ref[bi, :, h * LANES:(h + 1) * LANES], k))
    for (bi, h), s in zip(heads, scores):
        v = v_ref[bi, pl.ds(h, N_MEM, stride=MEM_HEADS), :].astype(BF16)
        o_ref[bi, :, h * LANES:(h + 1) * LANES] = _softmax_pv(s, v).astype(BF16)


def _mem_sample(memq, ck, cv, b_off):
    b, t, _ = memq.shape
    bb = math.gcd(b, 4)
    off = b_off // bb
    kern = functools.partial(_mem_sample_kernel, bb=bb)
    rows = N_MEM * MEM_HEADS
    return pl.pallas_call(
        kern,
        grid=(b // bb,),
        in_specs=[pl.BlockSpec((bb, t, 512), lambda i: (i, 0, 0)),
                  pl.BlockSpec((bb, rows, LANES), lambda i: (i + off, 0, 0)),
                  pl.BlockSpec((bb, rows, LANES), lambda i: (i + off, 0, 0))],
        out_specs=pl.BlockSpec((bb, t, 512), lambda i: (i, 0, 0)),
        out_shape=jax.ShapeDtypeStruct((b, t, 512), BF16),
        compiler_params=_params(("parallel",)),
        name="mem_sample",
    )(memq, ck, cv)


def _page_copies(pt_ref, grp, j, slot, bb, ch, nc, streams):
    cps = []
    for bi in range(bb):
        base = ((grp * bb + bi) * nc + j) * ch
        for r in range(ch):
            pg = pt_ref[base + r]
            for src, buf, sem in streams:
                cps.append(pltpu.make_async_copy(src.at[pg], buf.at[slot, bi * ch + r], sem.at[slot]))
    return cps


def _stream_begin(pt_ref, bb, ch, nc, total, streams):
    grp, j = pl.program_id(0), pl.program_id(1)
    s = grp * nc + j
    slot = lax.rem(s, PAGE_BUFFERS)

    @pl.when(s == 0)
    def _():
        for ahead in range(min(PAGE_BUFFERS - 1, total)):
            for cp in _page_copies(pt_ref, ahead // nc, ahead % nc, ahead, bb, ch, nc, streams):
                cp.start()

    for cp in _page_copies(pt_ref, grp, j, slot, bb, ch, nc, streams):
        cp.wait()
    return slot


def _stream_end(pt_ref, bb, ch, nc, total, streams):
    s = pl.program_id(0) * nc + pl.program_id(1)
    nxt = s + (PAGE_BUFFERS - 1)

    @pl.when(nxt < total)
    def _():
        for cp in _page_copies(pt_ref, lax.div(nxt, nc), lax.rem(nxt, nc), lax.rem(nxt, PAGE_BUFFERS),
                               bb, ch, nc, streams):
            cp.start()


def _causal_new(s2, nq):
    qpos = lax.broadcasted_iota(jnp.int32, s2.shape, 0) & (nq - 1)
    kpos = lax.broadcasted_iota(jnp.int32, s2.shape, 1)
    return jnp.where(kpos <= qpos, s2, NEG_INF)


def _decode_kernel(pt_ref, lam_ref, qd_ref, kn_ref, vn_ref, gs_ref, mq_ref, kmn_ref, cn_ref, gk_ref, aq_ref,
                   wukt_ref, wuv_ref, kc_hbm, vc_hbm, cc_hbm, rc_hbm, od_ref, om_ref,
                   kbuf, vbuf, cbuf, rbuf, sem, md_ref, ld_ref, accd_ref, lw_ref, qr_ref, mm_ref, lm_ref, accm_ref,
                   *, bb, ch, nc, total):
    j = pl.program_id(1)
    streams = [(kc_hbm, kbuf, sem.at[0]), (vc_hbm, vbuf, sem.at[1]), (cc_hbm, cbuf, sem.at[2]), (rc_hbm, rbuf, sem.at[3])]
    slot = _stream_begin(pt_ref, bb, ch, nc, total, streams)
    nq = mq_ref.shape[1]
    nr = MLA_HEADS * nq
    nk = MLA_HEADS * MLA_NOPE

    @pl.when(j == 0)
    def _():
        for ref in (md_ref, mm_ref):
            ref[...] = jnp.full(ref.shape, NEG_INF, F32)
        for ref in (ld_ref, lm_ref, accd_ref, accm_ref):
            ref[...] = jnp.zeros(ref.shape, F32)
        for bi in range(bb):
            lw_ref[bi, 0:nk, :] = wukt_ref[...]
            qg = (mq_ref[bi].astype(F32) * gk_ref[...]).astype(BF16)
            for h in range(MLA_HEADS):
                qa = _dot(qg[:, h * LANES:(h + 1) * LANES], aq_ref[h])
                lw_ref[bi, nk + h * nq:nk + (h + 1) * nq, :] = qa[:, 0:LANES].astype(BF16)
                qr_ref[bi, h * nq:(h + 1) * nq, :] = qa[:, LANES + 64:LANES + 96].astype(BF16)

    scores_d, scores_m, lat_in = [], [], []
    for bi in range(bb):
        pages = range(bi * ch, (bi + 1) * ch)
        kt = jnp.concatenate([kbuf[slot, r].astype(BF16) for r in pages], axis=-1)
        scores_d.append(_dot(qd_ref[bi], kt))
        c = jnp.concatenate([cbuf[slot, r] for r in pages], axis=0).astype(BF16)
        krt = jnp.concatenate([rbuf[slot, r] for r in pages], axis=-1)
        big = _dot_nt(lw_ref[bi], c)
        nkeys = big.shape[1]
        knt = big[0:nk]
        ssq = jnp.sum((knt * knt).reshape(MLA_HEADS, MLA_NOPE, nkeys), axis=1)
        ssq = ssq + jnp.sum(krt * krt, axis=0, keepdims=True)
        rn = lax.rsqrt(ssq * (1.0 / MLA_QK) + NORM_EPS)
        sm = big[nk:nk + nr] + _dot(qr_ref[bi], krt.astype(BF16))
        scores_m.append((sm.reshape(MLA_HEADS, nq, nkeys) * rn[:, None, :]).reshape(nr, nkeys))
        lat_in.append(c)

    for bi in range(bb):
        pages = range(bi * ch, (bi + 1) * ch)
        p, alpha = _online_update(scores_d[bi], md_ref.at[bi], ld_ref.at[bi])
        p = p.astype(BF16)
        pv = []
        for n in range(2):
            v = jnp.concatenate([vbuf[slot, r, pl.ds(n, PAGE, stride=2), :].astype(BF16) for r in pages], axis=0)
            pv.append(_dot(p[n * 32:(n + 1) * 32], v))
        accd_ref[bi] = alpha * accd_ref[bi] + jnp.concatenate(pv, axis=0)
        pm, alpham = _online_update(scores_m[bi], mm_ref.at[bi], lm_ref.at[bi])
        accm_ref[bi] = alpham * accm_ref[bi] + _dot(pm.astype(BF16), lat_in[bi])

    _stream_end(pt_ref, bb, ch, nc, total, streams)

    @pl.when(j == nc - 1)
    def _():
        for bi in range(bb):
            q = qd_ref[bi]
            vn = vn_ref[bi]
            p2, alpha2 = _online_update(_causal_new(_dot_nt(q, kn_ref[bi]), nq), md_ref.at[bi], ld_ref.at[bi])
            p2 = p2.astype(BF16)
            pv2 = [_dot(p2[n * 32:(n + 1) * 32], vn[:, n * LANES:(n + 1) * LANES]) for n in range(2)]
            o = (alpha2 * accd_ref[bi] + jnp.concatenate(pv2, axis=0)) / ld_ref[bi]
            for n in range(2):
                on = o[n * 32:(n + 1) * 32]
                od = on[0:16] - lam_ref[0] * on[16:32]
                od = _rms_rows(od, SUBLN_EPS) * gs_ref[...]
                for g in range(2):
                    hh = n * 2 + g
                    od_ref[bi, :, hh * LANES:(hh + 1) * LANES] = od[g * nq:(g + 1) * nq].astype(BF16)

            qf = mq_ref[bi]
            kn = kmn_ref[bi]
            s2 = jnp.concatenate([_dot_nt(qf[:, h * LANES:(h + 1) * LANES], kn[:, h * LANES:(h + 1) * LANES])
                                  for h in range(MLA_HEADS)], axis=0)
            p3, alpha3 = _online_update(_causal_new(s2, nq), mm_ref.at[bi], lm_ref.at[bi])
            lat = ((alpha3 * accm_ref[bi] + _dot(p3.astype(BF16), cn_ref[bi])) / lm_ref[bi]).astype(BF16)
            om = _dot(lat[0:nq], wuv_ref[0])
            for h in range(1, MLA_HEADS):
                om = om + _dot(lat[h * nq:(h + 1) * nq], wuv_ref[h])
            om_ref[bi] = om.astype(BF16)


def _decode(pt_flat, lam, qbd, kn16, vn16, gs, mq, kmla, ckv16, gk_pad, aq, wukt, wuv2, kc, vc, cc, rc, n_pages):
    b, nq, _ = mq.shape
    assert nq == 8, "score-row layouts assume 8 new tokens per sequence"
    ch = math.gcd(PAGES_PER_STEP, n_pages)
    bb = math.gcd(SEQS_PER_STEP, b)
    nc = n_pages // ch
    kern = functools.partial(_decode_kernel, bb=bb, ch=ch, nc=nc, total=(b // bb) * nc)
    nr = MLA_HEADS * nq
    per_seq = lambda w, r=nq: pl.BlockSpec((bb, r, w), lambda g, j, pt: (g, 0, 0))
    const = lambda a: pl.BlockSpec(a.shape, lambda g, j, pt: (0,) * a.ndim)
    grid_spec = pltpu.PrefetchScalarGridSpec(
        num_scalar_prefetch=1,
        grid=(b // bb, nc),
        in_specs=[pl.BlockSpec(memory_space=pltpu.SMEM), per_seq(256, 64), per_seq(256), per_seq(256), const(gs),
                  per_seq(1024), per_seq(1024), per_seq(LANES), const(gk_pad), const(aq), const(wukt), const(wuv2)]
                 + [pl.BlockSpec(memory_space=pl.ANY)] * 4,
        out_specs=[per_seq(512), per_seq(512)],
        scratch_shapes=[pltpu.VMEM((PAGE_BUFFERS, bb * ch, 256, LANES), F32),
                        pltpu.VMEM((PAGE_BUFFERS, bb * ch, 256, LANES), F32),
                        pltpu.VMEM((PAGE_BUFFERS, bb * ch, PAGE, LANES), F32),
                        pltpu.VMEM((PAGE_BUFFERS, bb * ch, MLA_ROPE, PAGE), F32),
                        pltpu.SemaphoreType.DMA((4, PAGE_BUFFERS)),
                        pltpu.VMEM((bb, 64, 1), F32), pltpu.VMEM((bb, 64, 1), F32), pltpu.VMEM((bb, 64, LANES), F32),
                        pltpu.VMEM((bb, MLA_HEADS * MLA_NOPE + nr, LANES), BF16), pltpu.VMEM((bb, nr, MLA_ROPE), BF16),
                        pltpu.VMEM((bb, nr, 1), F32), pltpu.VMEM((bb, nr, 1), F32), pltpu.VMEM((bb, nr, LANES), F32)],
    )
    return pl.pallas_call(
        kern,
        grid_spec=grid_spec,
        out_shape=[jax.ShapeDtypeStruct((b, nq, 512), BF16)] * 2,
        compiler_params=_params(("arbitrary", "arbitrary")),
        name="decode",
    )(pt_flat, lam, qbd, kn16, vn16, gs, mq, kmla, ckv16, gk_pad, aq, wukt, wuv2, kc, vc, cc, rc)


def _merge_kernel(xp_ref, odp_ref, omp_ref, ocp_ref, xs_ref, ods_ref, oms_ref, ocs_ref,
                  gmix_ref, wg_ref, bg_ref, wbr_ref, wout_ref, gffn_ref,
                  wrh_ref, wrl_ref, br_ref, xm_ref, h2_ref, tv_ref, ti_ref, *, p_tiles):
    is_p = pl.program_id(0) < p_tiles
    pick = lambda a_ref, b_ref: jnp.where(is_p, a_ref[...], b_ref[...])
    x = pick(xp_ref, xs_ref)
    h = (_rms_rows(x) * gmix_ref[...]).astype(BF16)
    gates = jax.nn.sigmoid(_dot(h, wg_ref[...]) + bg_ref[...])
    merged = gates[:, 0:D_MODEL] * _dot(pick(odp_ref, ods_ref), wbr_ref[0])
    merged = merged + gates[:, D_MODEL:2 * D_MODEL] * _dot(pick(omp_ref, oms_ref), wbr_ref[1])
    merged = merged + gates[:, 2 * D_MODEL:3 * D_MODEL] * _dot(pick(ocp_ref, ocs_ref), wbr_ref[2])
    xm = x + _dot(merged.astype(BF16), wout_ref[...])
    xm_ref[...] = xm
    h2 = _rms_rows(xm) * gffn_ref[...]
    h2_ref[...] = pltpu.einshape("stl->tsl", jnp.stack([h2[:, s * LANES:(s + 1) * LANES]
                                                        for s in range(D_MODEL // LANES)], axis=0))
    hh = h2.astype(BF16)
    hl = (h2 - hh.astype(F32)).astype(BF16)
    logits = _dot(hh, wrh_ref[...]) + _dot(hl, wrh_ref[...]) + _dot(hh, wrl_ref[...]) + br_ref[...]
    lane = lax.broadcasted_iota(jnp.int32, logits.shape, 1)
    logits = jnp.where(lane < N_EXPERTS, logits, -jnp.inf)
    tv = jnp.zeros(logits.shape, F32)
    ti = jnp.zeros(logits.shape, jnp.int32)
    vals = []
    for k in range(TOP_K):
        mx = jnp.max(logits, axis=-1, keepdims=True)
        idx = jnp.min(jnp.where(logits == mx, lane, LANES), axis=-1, keepdims=True)
        vals.append(mx)
        ti = jnp.where(lane == k, idx, ti)
        logits = jnp.where(lane == idx, -jnp.inf, logits)
    es = [jnp.exp(v - vals[0]) for v in vals]
    den = es[0] + es[1] + es[2] + es[3]
    for k in range(TOP_K):
        tv = jnp.where(lane == k, es[k] / den, tv)
    tv_ref[...] = tv
    ti_ref[...] = ti


def _merge(group_p, group_s, wts):
    n_p, n_s = group_p[0].shape[0], group_s[0].shape[0]
    tm = math.gcd(TOKEN_TILE, math.gcd(n_p, n_s))
    p_tiles, n = n_p // tm, n_p + n_s
    row = lambda w: pl.BlockSpec((tm, w), lambda i: (i, 0))
    row_p = lambda w: pl.BlockSpec((tm, w), lambda i: (jnp.minimum(i, p_tiles - 1), 0))
    row_s = lambda w: pl.BlockSpec((tm, w), lambda i: (jnp.maximum(i - p_tiles, 0), 0))
    widths = (D_MODEL, 512, 512, 512)
    return pl.pallas_call(
        functools.partial(_merge_kernel, p_tiles=p_tiles),
        grid=(n // tm,),
        in_specs=[row_p(w) for w in widths] + [row_s(w) for w in widths] + [_full(a) for a in wts],
        out_specs=[row(D_MODEL), pl.BlockSpec((tm, 8, LANES), lambda i: (i, 0, 0)), row(LANES), row(LANES)],
        out_shape=[jax.ShapeDtypeStruct((n, D_MODEL), F32), jax.ShapeDtypeStruct((n, 8, LANES), F32),
                   jax.ShapeDtypeStruct((n, LANES), F32), jax.ShapeDtypeStruct((n, LANES), jnp.int32)],
        compiler_params=_params(("parallel",)),
        name="merge",
    )(*group_p, *group_s, *wts)


def _expert_kernel(be_ref, first_ref, nval_ref, slot_ref, wgu_ref, bgu_ref, wd_ref, bd_ref, x_hbm, y_hbm,
                   slot_smem, xbuf, ybuf, zbuf, wgu16, wd16, gsem, ssem, isem, *, n_blocks, n_real):
    i = pl.program_id(0)
    nv = nval_ref[i]
    groups = lambda k: lax.shift_right_logical(nval_ref[k] + (ROW_GROUP - 1), ROW_GROUP.bit_length() - 1)

    def idx_copy(k):
        ring = lax.rem(k, 3)
        return pltpu.make_async_copy(slot_ref.at[k], slot_smem.at[ring], isem.at[ring])

    halves = slot_smem.shape[1] // 2
    bm = halves * LANES

    def for_groups(k, fn):
        used = groups(k)
        for g in range(bm // ROW_GROUP):
            pl.when(g < used)(functools.partial(fn, g * ROW_GROUP))

    def issue_gather(k):
        ring, xb = lax.rem(k, 3), k & 1

        def rows(r0):
            for r in range(r0, r0 + ROW_GROUP):
                tok = slot_smem[ring, halves + r // LANES, r % LANES]
                pltpu.make_async_copy(x_hbm.at[tok], xbuf.at[xb, r], gsem.at[xb]).start()

        for_groups(k, rows)

    def wait_gather(k):
        xb = k & 1
        for_groups(k, lambda r0: pltpu.make_async_copy(
            x_hbm.at[pl.ds(0, ROW_GROUP)], xbuf.at[xb, pl.ds(r0, ROW_GROUP)], gsem.at[xb]).wait())

    def issue_scatter(k):
        ring = lax.rem(k, 3)

        def rows(r0):
            for r in range(r0, r0 + ROW_GROUP):
                slot = slot_smem[ring, r // LANES, r % LANES]
                pltpu.make_async_copy(ybuf.at[r], y_hbm.at[slot], ssem.at[0]).start()

        for_groups(k, rows)

    def wait_scatter(k):
        for_groups(k, lambda r0: pltpu.make_async_copy(
            ybuf.at[pl.ds(r0, ROW_GROUP)], y_hbm.at[pl.ds(0, ROW_GROUP)], ssem.at[0]).wait())

    @pl.when(i == 0)
    def _():
        xbuf[...] = jnp.zeros(xbuf.shape, F32)
        zbuf[...] = jnp.zeros(zbuf.shape, F32)
        spare = pltpu.make_async_copy(zbuf, y_hbm.at[pl.ds(n_real, ROW_GROUP)], ssem.at[0])
        spare.start()
        spare.wait()
        first_idx = idx_copy(0)
        first_idx.start()
        first_idx.wait()
        issue_gather(0)
        if n_blocks > 1:
            idx_copy(1).start()

    @pl.when(i + 1 < n_blocks)
    def _():
        idx_copy(i + 1).wait()
        issue_gather(i + 1)

    @pl.when(i + 2 < n_blocks)
    def _():
        idx_copy(i + 2).start()

    @pl.when(nv > 0)
    def _():
        @pl.when(first_ref[i] == 1)
        def _():
            wgu16[...] = wgu_ref[...].astype(BF16)
            wd16[...] = wd_ref[...].astype(BF16)

        wait_gather(i)
        xb = i & 1
        xt = pltpu.einshape("rsl->srl", xbuf[xb])
        x = jnp.concatenate([xt[s] for s in range(D_MODEL // LANES)], axis=-1).astype(BF16)
        n_chunks = D_EXPERT // MXU_DIM

        def gate_up(c):
            g0, u0 = c * MXU_DIM, D_EXPERT + c * MXU_DIM
            return (_dot(x, wgu16[:, g0:g0 + MXU_DIM]) + bgu_ref[:, g0:g0 + MXU_DIM],
                    _dot(x, wgu16[:, u0:u0 + MXU_DIM]) + bgu_ref[:, u0:u0 + MXU_DIM])

        y = jnp.broadcast_to(bd_ref[...], (bm, D_MODEL))
        nxt = gate_up(0)
        for c in range(n_chunks):
            gate, up = nxt
            if c + 1 < n_chunks:
                nxt = gate_up(c + 1)
            gate = jnp.minimum(gate, SWIGLU_LIMIT)
            up = jnp.clip(up, -SWIGLU_LIMIT, SWIGLU_LIMIT)
            act = (up + 1.0) * gate * jax.nn.sigmoid(SWIGLU_ALPHA * gate)
            y = y + _dot(act.astype(BF16), wd16[c * MXU_DIM:(c + 1) * MXU_DIM, :])

        @pl.when(i > 0)
        def _():
            wait_scatter(i - 1)

        ybuf[...] = pltpu.einshape("srl->rsl", jnp.stack([y[:, s * LANES:(s + 1) * LANES]
                                                          for s in range(D_MODEL // LANES)], axis=0))
        issue_scatter(i)

    @pl.when((nv == 0) & (i > 0))
    def _():
        wait_scatter(i - 1)

    @pl.when(i == n_blocks - 1)
    def _():
        wait_scatter(i)


def _experts(block_e, first, nval, row_slot3, wgu, bgu, wd, bd, x3, n_real):
    n_blocks = row_slot3.shape[0]
    bm = row_slot3.shape[1] // 2 * LANES
    n_slots = n_real + ROW_GROUP
    kern = functools.partial(_expert_kernel, n_blocks=n_blocks, n_real=n_real)
    grid_spec = pltpu.PrefetchScalarGridSpec(
        num_scalar_prefetch=3,
        grid=(n_blocks,),
        in_specs=[pl.BlockSpec(row_slot3.shape, lambda i, be, fi, na: (0, 0, 0)),
                  pl.BlockSpec((None, D_MODEL, 2 * D_EXPERT), lambda i, be, fi, na: (be[i], 0, 0)),
                  pl.BlockSpec((None, 1, 2 * D_EXPERT), lambda i, be, fi, na: (be[i], 0, 0)),
                  pl.BlockSpec((None, D_EXPERT, D_MODEL), lambda i, be, fi, na: (be[i], 0, 0)),
                  pl.BlockSpec((None, 1, D_MODEL), lambda i, be, fi, na: (be[i], 0, 0)),
                  pl.BlockSpec(memory_space=pl.ANY)],
        out_specs=pl.BlockSpec(memory_space=pl.ANY),
        scratch_shapes=[pltpu.SMEM((3,) + row_slot3.shape[1:], jnp.int32),
                        pltpu.VMEM((2, bm, 8, LANES), F32), pltpu.VMEM((bm, 8, LANES), F32),
                        pltpu.VMEM((ROW_GROUP, 8, LANES), F32),
                        pltpu.VMEM((D_MODEL, 2 * D_EXPERT), BF16), pltpu.VMEM((D_EXPERT, D_MODEL), BF16),
                        pltpu.SemaphoreType.DMA((2,)), pltpu.SemaphoreType.DMA((1,)), pltpu.SemaphoreType.DMA((3,))],
    )
    return pl.pallas_call(
        kern,
        grid_spec=grid_spec,
        out_shape=jax.ShapeDtypeStruct((n_slots, 8, LANES), F32),
        compiler_params=_params(("arbitrary",)),
        name="experts",
    )(block_e, first, nval, row_slot3, wgu, bgu, wd, bd, x3)


def _combine_kernel(xm_ref, tv_ref, y_ref, o_ref):
    tv = tv_ref[...]
    gate = [jnp.broadcast_to(tv[:, k:k + 1], (tv.shape[0], LANES)) for k in range(TOP_K)]
    y = pltpu.einshape("tjl->jtl", y_ref[...])
    for s in range(D_MODEL // LANES):
        acc = xm_ref[:, s * LANES:(s + 1) * LANES]
        for k in range(TOP_K):
            acc = acc + gate[k] * y[k * 8 + s]
        o_ref[:, s * LANES:(s + 1) * LANES] = acc


def _combine(xm, tv, yslots, tok_off, n):
    tm = math.gcd(TOKEN_TILE, math.gcd(n, tok_off)) if tok_off else min(TOKEN_TILE, n)
    off = tok_off // tm
    y4 = yslots.reshape(yslots.shape[0] // TOP_K, TOP_K * 8, LANES)
    return pl.pallas_call(
        _combine_kernel,
        grid=(n // tm,),
        in_specs=[pl.BlockSpec((tm, D_MODEL), lambda i: (i + off, 0)),
                  pl.BlockSpec((tm, LANES), lambda i: (i + off, 0)),
                  pl.BlockSpec((tm, TOP_K * 8, LANES), lambda i: (i + off, 0, 0))],
        out_specs=pl.BlockSpec((tm, D_MODEL), lambda i: (i, 0)),
        out_shape=jax.ShapeDtypeStruct((n, D_MODEL), F32),
        compiler_params=_params(("parallel",)),
        name="combine",
    )(xm, tv, y4)


def _rope_tables(pos, rows):
    pos = pos.astype(F32)[:, None]
    lane = jnp.arange(LANES)

    def ang(half):
        inv = jnp.power(ROPE_THETA, -jnp.arange(half, dtype=F32) / half)
        return pos * inv[None, :]

    a64 = ang(32)[:, lane % 32]
    first = (lane % 64) < 32
    c64 = jnp.cos(a64)
    sa64 = jnp.where(first, -jnp.sin(a64), 0.0)
    sb64 = jnp.where(first, 0.0, jnp.sin(a64))
    a32 = ang(16)[:, lane % 16]
    in_a = (lane >= 64) & (lane < 80)
    in_b = (lane >= 80) & (lane < 96)
    c32 = jnp.where(in_a | in_b, jnp.cos(a32), 1.0)
    sa32 = jnp.where(in_a, -jnp.sin(a32), 0.0)
    sb32 = jnp.where(in_b, jnp.sin(a32), 0.0)
    tabs = [c64, sa64, sb64, c32, sa32, sb32]
    reps = rows // pos.shape[0]
    return [jnp.tile(t, (reps, 1)) if reps > 1 else t for t in tabs]


def _tied_pad(g, scale):
    blk = jnp.concatenate([g[:MLA_NOPE], g[MLA_NOPE:], g[MLA_NOPE:], jnp.zeros((32,), F32)]) * scale
    return jnp.tile(blk, MLA_HEADS)[None, :]


def _layer_weights(lp):
    w_in = lp["w_in"]
    kr_blk = jnp.zeros((D_MODEL, LANES), F32).at[:, 64:96].set(w_in[:, _OFF_KR:_OFF_MQ])
    wa = jnp.concatenate([w_in[:, _OFF_Q:_OFF_KR], kr_blk, w_in[:, _OFF_MQ:_OFF_G]], axis=1).astype(BF16)
    wuq = jnp.pad(lp["w_mla_uq"], ((0, 0), (0, 0), (0, LANES - MLA_QK))).reshape(MLA_Q_LORA, MLA_HEADS * LANES).astype(BF16)
    wuk_pad = jnp.pad(lp["w_mla_uk"], ((0, 0), (0, 0), (0, LANES - MLA_NOPE))).reshape(MLA_KV_LORA, MLA_HEADS * LANES)
    lane = jnp.arange(LANES)
    rope_eye = jnp.where(((lane >= 64) & (lane < 96))[:, None], jnp.eye(LANES, dtype=F32), 0.0)
    wk = jnp.concatenate([wuk_pad, jnp.tile(rope_eye, (1, MLA_HEADS))], axis=0).astype(BF16)
    g64 = jnp.kron(jnp.eye(4, dtype=F32), jnp.ones((64, 64), F32)).astype(BF16)
    g128 = jnp.kron(jnp.eye(2, dtype=F32), jnp.ones((128, 128), F32)).astype(BF16)
    gk_pad = _tied_pad(lp["mla_k_norm"], 1.0)
    inproj_w = [lp["norm_mix"][None, :], wa, wuq, wk, g64, g128,
                jnp.tile(lp["diff_q_norm"], 8)[None, :] * DIFF_SCALE, jnp.tile(lp["diff_k_norm"], 4)[None, :],
                lp["mla_q_a_norm"][None, :], lp["mla_kv_a_norm"][None, :],
                _tied_pad(lp["mla_q_norm"], MLA_SCALE), gk_pad,
                jnp.tile(lp["mem_q_norm"], MEM_HEADS)[None, :] * MEM_SCALE]
    wuv = lp["w_mla_uv"]
    wuv2 = jnp.einsum("rhd,hg->hrgd", wuv, jnp.eye(MLA_HEADS, dtype=F32)).reshape(MLA_HEADS, MLA_KV_LORA, 512).astype(BF16)
    wuk_t = jnp.transpose(lp["w_mla_uk"], (1, 2, 0))
    aq = jnp.zeros((MLA_HEADS, LANES, 2 * LANES), F32).at[:, 0:MLA_NOPE, 0:LANES].set(wuk_t)
    aq = aq.at[:, :, LANES:].add(rope_eye[None]).astype(BF16)
    wukt = wuk_t.reshape(MLA_HEADS * MLA_NOPE, MLA_KV_LORA).astype(BF16)
    wr = jnp.pad(lp["w_router"], ((0, 0), (0, LANES - N_EXPERTS)))
    wrh = wr.astype(BF16)
    wrl = (wr - wrh.astype(F32)).astype(BF16)
    merge_w = [lp["norm_mix"][None, :], w_in[:, _OFF_G:].astype(BF16), lp["b_gate"][None, :],
               lp["w_branch"].reshape(3, 512, D_MODEL).astype(BF16), lp["w_out"].astype(BF16),
               lp["norm_ffn"][None, :], wrh, wrl, jnp.pad(lp["b_router"], (0, LANES - N_EXPERTS))[None, :]]
    return dict(inproj=inproj_w, g128=g128, wuv2=wuv2, aq=aq, wukt=wukt, gk_pad=gk_pad, merge=merge_w)


def _moe_plan(ti, bm):
    n = ti.shape[0]
    a = n * TOP_K
    flat_e = ti[:, :TOP_K].reshape(a)
    onehot = (flat_e[:, None] == jnp.arange(N_EXPERTS, dtype=jnp.int32)[None, :]).astype(jnp.int32)
    csum = jnp.cumsum(onehot, axis=0)
    counts = csum[-1]
    rank = jnp.sum(onehot * csum, axis=1) - 1
    padded = (counts + bm - 1) // bm * bm
    pend = jnp.cumsum(padded)
    pstart = pend - padded
    dest = jnp.sum(onehot * pstart[None, :], axis=1) + rank
    n_rows = (a + N_EXPERTS * (bm - 1) + bm - 1) // bm * bm
    n_blocks = n_rows // bm
    spare = a + (jnp.arange(n_rows, dtype=jnp.int32) % ROW_GROUP)
    row_slot = spare.at[dest].set(jnp.arange(a, dtype=jnp.int32))
    blk_start = jnp.arange(n_blocks, dtype=jnp.int32) * bm
    block_e = jnp.minimum(jnp.sum((blk_start[:, None] >= pend[None, :]).astype(jnp.int32), axis=1), N_EXPERTS - 1)
    first = jnp.concatenate([jnp.ones((1,), jnp.int32), (block_e[1:] != block_e[:-1]).astype(jnp.int32)])
    eh = (block_e[:, None] == jnp.arange(N_EXPERTS, dtype=jnp.int32)[None, :]).astype(jnp.int32)
    valid_end = jnp.sum(eh * (pstart + counts)[None, :], axis=1)
    nval = jnp.where(blk_start < pend[-1], jnp.clip(valid_end - blk_start, 0, bm), 0).astype(jnp.int32)
    row_tok = jnp.minimum(lax.shift_right_logical(row_slot, 2), n - 1)
    row_idx = jnp.concatenate([row_slot.reshape(n_blocks, bm // LANES, LANES),
                               row_tok.reshape(n_blocks, bm // LANES, LANES)], axis=1)
    return block_e, first, nval, row_idx, a


def _qbd(dq_s, b, nq):
    q = dq_s.reshape(b, nq, 2, 2, 2, DIFF_HEAD_DIM)
    q = jnp.transpose(q, (0, 2, 4, 3, 1, 5))
    eye = jnp.eye(4, dtype=q.dtype).reshape(2, 2, 2, 2)
    out = jnp.einsum("bncgqd,ncmk->bncgqmkd", q, eye)
    return out.reshape(b, 64, 256)


def kernel(x_prompt, x_sample, mem_prompt, cache_diff_k, cache_diff_v, cache_mla_ckv, cache_mla_krope, cache_mem_k, cache_mem_v, page_table, norm_mix, norm_mem, w_in, b_gate, diff_q_norm, diff_k_norm, diff_lambda, diff_subln, mla_q_a_norm, w_mla_uq, mla_kv_a_norm, w_mla_uk, w_mla_uv, mla_q_norm, mla_k_norm, w_mem_kv, mem_q_norm, mem_k_norm, w_branch, w_out, norm_ffn, w_router, b_router, w_gate_up, b_gate_up, w_down, b_down):
    depth = w_in.shape[0]
    bp, t, _ = x_prompt.shape
    bs, nq, _ = x_sample.shape
    n_pool, n_pages = cache_diff_k.shape[1], page_table.shape[1]
    past_len = n_pages * PAGE
    n_p, n_s = bp * t, bs * nq

    tm_p, tm_s = min(TOKEN_TILE, n_p), min(TOKEN_TILE, n_s)
    tabs_p = _rope_tables(jnp.arange(t, dtype=jnp.int32), t)
    tabs_s = _rope_tables(past_len + jnp.arange(nq, dtype=jnp.int32), tm_s)
    del tm_p

    kc = jnp.transpose(cache_diff_k, (0, 1, 3, 4, 5, 2)).reshape(depth * n_pool, 256, PAGE)
    vc = cache_diff_v.reshape(depth * n_pool, 2 * PAGE, DIFF_V_DIM)
    cc = cache_mla_ckv.reshape(depth * n_pool, PAGE, MLA_KV_LORA)
    rc = jnp.transpose(cache_mla_krope, (0, 1, 3, 2)).reshape(depth * n_pool, MLA_ROPE, PAGE)
    mkc = cache_mem_k.reshape(depth * bs, N_MEM * MEM_HEADS, MEM_HEAD_DIM)
    mvc = cache_mem_v.reshape(depth * bs, N_MEM * MEM_HEADS, MEM_HEAD_DIM)

    xp = x_prompt.reshape(n_p, D_MODEL)
    xs = x_sample.reshape(n_s, D_MODEL)
    outs = [[] for _ in range(10)]
    for layer in range(depth):
        lp = dict(norm_mix=norm_mix[layer], w_in=w_in[layer], b_gate=b_gate[layer], diff_q_norm=diff_q_norm[layer],
                  diff_k_norm=diff_k_norm[layer], mla_q_a_norm=mla_q_a_norm[layer], w_mla_uq=w_mla_uq[layer],
                  mla_kv_a_norm=mla_kv_a_norm[layer], w_mla_uk=w_mla_uk[layer], w_mla_uv=w_mla_uv[layer],
                  mla_q_norm=mla_q_norm[layer], mla_k_norm=mla_k_norm[layer], mem_q_norm=mem_q_norm[layer],
                  w_branch=w_branch[layer], w_out=w_out[layer], norm_ffn=norm_ffn[layer],
                  w_router=w_router[layer], b_router=b_router[layer])
        w = _layer_weights(lp)
        lam_init = 0.8 - 0.6 * math.exp(-0.3 * layer)
        lamp = diff_lambda[layer].astype(F32)
        lam = (jnp.exp(jnp.sum(lamp[0] * lamp[1])) - jnp.exp(jnp.sum(lamp[2] * lamp[3])) + lam_init).reshape(1)
        gs = (diff_subln[layer] * (1.0 - lam_init))[None, :]
        pt_flat = (page_table + layer * n_pool).reshape(-1).astype(jnp.int32)

        (dq, _, dk16, dv32, dv16, mq, ckv32, ckv16, kr32, kmla, memq, dvt, ckvt, dkt) = _inproj(xp, tabs_p, w["inproj"], t)
        p_dk = jnp.transpose(dkt.reshape(bp, 2, 2, DIFF_HEAD_DIM, t), (0, 4, 1, 2, 3))
        o_diff = _diff_prompt(lam, dq.reshape(bp, t, 512), dk16.reshape(bp, t, 256), dvt, gs)
        o_mla = _mla_prompt(mq.reshape(bp, t, 1024), kmla.reshape(bp, t, 1024), ckvt, w["wuv2"])
        mk32, mk16, mv32, mv16 = _memkv(mem_prompt.reshape(bp * N_MEM, D_MODEL), norm_mem[layer][None, :],
                                        w_mem_kv[layer].astype(BF16), w["g128"],
                                        jnp.tile(mem_k_norm[layer], MEM_HEADS)[None, :])
        o_mem = _mem_prompt(memq.reshape(bp, t, 512), mk16.reshape(bp, N_MEM, 512), mv16.reshape(bp, N_MEM, 512))
        group_p = (xp, o_diff.reshape(n_p, 512), o_mla.reshape(n_p, 512), o_mem.reshape(n_p, 512))
        for lst, val in zip(outs[:6], (p_dk, dv32.reshape(bp, t, 2, 128),
                                       ckv32.reshape(bp, t, 128), kr32.reshape(bp, t, 32),
                                       mk32.reshape(bp, N_MEM, 4, 128), mv32.reshape(bp, N_MEM, 4, 128))):
            lst.append(val)

        (dq, dk32, dk16, dv32, dv16, mq, ckv32, ckv16, kr32, kmla, memq, _, _, _) = _inproj(xs, tabs_s, w["inproj"],
                                                                                            min(TOKEN_TILE, n_s))
        o_diff, o_mla = _decode(pt_flat, lam, _qbd(dq, bs, nq), dk16.reshape(bs, nq, 256), dv16.reshape(bs, nq, 256), gs,
                                mq.reshape(bs, nq, 1024), kmla.reshape(bs, nq, 1024), ckv16.reshape(bs, nq, LANES),
                                w["gk_pad"], w["aq"], w["wukt"], w["wuv2"], kc, vc, cc, rc, n_pages)
        o_mem = _mem_sample(memq.reshape(bs, nq, 512), mkc, mvc, layer * bs)
        group_s = (xs, o_diff.reshape(n_s, 512), o_mla.reshape(n_s, 512), o_mem.reshape(n_s, 512))
        for lst, val in zip(outs[6:], (dk32.reshape(bs, nq, 2, 2, 64), dv32.reshape(bs, nq, 2, 128),
                                       ckv32.reshape(bs, nq, 128), kr32.reshape(bs, nq, 32))):
            lst.append(val)

        xm, h2, tv, ti = _merge(group_p, group_s, w["merge"])
        block_e, first, nval, row_slot3, n_slots = _moe_plan(ti, MOE_ROWS)
        yslots = _experts(block_e + layer * N_EXPERTS, first, nval, row_slot3,
                          w_gate_up.reshape(depth * N_EXPERTS, D_MODEL, 2 * D_EXPERT),
                          b_gate_up.reshape(depth * N_EXPERTS, 1, 2 * D_EXPERT),
                          w_down.reshape(depth * N_EXPERTS, D_EXPERT, D_MODEL),
                          b_down.reshape(depth * N_EXPERTS, 1, D_MODEL), h2, n_slots)
        xp = _combine(xm, tv, yslots, 0, n_p)
        xs = _combine(xm, tv, yslots, n_p, n_s)

    stack = lambda lst: jnp.stack(lst)
    return (xp.reshape(bp, t, D_MODEL), xs.reshape(bs, nq, D_MODEL)) + tuple(stack(o) for o in outs)
```

```python
import functools
import math

import jax
import jax.numpy as jnp
from jax import lax
from jax.experimental import pallas as pl
from jax.experimental.pallas import tpu as pltpu

F32 = jnp.float32
BF16 = jnp.bfloat16

D_MODEL = 1024
DIFF_HEAD_DIM = 64
DIFF_V_DIM = 128
MLA_HEADS = 8
MLA_Q_LORA = 256
MLA_KV_LORA = 128
MLA_NOPE = 64
MLA_ROPE = 32
MLA_V = 64
MLA_QK = MLA_NOPE + MLA_ROPE
MEM_HEADS = 4
MEM_HEAD_DIM = 128
N_MEM = 256
N_EXPERTS = 32
TOP_K = 4
D_EXPERT = 1024
SWIGLU_ALPHA = 1.702
SWIGLU_LIMIT = 7.0
ROPE_THETA = 10000.0
NORM_EPS = 1e-6
SUBLN_EPS = 1e-5
NEG_INF = -1e30
PAGE = 128

DIFF_SCALE = DIFF_HEAD_DIM ** -0.5
MLA_SCALE = MLA_QK ** -0.5
MEM_SCALE = MEM_HEAD_DIM ** -0.5

LANES = 128
MXU_DIM = 256
VMEM_LIMIT = 52 * 1024 * 1024
TOKEN_TILE = 256
ATTN_TILE = 256
PROMPT_Q_TILE = 256
PAGES_PER_STEP = 8
SEQS_PER_STEP = 4
PAGE_BUFFERS = 3
MOE_ROWS = 256
ROW_GROUP = 32

_OFF_Q, _OFF_K, _OFF_V, _OFF_CQ, _OFF_CKV, _OFF_KR, _OFF_MQ, _OFF_G = 0, 512, 768, 1024, 1280, 1408, 1440, 1952


def _params(sem):
    return pltpu.CompilerParams(dimension_semantics=sem, vmem_limit_bytes=VMEM_LIMIT)


def _full(a):
    nd = a.ndim
    return pl.BlockSpec(a.shape, lambda *_: (0,) * nd)


def _dot(a, b):
    return jnp.dot(a, b, preferred_element_type=F32)


def _dot_nt(a, b):
    return lax.dot_general(a, b, (((1,), (1,)), ((), ())), preferred_element_type=F32)


def _rms_rows(x, eps=NORM_EPS):
    return x * lax.rsqrt(jnp.mean(x * x, axis=-1, keepdims=True) + eps)


def _group_sumsq(v, g_ref):
    sq = (v * v).astype(BF16)
    parts = [_dot(sq[:, j * MXU_DIM:(j + 1) * MXU_DIM], g_ref[...]) for j in range(v.shape[1] // MXU_DIM)]
    return parts[0] if len(parts) == 1 else jnp.concatenate(parts, axis=-1)


def _rope_lanes(v, c, sa, sb, half):
    outs = []
    for j in range(v.shape[1] // LANES):
        b = v[:, j * LANES:(j + 1) * LANES]
        outs.append(b * c + pltpu.roll(b, LANES - half, 1) * sa + pltpu.roll(b, half, 1) * sb)
    return outs[0] if len(outs) == 1 else jnp.concatenate(outs, axis=-1)


def _inproj_kernel(x_ref, gmix_ref, wa_ref, wuq_ref, wk_ref, g64_ref, g128_ref,
                   gq_ref, gk_ref, gcq_ref, gckv_ref, gmq_ref, gkm_ref, gmemq_ref,
                   c64_ref, sa64_ref, sb64_ref, c32_ref, sa32_ref, sb32_ref,
                   dq_ref, dk32_ref, dk16_ref, dv32_ref, dv16_ref, mq_ref,
                   ckv32_ref, ckv16_ref, kr32_ref, kmla_ref, memq_ref, dvt_ref, ckvt_ref, dkt_ref):
    x = x_ref[...]
    h = (_rms_rows(x) * gmix_ref[...]).astype(BF16)
    z = _dot(h, wa_ref[...])
    c64, sa64, sb64 = c64_ref[...], sa64_ref[...], sb64_ref[...]
    c32, sa32, sb32 = c32_ref[...], sa32_ref[...], sb32_ref[...]

    zq = z[:, 0:512]
    qn = zq * lax.rsqrt(_group_sumsq(zq, g64_ref) * (1.0 / DIFF_HEAD_DIM) + NORM_EPS) * gq_ref[...]
    dq_ref[...] = _rope_lanes(qn, c64, sa64, sb64, 32).astype(BF16)

    zk = z[:, 512:768]
    kn = zk * lax.rsqrt(_group_sumsq(zk, g64_ref) * (1.0 / DIFF_HEAD_DIM) + NORM_EPS) * gk_ref[...]
    dk = _rope_lanes(kn, c64, sa64, sb64, 32)
    dk32_ref[...] = dk
    dkt_ref[...] = dk.T
    dk16_ref[...] = dk.astype(BF16)

    dv = z[:, 768:1024]
    dv32_ref[...] = dv
    dv16_ref[...] = dv.astype(BF16)
    dvt_ref[...] = dv.T.astype(BF16)

    cq = (_rms_rows(z[:, 1024:1280]) * gcq_ref[...]).astype(BF16)
    mqr = _dot(cq, wuq_ref[...])
    mqn = mqr * lax.rsqrt(_group_sumsq(mqr, g128_ref) * (1.0 / MLA_QK) + NORM_EPS) * gmq_ref[...]
    mq_ref[...] = _rope_lanes(mqn, c32, sa32, sb32, 16).astype(BF16)

    ckv = _rms_rows(z[:, 1280:1408]) * gckv_ref[...]
    ckv32_ref[...] = ckv
    ckv16 = ckv.astype(BF16)
    ckv16_ref[...] = ckv16
    ckvt_ref[...] = ckv.T.astype(BF16)

    krb = _rope_lanes(z[:, 1408:1536], c32, sa32, sb32, 16)
    kr32_ref[...] = krb[:, 64:96]
    kin = jnp.concatenate([ckv16, krb.astype(BF16)], axis=-1)
    kraw = _dot(kin, wk_ref[...])
    kmla_ref[...] = (kraw * lax.rsqrt(_group_sumsq(kraw, g128_ref) * (1.0 / MLA_QK) + NORM_EPS)
                     * gkm_ref[...]).astype(BF16)

    zm = z[:, 1536:2048]
    memq_ref[...] = (zm * lax.rsqrt(_group_sumsq(zm, g128_ref) * (1.0 / MEM_HEAD_DIM) + NORM_EPS)
                     * gmemq_ref[...]).astype(BF16)


def _inproj(x2d, tabs, wts, seq_len):
    n = x2d.shape[0]
    tm = min(TOKEN_TILE, n)
    period = tabs[0].shape[0] // tm
    per_seq = seq_len // tm
    row = lambda w: pl.BlockSpec((tm, w), lambda i: (i, 0))
    tab = pl.BlockSpec((tm, LANES), lambda i: (i % period, 0))
    out_w = [(512, BF16), (256, F32), (256, BF16), (256, F32), (256, BF16), (1024, BF16),
             (128, F32), (128, BF16), (32, F32), (1024, BF16), (512, BF16)]
    tile_t = lambda d: pl.BlockSpec((None, d, tm), lambda i: (i, 0, 0))
    return pl.pallas_call(
        _inproj_kernel,
        grid=(n // tm,),
        in_specs=[row(D_MODEL)] + [_full(a) for a in wts] + [tab] * 6,
        out_specs=[row(w) for w, _ in out_w] + [tile_t(256), tile_t(LANES),
                                                pl.BlockSpec((None, 256, tm), lambda i: (i // per_seq, 0, i % per_seq))],
        out_shape=[jax.ShapeDtypeStruct((n, w), dt) for w, dt in out_w]
                  + [jax.ShapeDtypeStruct((n // tm, 256, tm), BF16), jax.ShapeDtypeStruct((n // tm, LANES, tm), BF16),
                     jax.ShapeDtypeStruct((n // seq_len, 256, seq_len), F32)],
        compiler_params=_params(("parallel",)),
        name="inproj",
    )(x2d, *wts, *tabs)


def _memkv_kernel(x_ref, g_ref, w_ref, g128_ref, gk_ref, k32_ref, k16_ref, v32_ref, v16_ref):
    h = (_rms_rows(x_ref[...]) * g_ref[...]).astype(BF16)
    kv = _dot(h, w_ref[...])
    k = kv[:, 0:512]
    k = k * lax.rsqrt(_group_sumsq(k, g128_ref) * (1.0 / MEM_HEAD_DIM) + NORM_EPS) * gk_ref[...]
    v = kv[:, 512:1024]
    k32_ref[...] = k
    k16_ref[...] = k.astype(BF16)
    v32_ref[...] = v
    v16_ref[...] = v.astype(BF16)


def _memkv(mem2d, g, w, g128, gk):
    n = mem2d.shape[0]
    tm = min(TOKEN_TILE, n)
    row = lambda w_: pl.BlockSpec((tm, w_), lambda i: (i, 0))
    return pl.pallas_call(
        _memkv_kernel,
        grid=(n // tm,),
        in_specs=[row(D_MODEL), _full(g), _full(w), _full(g128), _full(gk)],
        out_specs=[row(512)] * 4,
        out_shape=[jax.ShapeDtypeStruct((n, 512), dt) for dt in (F32, BF16, F32, BF16)],
        compiler_params=_params(("parallel",)),
        name="memkv",
    )(mem2d, g, w, g128, gk)


def _online_update(s, m_ref, l_ref, rows=None):
    sl = slice(None) if rows is None else rows
    m_old = m_ref[sl, :]
    m_new = jnp.maximum(m_old, jnp.max(s, axis=-1, keepdims=True))
    alpha = jnp.exp(m_old - m_new)
    p = jnp.exp(s - m_new)
    l_ref[sl, :] = alpha * l_ref[sl, :] + jnp.sum(p, axis=-1, keepdims=True)
    m_ref[sl, :] = m_new
    return p, alpha


def _cols_softmax_step(s, m_ref, l_ref, cols):
    m_old = m_ref[:, cols]
    m_new = jnp.maximum(m_old, jnp.max(s, axis=0, keepdims=True))
    alpha = jnp.exp(m_old - m_new)
    p = jnp.exp(s - m_new)
    l_ref[:, cols] = alpha * l_ref[:, cols] + jnp.sum(p, axis=0, keepdims=True)
    m_ref[:, cols] = m_new
    return p, alpha


def _causal_cols(s, tq, delta):
    row = lax.broadcasted_iota(jnp.int32, s.shape, 0)
    col = lax.broadcasted_iota(jnp.int32, s.shape, 1) & (tq - 1)
    return jnp.where(row <= col + delta, s, NEG_INF)


def _cols_attention(i, tq, tk, n_cols, score_fn, vt_ref, m_ref, l_ref, acc_ref):
    m_ref[...] = jnp.full(m_ref.shape, NEG_INF, F32)
    l_ref[...] = jnp.zeros(l_ref.shape, F32)
    acc_ref[...] = jnp.zeros(acc_ref.shape, F32)
    j_last = (i * tq) // tk
    delta = i * tq - j_last * tk

    n_chunks = n_cols // MXU_DIM

    def scores(j):
        return tuple(score_fn(j, cc) for cc in range(n_chunks))

    def finish(j, s_all, masked):
        vt = vt_ref[j]
        for cc in range(n_chunks):
            cols = slice(cc * MXU_DIM, (cc + 1) * MXU_DIM)
            s = _causal_cols(s_all[cc], tq, delta) if masked else s_all[cc]
            p, alpha = _cols_softmax_step(s, m_ref, l_ref, cols)
            acc_ref[:, cols] = alpha * acc_ref[:, cols] + _dot(vt, p.astype(BF16))

    def body(j, s_cur):
        s_next = scores(j + 1)
        finish(j, s_cur, False)
        return s_next

    s_last = lax.fori_loop(0, j_last, body, scores(0))
    finish(j_last, s_last, True)


def _diff_prompt_kernel(lam_ref, q_ref, k_ref, vt_ref, gs_ref, o_ref, qt_ref, m_ref, l_ref, acc_ref, *, tq, tk):
    i = pl.program_id(2)
    dim = lax.broadcasted_iota(jnp.int32, (LANES, tq), 0)
    for g in range(2):
        qgt = q_ref[:, g * LANES:(g + 1) * LANES].astype(F32).T
        qt_ref[:, g * tq:(g + 1) * tq] = jnp.where(dim < DIFF_HEAD_DIM, qgt, 0.0).astype(BF16)
        qt_ref[:, (2 + g) * tq:(3 + g) * tq] = jnp.where(dim >= DIFF_HEAD_DIM, qgt, 0.0).astype(BF16)

    def scores(j, cc):
        off = pl.multiple_of(j * tk, tk)
        return _dot(k_ref[pl.ds(off, tk), :], qt_ref[:, cc * MXU_DIM:(cc + 1) * MXU_DIM])

    _cols_attention(i, tq, tk, 4 * tq, scores, vt_ref, m_ref, l_ref, acc_ref)

    o = acc_ref[...] / l_ref[...]
    odt = o[:, 0:2 * tq] - lam_ref[0] * o[:, 2 * tq:4 * tq]
    for g in range(2):
        od = odt[:, g * tq:(g + 1) * tq].T
        od = _rms_rows(od, SUBLN_EPS) * gs_ref[...]
        o_ref[:, g * LANES:(g + 1) * LANES] = od.astype(BF16)


def _diff_prompt(lam, dq, dk16, dvt, gs):
    b, t, _ = dq.shape
    tq, tk = PROMPT_Q_TILE, dvt.shape[-1]
    kern = functools.partial(_diff_prompt_kernel, tq=tq, tk=tk)
    return pl.pallas_call(
        kern,
        grid=(b, 2, t // tq),
        in_specs=[pl.BlockSpec(memory_space=pltpu.SMEM),
                  pl.BlockSpec((None, tq, 256), lambda bi, n, i: (bi, i, n)),
                  pl.BlockSpec((None, t, LANES), lambda bi, n, i: (bi, 0, n)),
                  pl.BlockSpec((None, t // tk, LANES, tk), lambda bi, n, i: (bi, 0, n, 0)),
                  pl.BlockSpec((1, LANES), lambda bi, n, i: (0, 0))],
        out_specs=pl.BlockSpec((None, tq, 256), lambda bi, n, i: (bi, i, n)),
        out_shape=jax.ShapeDtypeStruct((b, t, 512), BF16),
        scratch_shapes=[pltpu.VMEM((LANES, 4 * tq), BF16), pltpu.VMEM((1, 4 * tq), F32),
                        pltpu.VMEM((1, 4 * tq), F32), pltpu.VMEM((LANES, 4 * tq), F32)],
        compiler_params=_params(("parallel", "parallel", "parallel")),
        name="diff_prompt",
    )(lam, dq, dk16, dvt.reshape(b, t // tk, 256, tk), gs)


def _mla_prompt_kernel(q_ref, k_ref, ct_ref, wuv_ref, o_ref, qt_ref, m_ref, l_ref, acc_ref, *, tq, tk):
    i = pl.program_id(1)
    for h in range(MLA_HEADS):
        qt_ref[:, h * tq:(h + 1) * tq] = q_ref[:, h * LANES:(h + 1) * LANES].astype(F32).T.astype(BF16)

    def scores(j, h):
        off = pl.multiple_of(j * tk, tk)
        return _dot(k_ref[pl.ds(off, tk), h * LANES:(h + 1) * LANES], qt_ref[:, h * tq:(h + 1) * tq])

    _cols_attention(i, tq, tk, MLA_HEADS * tq, scores, ct_ref, m_ref, l_ref, acc_ref)

    lat_t = acc_ref[...] / l_ref[...]
    o = None
    for h in range(MLA_HEADS):
        part = _dot(lat_t[:, h * tq:(h + 1) * tq].T.astype(BF16), wuv_ref[h])
        o = part if o is None else o + part
    o_ref[...] = o.astype(BF16)


def _mla_prompt(mq, kmla, ckvt, wuv2):
    b, t, _ = mq.shape
    tq, tk = PROMPT_Q_TILE, ckvt.shape[-1]
    assert tq == MXU_DIM, "one head per MXU-wide column chunk"
    kern = functools.partial(_mla_prompt_kernel, tq=tq, tk=tk)
    cols = MLA_HEADS * tq
    return pl.pallas_call(
        kern,
        grid=(b, t // tq),
        in_specs=[pl.BlockSpec((None, tq, 1024), lambda bi, i: (bi, i, 0)),
                  pl.BlockSpec((None, t, 1024), lambda bi, i: (bi, 0, 0)),
                  pl.BlockSpec((None, t // tk, LANES, tk), lambda bi, i: (bi, 0, 0, 0)),
                  pl.BlockSpec(wuv2.shape, lambda bi, i: (0, 0, 0))],
        out_specs=pl.BlockSpec((None, tq, 512), lambda bi, i: (bi, i, 0)),
        out_shape=jax.ShapeDtypeStruct((b, t, 512), BF16),
        scratch_shapes=[pltpu.VMEM((LANES, cols), BF16), pltpu.VMEM((1, cols), F32),
                        pltpu.VMEM((1, cols), F32), pltpu.VMEM((LANES, cols), F32)],
        compiler_params=_params(("parallel", "parallel")),
        name="mla_prompt",
    )(mq, kmla, ckvt.reshape(b, t // tk, LANES, tk), wuv2)


def _softmax_pv(s, v):
    m = jnp.max(s, axis=-1, keepdims=True)
    p = jnp.exp(s - m)
    l = jnp.sum(p, axis=-1, keepdims=True)
    return _dot(p.astype(BF16), v) / l


def _mem_prompt_kernel(q_ref, k_ref, v_ref, o_ref):
    for h in range(MEM_HEADS):
        sl = slice(h * LANES, (h + 1) * LANES)
        s = _dot_nt(q_ref[:, sl], k_ref[:, sl])
        o_ref[:, sl] = _softmax_pv(s, v_ref[:, sl]).astype(BF16)


def _mem_prompt(memq, mk16, mv16):
    b, t, _ = memq.shape
    tq = min(2 * ATTN_TILE, t)
    return pl.pallas_call(
        _mem_prompt_kernel,
        grid=(b, t // tq),
        in_specs=[pl.BlockSpec((None, tq, 512), lambda bi, i: (bi, i, 0)),
                  pl.BlockSpec((None, N_MEM, 512), lambda bi, i: (bi, 0, 0)),
                  pl.BlockSpec((None, N_MEM, 512), lambda bi, i: (bi, 0, 0))],
        out_specs=pl.BlockSpec((None, tq, 512), lambda bi, i: (bi, i, 0)),
        out_shape=jax.ShapeDtypeStruct((b, t, 512), BF16),
        compiler_params=_params(("parallel", "parallel")),
        name="mem_prompt",
    )(memq, mk16, mv16)


def _mem_sample_kernel(q_ref, k_ref, v_ref, o_ref, *, bb):
    heads = [(bi, h) for bi in range(bb) for h in range(MEM_HEADS)]
    scores = []
    for bi, h in heads:
        k = k_ref[bi, pl.ds(h, N_MEM, stride=MEM_HEADS), :].astype(BF16)
        scores.append(_dot_nt(q_ref[bi, :, h * LANES:(h + 1) * LANES], k))
    for (bi, h), s in zip(heads, scores):
        v = v_ref[bi, pl.ds(h, N_MEM, stride=MEM_HEADS), :].astype(BF16)
        o_ref[bi, :, h * LANES:(h + 1) * LANES] = _softmax_pv(s, v).astype(BF16)


def _mem_sample(memq, ck, cv, b_off):
    b, t, _ = memq.shape
    bb = math.gcd(b, 4)
    off = b_off // bb
    kern = functools.partial(_mem_sample_kernel, bb=bb)
    rows = N_MEM * MEM_HEADS
    return pl.pallas_call(
        kern,
        grid=(b // bb,),
        in_specs=[pl.BlockSpec((bb, t, 512), lambda i: (i, 0, 0)),
                  pl.BlockSpec((bb, rows, LANES), lambda i: (i + off, 0, 0)),
                  pl.BlockSpec((bb, rows, LANES), lambda i: (i + off, 0, 0))],
        out_specs=pl.BlockSpec((bb, t, 512), lambda i: (i, 0, 0)),
        out_shape=jax.ShapeDtypeStruct((b, t, 512), BF16),
        compiler_params=_params(("parallel",)),
        name="mem_sample",
    )(memq, ck, cv)


def _page_copies(pt_ref, grp, j, slot, bb, ch, nc, streams):
    cps = []
    for bi in range(bb):
        base = ((grp * bb + bi) * nc + j) * ch
        for r in range(ch):
            pg = pt_ref[base + r]
            for src, buf, sem in streams:
                cps.append(pltpu.make_async_copy(src.at[pg], buf.at[slot, bi * ch + r], sem.at[slot]))
    return cps


def _stream_begin(pt_ref, grp, j, bb, ch, nc, total, streams):
    d = grp * nc + j

    @pl.when(d == 0)
    def _():
        for ahead in range(min(PAGE_BUFFERS - 1, total)):
            for cp in _page_copies(pt_ref, ahead // nc, ahead % nc, ahead, bb, ch, nc, streams):
                cp.start()

    for cp in _page_copies(pt_ref, grp, j, lax.rem(d, PAGE_BUFFERS), bb, ch, nc, streams):
        cp.wait()


def _stream_end(pt_ref, d, bb, ch, nc, total, streams):
    nxt = d + (PAGE_BUFFERS - 1)

    @pl.when(nxt < total)
    def _():
        for cp in _page_copies(pt_ref, lax.div(nxt, nc), lax.rem(nxt, nc), lax.rem(nxt, PAGE_BUFFERS),
                               bb, ch, nc, streams):
            cp.start()


def _causal_new(s2, nq):
    qpos = lax.broadcasted_iota(jnp.int32, s2.shape, 0) & (nq - 1)
    kpos = lax.broadcasted_iota(jnp.int32, s2.shape, 1)
    return jnp.where(kpos <= qpos, s2, NEG_INF)


def _decode_kernel(pt_ref, lam_ref, qd_ref, kn_ref, vn_ref, gs_ref, mq_ref, kmn_ref, cn_ref, gk_ref, aq_ref,
                   wukt_ref, wuv_ref, kc_hbm, vc_hbm, cc_hbm, rc_hbm, od_ref, om_ref,
                   kbuf, vbuf, cbuf, rbuf, sem, md_ref, ld_ref, accd_ref, lw_ref, qr_ref, mm_ref, lm_ref, accm_ref,
                   sd_a, sd_b, sm_a, sm_b, *, bb, ch, nc, total):
    grp, j = pl.program_id(0), pl.program_id(1)
    d = grp * nc + j
    streams = [(kc_hbm, kbuf, sem.at[0]), (vc_hbm, vbuf, sem.at[1]), (cc_hbm, cbuf, sem.at[2]), (rc_hbm, rbuf, sem.at[3])]
    score_bufs = ((sd_a, sm_a), (sd_b, sm_b))
    nq = mq_ref.shape[1]
    nr = MLA_HEADS * nq
    nk = MLA_HEADS * MLA_NOPE

    def init():
        for ref in (md_ref, mm_ref):
            ref[...] = jnp.full(ref.shape, NEG_INF, F32)
        for ref in (ld_ref, lm_ref, accd_ref, accm_ref):
            ref[...] = jnp.zeros(ref.shape, F32)
        for bi in range(bb):
            lw_ref[bi, 0:nk, :] = wukt_ref[...]
            qg = (mq_ref[bi].astype(F32) * gk_ref[...]).astype(BF16)
            for h in range(MLA_HEADS):
                qa = _dot(qg[:, h * LANES:(h + 1) * LANES], aq_ref[h])
                lw_ref[bi, nk + h * nq:nk + (h + 1) * nq, :] = qa[:, 0:LANES].astype(BF16)
                qr_ref[bi, h * nq:(h + 1) * nq, :] = qa[:, LANES + 64:LANES + 96].astype(BF16)

    def score_phase(slot, sd_ref, sm_ref, seqs=range(bb)):
        for bi in seqs:
            pages = range(bi * ch, (bi + 1) * ch)
            kt = jnp.concatenate([kbuf[slot, r].astype(BF16) for r in pages], axis=-1)
            sd_ref[bi] = _dot(qd_ref[bi], kt)
            c = jnp.concatenate([cbuf[slot, r] for r in pages], axis=0).astype(BF16)
            krt = jnp.concatenate([rbuf[slot, r] for r in pages], axis=-1)
            big = _dot_nt(lw_ref[bi], c)
            nkeys = big.shape[1]
            knt = big[0:nk]
            ssq = jnp.sum((knt * knt).reshape(MLA_HEADS, MLA_NOPE, nkeys), axis=1)
            ssq = ssq + jnp.sum(krt * krt, axis=0, keepdims=True)
            rn = lax.rsqrt(ssq * (1.0 / MLA_QK) + NORM_EPS)
            sm = big[nk:nk + nr] + _dot(qr_ref[bi], krt.astype(BF16))
            sm_ref[bi] = (sm.reshape(MLA_HEADS, nq, nkeys) * rn[:, None, :]).reshape(nr, nkeys)

    def value_phase(slot, sd_ref, sm_ref, seqs=range(bb)):
        for bi in seqs:
            pages = range(bi * ch, (bi + 1) * ch)
            p, alpha = _online_update(sd_ref[bi], md_ref.at[bi], ld_ref.at[bi])
            p = p.astype(BF16)
            pv = []
            for n in range(2):
                v = jnp.concatenate([vbuf[slot, r, pl.ds(n, PAGE, stride=2), :].astype(BF16) for r in pages], axis=0)
                pv.append(_dot(p[n * 32:(n + 1) * 32], v))
            accd_ref[bi] = alpha * accd_ref[bi] + jnp.concatenate(pv, axis=0)
            c = jnp.concatenate([cbuf[slot, r] for r in pages], axis=0).astype(BF16)
            pm, alpham = _online_update(sm_ref[bi], mm_ref.at[bi], lm_ref.at[bi])
            accm_ref[bi] = alpham * accm_ref[bi] + _dot(pm.astype(BF16), c)

    cur_slot = lax.rem(d, PAGE_BUFFERS)
    prev_slot = lax.rem(d + (PAGE_BUFFERS - 1), PAGE_BUFFERS)

    @pl.when(j == 0)
    def _():
        _stream_begin(pt_ref, grp, j, bb, ch, nc, total, streams)
        init()
        score_phase(cur_slot, *score_bufs[0])

    for parity in range(2):
        @pl.when((j > 0) & (j < nc) & ((j & 1) == parity))
        def _():
            _stream_begin(pt_ref, grp, j, bb, ch, nc, total, streams)
            score_phase(cur_slot, *score_bufs[parity])
            value_phase(prev_slot, *score_bufs[1 - parity])

    @pl.when(j < nc)
    def _():
        _stream_end(pt_ref, d, bb, ch, nc, total, streams)

    @pl.when(j == nc)
    def _():
        value_phase(prev_slot, *score_bufs[(nc - 1) % 2])
        for bi in range(bb):
            q = qd_ref[bi]
            vn = vn_ref[bi]
            p2, alpha2 = _online_update(_causal_new(_dot_nt(q, kn_ref[bi]), nq), md_ref.at[bi], ld_ref.at[bi])
            p2 = p2.astype(BF16)
            pv2 = [_dot(p2[n * 32:(n + 1) * 32], vn[:, n * LANES:(n + 1) * LANES]) for n in range(2)]
            o = (alpha2 * accd_ref[bi] + jnp.concatenate(pv2, axis=0)) / ld_ref[bi]
            for n in range(2):
                on = o[n * 32:(n + 1) * 32]
                od = on[0:16] - lam_ref[0] * on[16:32]
                od = _rms_rows(od, SUBLN_EPS) * gs_ref[...]
                for g in range(2):
                    hh = n * 2 + g
                    od_ref[bi, :, hh * LANES:(hh + 1) * LANES] = od[g * nq:(g + 1) * nq].astype(BF16)

            qf = mq_ref[bi]
            kn = kmn_ref[bi]
            s2 = jnp.concatenate([_dot_nt(qf[:, h * LANES:(h + 1) * LANES], kn[:, h * LANES:(h + 1) * LANES])
                                  for h in range(MLA_HEADS)], axis=0)
            p3, alpha3 = _online_update(_causal_new(s2, nq), mm_ref.at[bi], lm_ref.at[bi])
            lat = ((alpha3 * accm_ref[bi] + _dot(p3.astype(BF16), cn_ref[bi])) / lm_ref[bi]).astype(BF16)
            om = _dot(lat[0:nq], wuv_ref[0])
            for h in range(1, MLA_HEADS):
                om = om + _dot(lat[h * nq:(h + 1) * nq], wuv_ref[h])
            om_ref[bi] = om.astype(BF16)


def _decode(pt_flat, lam, qbd, kn16, vn16, gs, mq, kmla, ckv16, gk_pad, aq, wukt, wuv2, kc, vc, cc, rc, n_pages):
    b, nq, _ = mq.shape
    assert nq == 8, "score-row layouts assume 8 new tokens per sequence"
    ch = math.gcd(PAGES_PER_STEP, n_pages)
    bb = math.gcd(SEQS_PER_STEP, b)
    nc = n_pages // ch
    kern = functools.partial(_decode_kernel, bb=bb, ch=ch, nc=nc, total=(b // bb) * nc)
    nr = MLA_HEADS * nq
    per_seq = lambda w, r=nq: pl.BlockSpec((bb, r, w), lambda g, j, pt: (g, 0, 0))
    const = lambda a: pl.BlockSpec(a.shape, lambda g, j, pt: (0,) * a.ndim)
    grid_spec = pltpu.PrefetchScalarGridSpec(
        num_scalar_prefetch=1,
        grid=(b // bb, nc + 1),
        in_specs=[pl.BlockSpec(memory_space=pltpu.SMEM), per_seq(256, 64), per_seq(256), per_seq(256), const(gs),
                  per_seq(1024), per_seq(1024), per_seq(LANES), const(gk_pad), const(aq), const(wukt), const(wuv2)]
                 + [pl.BlockSpec(memory_space=pl.ANY)] * 4,
        out_specs=[per_seq(512), per_seq(512)],
        scratch_shapes=[pltpu.VMEM((PAGE_BUFFERS, bb * ch, 256, LANES), F32),
                        pltpu.VMEM((PAGE_BUFFERS, bb * ch, 256, LANES), F32),
                        pltpu.VMEM((PAGE_BUFFERS, bb * ch, PAGE, LANES), F32),
                        pltpu.VMEM((PAGE_BUFFERS, bb * ch, MLA_ROPE, PAGE), F32),
                        pltpu.SemaphoreType.DMA((4, PAGE_BUFFERS)),
                        pltpu.VMEM((bb, 64, 1), F32), pltpu.VMEM((bb, 64, 1), F32), pltpu.VMEM((bb, 64, LANES), F32),
                        pltpu.VMEM((bb, MLA_HEADS * MLA_NOPE + nr, LANES), BF16), pltpu.VMEM((bb, nr, MLA_ROPE), BF16),
                        pltpu.VMEM((bb, nr, 1), F32), pltpu.VMEM((bb, nr, 1), F32), pltpu.VMEM((bb, nr, LANES), F32)]
                       + [pltpu.VMEM((bb, 64, ch * PAGE), F32)] * 2 + [pltpu.VMEM((bb, nr, ch * PAGE), F32)] * 2,
    )
    return pl.pallas_call(
        kern,
        grid_spec=grid_spec,
        out_shape=[jax.ShapeDtypeStruct((b, nq, 512), BF16)] * 2,
        compiler_params=_params(("arbitrary", "arbitrary")),
        name="decode",
    )(pt_flat, lam, qbd, kn16, vn16, gs, mq, kmla, ckv16, gk_pad, aq, wukt, wuv2, kc, vc, cc, rc)


def _merge_kernel(xp_ref, odp_ref, omp_ref, ocp_ref, xs_ref, ods_ref, oms_ref, ocs_ref,
                  gmix_ref, wg_ref, bg_ref, wbr_ref, wout_ref, gffn_ref,
                  wrh_ref, wrl_ref, br_ref, xm_ref, h2_ref, tv_ref, ti_ref, *, p_tiles):
    is_p = pl.program_id(0) < p_tiles
    pick = lambda a_ref, b_ref: jnp.where(is_p, a_ref[...], b_ref[...])
    x = pick(xp_ref, xs_ref)
    h = (_rms_rows(x) * gmix_ref[...]).astype(BF16)
    gates = jax.nn.sigmoid(_dot(h, wg_ref[...]) + bg_ref[...])
    merged = gates[:, 0:D_MODEL] * _dot(pick(odp_ref, ods_ref), wbr_ref[0])
    merged = merged + gates[:, D_MODEL:2 * D_MODEL] * _dot(pick(omp_ref, oms_ref), wbr_ref[1])
    merged = merged + gates[:, 2 * D_MODEL:3 * D_MODEL] * _dot(pick(ocp_ref, ocs_ref), wbr_ref[2])
    xm = x + _dot(merged.astype(BF16), wout_ref[...])
    xm_ref[...] = xm
    h2 = _rms_rows(xm) * gffn_ref[...]
    h2_ref[...] = pltpu.einshape("stl->tsl", jnp.stack([h2[:, s * LANES:(s + 1) * LANES]
                                                        for s in range(D_MODEL // LANES)], axis=0))
    hh = h2.astype(BF16)
    hl = (h2 - hh.astype(F32)).astype(BF16)
    logits = _dot(hh, wrh_ref[...]) + _dot(hl, wrh_ref[...]) + _dot(hh, wrl_ref[...]) + br_ref[...]
    lane = lax.broadcasted_iota(jnp.int32, logits.shape, 1)
    logits = jnp.where(lane < N_EXPERTS, logits, -jnp.inf)
    tv = jnp.zeros(logits.shape, F32)
    ti = jnp.zeros(logits.shape, jnp.int32)
    vals = []
    for k in range(TOP_K):
        mx = jnp.max(logits, axis=-1, keepdims=True)
        idx = jnp.min(jnp.where(logits == mx, lane, LANES), axis=-1, keepdims=True)
        vals.append(mx)
        ti = jnp.where(lane == k, idx, ti)
        logits = jnp.where(lane == idx, -jnp.inf, logits)
    es = [jnp.exp(v - vals[0]) for v in vals]
    den = es[0] + es[1] + es[2] + es[3]
    for k in range(TOP_K):
        tv = jnp.where(lane == k, es[k] / den, tv)
    tv_ref[...] = tv
    ti_ref[...] = ti


def _merge(group_p, group_s, wts):
    n_p, n_s = group_p[0].shape[0], group_s[0].shape[0]
    tm = math.gcd(TOKEN_TILE, math.gcd(n_p, n_s))
    p_tiles, n = n_p // tm, n_p + n_s
    row = lambda w: pl.BlockSpec((tm, w), lambda i: (i, 0))
    row_p = lambda w: pl.BlockSpec((tm, w), lambda i: (jnp.minimum(i, p_tiles - 1), 0))
    row_s = lambda w: pl.BlockSpec((tm, w), lambda i: (jnp.maximum(i - p_tiles, 0), 0))
    widths = (D_MODEL, 512, 512, 512)
    return pl.pallas_call(
        functools.partial(_merge_kernel, p_tiles=p_tiles),
        grid=(n // tm,),
        in_specs=[row_p(w) for w in widths] + [row_s(w) for w in widths] + [_full(a) for a in wts],
        out_specs=[row(D_MODEL), pl.BlockSpec((tm, 8, LANES), lambda i: (i, 0, 0)), row(LANES), row(LANES)],
        out_shape=[jax.ShapeDtypeStruct((n, D_MODEL), F32), jax.ShapeDtypeStruct((n, 8, LANES), F32),
                   jax.ShapeDtypeStruct((n, LANES), F32), jax.ShapeDtypeStruct((n, LANES), jnp.int32)],
        compiler_params=_params(("parallel",)),
        name="merge",
    )(*group_p, *group_s, *wts)


def _expert_kernel(be_ref, first_ref, nval_ref, slot_ref, wgu_ref, bgu_ref, wd_ref, bd_ref, x_hbm, y_hbm,
                   slot_smem, xbuf, ybuf, zbuf, wgu16, wd16, gsem, ssem, isem, *, n_blocks, n_real):
    i = pl.program_id(0)
    nv = nval_ref[i]
    groups = lambda k: lax.shift_right_logical(nval_ref[k] + (ROW_GROUP - 1), ROW_GROUP.bit_length() - 1)

    def idx_copy(k):
        ring = lax.rem(k, 3)
        return pltpu.make_async_copy(slot_ref.at[k], slot_smem.at[ring], isem.at[ring])

    halves = slot_smem.shape[1] // 2
    bm = halves * LANES

    def for_groups(k, fn):
        used = groups(k)
        for g in range(bm // ROW_GROUP):
            pl.when(g < used)(functools.partial(fn, g * ROW_GROUP))

    def issue_gather(k):
        ring, xb = lax.rem(k, 3), k & 1

        def rows(r0):
            for r in range(r0, r0 + ROW_GROUP):
                tok = slot_smem[ring, halves + r // LANES, r % LANES]
                pltpu.make_async_copy(x_hbm.at[tok], xbuf.at[xb, r], gsem.at[xb]).start()

        for_groups(k, rows)

    def wait_gather(k):
        xb = k & 1
        for_groups(k, lambda r0: pltpu.make_async_copy(
            x_hbm.at[pl.ds(0, ROW_GROUP)], xbuf.at[xb, pl.ds(r0, ROW_GROUP)], gsem.at[xb]).wait())

    def issue_scatter(k):
        ring = lax.rem(k, 3)

        def rows(r0):
            for r in range(r0, r0 + ROW_GROUP):
                slot = slot_smem[ring, r // LANES, r % LANES]
                pltpu.make_async_copy(ybuf.at[r], y_hbm.at[slot], ssem.at[0]).start()

        for_groups(k, rows)

    def wait_scatter(k):
        for_groups(k, lambda r0: pltpu.make_async_copy(
            ybuf.at[pl.ds(r0, ROW_GROUP)], y_hbm.at[pl.ds(0, ROW_GROUP)], ssem.at[0]).wait())

    @pl.when(i == 0)
    def _():
        xbuf[...] = jnp.zeros(xbuf.shape, F32)
        zbuf[...] = jnp.zeros(zbuf.shape, F32)
        spare = pltpu.make_async_copy(zbuf, y_hbm.at[pl.ds(n_real, ROW_GROUP)], ssem.at[0])
        spare.start()
        spare.wait()
        first_idx = idx_copy(0)
        first_idx.start()
        first_idx.wait()
        issue_gather(0)
        if n_blocks > 1:
            idx_copy(1).start()

    @pl.when(i + 1 < n_blocks)
    def _():
        idx_copy(i + 1).wait()
        issue_gather(i + 1)

    @pl.when(i + 2 < n_blocks)
    def _():
        idx_copy(i + 2).start()

    @pl.when(nv > 0)
    def _():
        @pl.when(first_ref[i] == 1)
        def _():
            wgu16[...] = wgu_ref[...].astype(BF16)
            wd16[...] = wd_ref[...].astype(BF16)

        wait_gather(i)
        xb = i & 1
        xt = pltpu.einshape("rsl->srl", xbuf[xb])
        x = jnp.concatenate([xt[s] for s in range(D_MODEL // LANES)], axis=-1).astype(BF16)
        n_chunks = D_EXPERT // MXU_DIM

        def gate_up(c):
            g0, u0 = c * MXU_DIM, D_EXPERT + c * MXU_DIM
            return (_dot(x, wgu16[:, g0:g0 + MXU_DIM]) + bgu_ref[:, g0:g0 + MXU_DIM],
                    _dot(x, wgu16[:, u0:u0 + MXU_DIM]) + bgu_ref[:, u0:u0 + MXU_DIM])

        y = jnp.broadcast_to(bd_ref[...], (bm, D_MODEL))
        nxt = gate_up(0)
        for c in range(n_chunks):
            gate, up = nxt
            if c + 1 < n_chunks:
                nxt = gate_up(c + 1)
            gate = jnp.minimum(gate, SWIGLU_LIMIT)
            up = jnp.clip(up, -SWIGLU_LIMIT, SWIGLU_LIMIT)
            act = (up + 1.0) * gate * jax.nn.sigmoid(SWIGLU_ALPHA * gate)
            y = y + _dot(act.astype(BF16), wd16[c * MXU_DIM:(c + 1) * MXU_DIM, :])

        @pl.when(i > 0)
        def _():
            wait_scatter(i - 1)

        ybuf[...] = pltpu.einshape("srl->rsl", jnp.stack([y[:, s * LANES:(s + 1) * LANES]
                                                          for s in range(D_MODEL // LANES)], axis=0))
        issue_scatter(i)

    @pl.when((nv == 0) & (i > 0))
    def _():
        wait_scatter(i - 1)

    @pl.when(i == n_blocks - 1)
    def _():
        wait_scatter(i)


def _experts(block_e, first, nval, row_slot3, wgu, bgu, wd, bd, x3, n_real):
    n_blocks = row_slot3.shape[0]
    bm = row_slot3.shape[1] // 2 * LANES
    n_slots = n_real + ROW_GROUP
    kern = functools.partial(_expert_kernel, n_blocks=n_blocks, n_real=n_real)
    grid_spec = pltpu.PrefetchScalarGridSpec(
        num_scalar_prefetch=3,
        grid=(n_blocks,),
        in_specs=[pl.BlockSpec(row_slot3.shape, lambda i, be, fi, na: (0, 0, 0)),
                  pl.BlockSpec((None, D_MODEL, 2 * D_EXPERT), lambda i, be, fi, na: (be[i], 0, 0)),
                  pl.BlockSpec((None, 1, 2 * D_EXPERT), lambda i, be, fi, na: (be[i], 0, 0)),
                  pl.BlockSpec((None, D_EXPERT, D_MODEL), lambda i, be, fi, na: (be[i], 0, 0)),
                  pl.BlockSpec((None, 1, D_MODEL), lambda i, be, fi, na: (be[i], 0, 0)),
                  pl.BlockSpec(memory_space=pl.ANY)],
        out_specs=pl.BlockSpec(memory_space=pl.ANY),
        scratch_shapes=[pltpu.SMEM((3,) + row_slot3.shape[1:], jnp.int32),
                        pltpu.VMEM((2, bm, 8, LANES), F32), pltpu.VMEM((bm, 8, LANES), F32),
                        pltpu.VMEM((ROW_GROUP, 8, LANES), F32),
                        pltpu.VMEM((D_MODEL, 2 * D_EXPERT), BF16), pltpu.VMEM((D_EXPERT, D_MODEL), BF16),
                        pltpu.SemaphoreType.DMA((2,)), pltpu.SemaphoreType.DMA((1,)), pltpu.SemaphoreType.DMA((3,))],
    )
    return pl.pallas_call(
        kern,
        grid_spec=grid_spec,
        out_shape=jax.ShapeDtypeStruct((n_slots, 8, LANES), F32),
        compiler_params=_params(("arbitrary",)),
        name="experts",
    )(block_e, first, nval, row_slot3, wgu, bgu, wd, bd, x3)


def _combine_kernel(xm_ref, tv_ref, y_ref, o_ref):
    tv = tv_ref[...]
    gate = [jnp.broadcast_to(tv[:, k:k + 1], (tv.shape[0], LANES)) for k in range(TOP_K)]
    y = pltpu.einshape("tjl->jtl", y_ref[...])
    for s in range(D_MODEL // LANES):
        acc = xm_ref[:, s * LANES:(s + 1) * LANES]
        for k in range(TOP_K):
            acc = acc + gate[k] * y[k * 8 + s]
        o_ref[:, s * LANES:(s + 1) * LANES] = acc


def _combine(xm, tv, yslots, tok_off, n):
    tm = math.gcd(TOKEN_TILE, math.gcd(n, tok_off)) if tok_off else min(TOKEN_TILE, n)
    off = tok_off // tm
    y4 = yslots.reshape(yslots.shape[0] // TOP_K, TOP_K * 8, LANES)
    return pl.pallas_call(
        _combine_kernel,
        grid=(n // tm,),
        in_specs=[pl.BlockSpec((tm, D_MODEL), lambda i: (i + off, 0)),
                  pl.BlockSpec((tm, LANES), lambda i: (i + off, 0)),
                  pl.BlockSpec((tm, TOP_K * 8, LANES), lambda i: (i + off, 0, 0))],
        out_specs=pl.BlockSpec((tm, D_MODEL), lambda i: (i, 0)),
        out_shape=jax.ShapeDtypeStruct((n, D_MODEL), F32),
        compiler_params=_params(("parallel",)),
        name="combine",
    )(xm, tv, y4)


def _rope_tables(pos, rows):
    pos = pos.astype(F32)[:, None]
    lane = jnp.arange(LANES)

    def ang(half):
        inv = jnp.power(ROPE_THETA, -jnp.arange(half, dtype=F32) / half)
        return pos * inv[None, :]

    a64 = ang(32)[:, lane % 32]
    first = (lane % 64) < 32
    c64 = jnp.cos(a64)
    sa64 = jnp.where(first, -jnp.sin(a64), 0.0)
    sb64 = jnp.where(first, 0.0, jnp.sin(a64))
    a32 = ang(16)[:, lane % 16]
    in_a = (lane >= 64) & (lane < 80)
    in_b = (lane >= 80) & (lane < 96)
    c32 = jnp.where(in_a | in_b, jnp.cos(a32), 1.0)
    sa32 = jnp.where(in_a, -jnp.sin(a32), 0.0)
    sb32 = jnp.where(in_b, jnp.sin(a32), 0.0)
    tabs = [c64, sa64, sb64, c32, sa32, sb32]
    reps = rows // pos.shape[0]
    return [jnp.tile(t, (reps, 1)) if reps > 1 else t for t in tabs]


def _tied_pad(g, scale):
    blk = jnp.concatenate([g[:MLA_NOPE], g[MLA_NOPE:], g[MLA_NOPE:], jnp.zeros((32,), F32)]) * scale
    return jnp.tile(blk, MLA_HEADS)[None, :]


def _layer_weights(lp):
    w_in = lp["w_in"]
    kr_blk = jnp.zeros((D_MODEL, LANES), F32).at[:, 64:96].set(w_in[:, _OFF_KR:_OFF_MQ])
    wa = jnp.concatenate([w_in[:, _OFF_Q:_OFF_KR], kr_blk, w_in[:, _OFF_MQ:_OFF_G]], axis=1).astype(BF16)
    wuq = jnp.pad(lp["w_mla_uq"], ((0, 0), (0, 0), (0, LANES - MLA_QK))).reshape(MLA_Q_LORA, MLA_HEADS * LANES).astype(BF16)
    wuk_pad = jnp.pad(lp["w_mla_uk"], ((0, 0), (0, 0), (0, LANES - MLA_NOPE))).reshape(MLA_KV_LORA, MLA_HEADS * LANES)
    lane = jnp.arange(LANES)
    rope_eye = jnp.where(((lane >= 64) & (lane < 96))[:, None], jnp.eye(LANES, dtype=F32), 0.0)
    wk = jnp.concatenate([wuk_pad, jnp.tile(rope_eye, (1, MLA_HEADS))], axis=0).astype(BF16)
    g64 = jnp.kron(jnp.eye(4, dtype=F32), jnp.ones((64, 64), F32)).astype(BF16)
    g128 = jnp.kron(jnp.eye(2, dtype=F32), jnp.ones((128, 128), F32)).astype(BF16)
    gk_pad = _tied_pad(lp["mla_k_norm"], 1.0)
    inproj_w = [lp["norm_mix"][None, :], wa, wuq, wk, g64, g128,
                jnp.tile(lp["diff_q_norm"], 8)[None, :] * DIFF_SCALE, jnp.tile(lp["diff_k_norm"], 4)[None, :],
                lp["mla_q_a_norm"][None, :], lp["mla_kv_a_norm"][None, :],
                _tied_pad(lp["mla_q_norm"], MLA_SCALE), gk_pad,
                jnp.tile(lp["mem_q_norm"], MEM_HEADS)[None, :] * MEM_SCALE]
    wuv = lp["w_mla_uv"]
    wuv2 = jnp.einsum("rhd,hg->hrgd", wuv, jnp.eye(MLA_HEADS, dtype=F32)).reshape(MLA_HEADS, MLA_KV_LORA, 512).astype(BF16)
    wuk_t = jnp.transpose(lp["w_mla_uk"], (1, 2, 0))
    aq = jnp.zeros((MLA_HEADS, LANES, 2 * LANES), F32).at[:, 0:MLA_NOPE, 0:LANES].set(wuk_t)
    aq = aq.at[:, :, LANES:].add(rope_eye[None]).astype(BF16)
    wukt = wuk_t.reshape(MLA_HEADS * MLA_NOPE, MLA_KV_LORA).astype(BF16)
    wr = jnp.pad(lp["w_router"], ((0, 0), (0, LANES - N_EXPERTS)))
    wrh = wr.astype(BF16)
    wrl = (wr - wrh.astype(F32)).astype(BF16)
    merge_w = [lp["norm_mix"][None, :], w_in[:, _OFF_G:].astype(BF16), lp["b_gate"][None, :],
               lp["w_branch"].reshape(3, 512, D_MODEL).astype(BF16), lp["w_out"].astype(BF16),
               lp["norm_ffn"][None, :], wrh, wrl, jnp.pad(lp["b_router"], (0, LANES - N_EXPERTS))[None, :]]
    return dict(inproj=inproj_w, g128=g128, wuv2=wuv2, aq=aq, wukt=wukt, gk_pad=gk_pad, merge=merge_w)


def _moe_plan(ti, bm):
    n = ti.shape[0]
    a = n * TOP_K
    flat_e = ti[:, :TOP_K].reshape(a)
    onehot = (flat_e[:, None] == jnp.arange(N_EXPERTS, dtype=jnp.int32)[None, :]).astype(jnp.int32)
    csum = jnp.cumsum(onehot, axis=0)
    counts = csum[-1]
    rank = jnp.sum(onehot * csum, axis=1) - 1
    padded = (counts + bm - 1) // bm * bm
    pend = jnp.cumsum(padded)
    pstart = pend - padded
    dest = jnp.sum(onehot * pstart[None, :], axis=1) + rank
    n_rows = (a + N_EXPERTS * (bm - 1) + bm - 1) // bm * bm
    n_blocks = n_rows // bm
    spare = a + (jnp.arange(n_rows, dtype=jnp.int32) % ROW_GROUP)
    row_slot = spare.at[dest].set(jnp.arange(a, dtype=jnp.int32), unique_indices=True)
    blk_start = jnp.arange(n_blocks, dtype=jnp.int32) * bm
    block_e = jnp.minimum(jnp.sum((blk_start[:, None] >= pend[None, :]).astype(jnp.int32), axis=1), N_EXPERTS - 1)
    first = jnp.concatenate([jnp.ones((1,), jnp.int32), (block_e[1:] != block_e[:-1]).astype(jnp.int32)])
    eh = (block_e[:, None] == jnp.arange(N_EXPERTS, dtype=jnp.int32)[None, :]).astype(jnp.int32)
    valid_end = jnp.sum(eh * (pstart + counts)[None, :], axis=1)
    nval = jnp.where(blk_start < pend[-1], jnp.clip(valid_end - blk_start, 0, bm), 0).astype(jnp.int32)
    row_tok = jnp.minimum(lax.shift_right_logical(row_slot, 2), n - 1)
    row_idx = jnp.concatenate([row_slot.reshape(n_blocks, bm // LANES, LANES),
                               row_tok.reshape(n_blocks, bm // LANES, LANES)], axis=1)
    return block_e, first, nval, row_idx, a


def _qbd(dq_s, b, nq):
    q = dq_s.reshape(b, nq, 2, 2, 2, DIFF_HEAD_DIM)
    q = jnp.transpose(q, (0, 2, 4, 3, 1, 5))
    eye = jnp.eye(4, dtype=q.dtype).reshape(2, 2, 2, 2)
    out = jnp.einsum("bncgqd,ncmk->bncgqmkd", q, eye)
    return out.reshape(b, 64, 256)


def kernel(x_prompt, x_sample, mem_prompt, cache_diff_k, cache_diff_v, cache_mla_ckv, cache_mla_krope, cache_mem_k, cache_mem_v, page_table, norm_mix, norm_mem, w_in, b_gate, diff_q_norm, diff_k_norm, diff_lambda, diff_subln, mla_q_a_norm, w_mla_uq, mla_kv_a_norm, w_mla_uk, w_mla_uv, mla_q_norm, mla_k_norm, w_mem_kv, mem_q_norm, mem_k_norm, w_branch, w_out, norm_ffn, w_router, b_router, w_gate_up, b_gate_up, w_down, b_down):
    depth = w_in.shape[0]
    bp, t, _ = x_prompt.shape
    bs, nq, _ = x_sample.shape
    n_pool, n_pages = cache_diff_k.shape[1], page_table.shape[1]
    past_len = n_pages * PAGE
    n_p, n_s = bp * t, bs * nq

    tm_p, tm_s = min(TOKEN_TILE, n_p), min(TOKEN_TILE, n_s)
    tabs_p = _rope_tables(jnp.arange(t, dtype=jnp.int32), t)
    tabs_s = _rope_tables(past_len + jnp.arange(nq, dtype=jnp.int32), tm_s)
    del tm_p

    kc = jnp.transpose(cache_diff_k, (0, 1, 3, 4, 5, 2)).reshape(depth * n_pool, 256, PAGE)
    vc = cache_diff_v.reshape(depth * n_pool, 2 * PAGE, DIFF_V_DIM)
    cc = cache_mla_ckv.reshape(depth * n_pool, PAGE, MLA_KV_LORA)
    rc = jnp.transpose(cache_mla_krope, (0, 1, 3, 2)).reshape(depth * n_pool, MLA_ROPE, PAGE)
    mkc = cache_mem_k.reshape(depth * bs, N_MEM * MEM_HEADS, MEM_HEAD_DIM)
    mvc = cache_mem_v.reshape(depth * bs, N_MEM * MEM_HEADS, MEM_HEAD_DIM)

    xp = x_prompt.reshape(n_p, D_MODEL)
    xs = x_sample.reshape(n_s, D_MODEL)
    outs = [[] for _ in range(10)]
    for layer in range(depth):
        lp = dict(norm_mix=norm_mix[layer], w_in=w_in[layer], b_gate=b_gate[layer], diff_q_norm=diff_q_norm[layer],
                  diff_k_norm=diff_k_norm[layer], mla_q_a_norm=mla_q_a_norm[layer], w_mla_uq=w_mla_uq[layer],
                  mla_kv_a_norm=mla_kv_a_norm[layer], w_mla_uk=w_mla_uk[layer], w_mla_uv=w_mla_uv[layer],
                  mla_q_norm=mla_q_norm[layer], mla_k_norm=mla_k_norm[layer], mem_q_norm=mem_q_norm[layer],
                  w_branch=w_branch[layer], w_out=w_out[layer], norm_ffn=norm_ffn[layer],
                  w_router=w_router[layer], b_router=b_router[layer])
        w = _layer_weights(lp)
        lam_init = 0.8 - 0.6 * math.exp(-0.3 * layer)
        lamp = diff_lambda[layer].astype(F32)
        lam = (jnp.exp(jnp.sum(lamp[0] * lamp[1])) - jnp.exp(jnp.sum(lamp[2] * lamp[3])) + lam_init).reshape(1)
        gs = (diff_subln[layer] * (1.0 - lam_init))[None, :]
        pt_flat = (page_table + layer * n_pool).reshape(-1).astype(jnp.int32)

        (dq, _, dk16, dv32, dv16, mq, ckv32, ckv16, kr32, kmla, memq, dvt, ckvt, dkt) = _inproj(xp, tabs_p, w["inproj"], t)
        p_dk = jnp.transpose(dkt.reshape(bp, 2, 2, DIFF_HEAD_DIM, t), (0, 4, 1, 2, 3))
        o_diff = _diff_prompt(lam, dq.reshape(bp, t, 512), dk16.reshape(bp, t, 256), dvt, gs)
        o_mla = _mla_prompt(mq.reshape(bp, t, 1024), kmla.reshape(bp, t, 1024), ckvt, w["wuv2"])
        mk32, mk16, mv32, mv16 = _memkv(mem_prompt.reshape(bp * N_MEM, D_MODEL), norm_mem[layer][None, :],
                                        w_mem_kv[layer].astype(BF16), w["g128"],
                                        jnp.tile(mem_k_norm[layer], MEM_HEADS)[None, :])
        o_mem = _mem_prompt(memq.reshape(bp, t, 512), mk16.reshape(bp, N_MEM, 512), mv16.reshape(bp, N_MEM, 512))
        group_p = (xp, o_diff.reshape(n_p, 512), o_mla.reshape(n_p, 512), o_mem.reshape(n_p, 512))
        for lst, val in zip(outs[:6], (p_dk, dv32.reshape(bp, t, 2, 128),
                                       ckv32.reshape(bp, t, 128), kr32.reshape(bp, t, 32),
                                       mk32.reshape(bp, N_MEM, 4, 128), mv32.reshape(bp, N_MEM, 4, 128))):
            lst.append(val)

        (dq, dk32, dk16, dv32, dv16, mq, ckv32, ckv16, kr32, kmla, memq, _, _, _) = _inproj(xs, tabs_s, w["inproj"],
                                                                                            min(TOKEN_TILE, n_s))
        o_diff, o_mla = _decode(pt_flat, lam, _qbd(dq, bs, nq), dk16.reshape(bs, nq, 256), dv16.reshape(bs, nq, 256), gs,
                                mq.reshape(bs, nq, 1024), kmla.reshape(bs, nq, 1024), ckv16.reshape(bs, nq, LANES),
                                w["gk_pad"], w["aq"], w["wukt"], w["wuv2"], kc, vc, cc, rc, n_pages)
        o_mem = _mem_sample(memq.reshape(bs, nq, 512), mkc, mvc, layer * bs)
        group_s = (xs, o_diff.reshape(n_s, 512), o_mla.reshape(n_s, 512), o_mem.reshape(n_s, 512))
        for lst, val in zip(outs[6:], (dk32.reshape(bs, nq, 2, 2, 64), dv32.reshape(bs, nq, 2, 128),
                                       ckv32.reshape(bs, nq, 128), kr32.reshape(bs, nq, 32))):
            lst.append(val)

        xm, h2, tv, ti = _merge(group_p, group_s, w["merge"])
        block_e, first, nval, row_slot3, n_slots = _moe_plan(ti, MOE_ROWS)
        yslots = _experts(block_e + layer * N_EXPERTS, first, nval, row_slot3,
                          w_gate_up.reshape(depth * N_EXPERTS, D_MODEL, 2 * D_EXPERT),
                          b_gate_up.reshape(depth * N_EXPERTS, 1, 2 * D_EXPERT),
                          w_down.reshape(depth * N_EXPERTS, D_EXPERT, D_MODEL),
                          b_down.reshape(depth * N_EXPERTS, 1, D_MODEL), h2, n_slots)
        xp = _combine(xm, tv, yslots, 0, n_p)
        xs = _combine(xm, tv, yslots, n_p, n_s)

    stack = lambda lst: jnp.stack(lst)
    return (xp.reshape(bp, t, D_MODEL), xs.reshape(bs, nq, D_MODEL)) + tuple(stack(o) for o in outs)
```

```python
import functools
import math

import jax
import jax.numpy as jnp
from jax import lax
from jax.experimental import pallas as pl
from jax.experimental.pallas import tpu as pltpu

F32 = jnp.float32
BF16 = jnp.bfloat16

D_MODEL = 1024
DIFF_HEAD_DIM = 64
DIFF_V_DIM = 128
MLA_HEADS = 8
MLA_Q_LORA = 256
MLA_KV_LORA = 128
MLA_NOPE = 64
MLA_ROPE = 32
MLA_V = 64
MLA_QK = MLA_NOPE + MLA_ROPE
MEM_HEADS = 4
MEM_HEAD_DIM = 128
N_MEM = 256
N_EXPERTS = 32
TOP_K = 4
D_EXPERT = 1024
SWIGLU_ALPHA = 1.702
SWIGLU_LIMIT = 7.0
ROPE_THETA = 10000.0
NORM_EPS = 1e-6
SUBLN_EPS = 1e-5
NEG_INF = -1e30
PAGE = 128

DIFF_SCALE = DIFF_HEAD_DIM ** -0.5
MLA_SCALE = MLA_QK ** -0.5
MEM_SCALE = MEM_HEAD_DIM ** -0.5

LANES = 128
MXU_DIM = 256
VMEM_LIMIT = 52 * 1024 * 1024
TOKEN_TILE = 256
ATTN_TILE = 256
PROMPT_Q_TILE = 256
PAGES_PER_STEP = 8
SEQS_PER_STEP = 4
PAGE_BUFFERS = 3
MOE_ROWS = 256
ROW_GROUP = 32

_OFF_Q, _OFF_K, _OFF_V, _OFF_CQ, _OFF_CKV, _OFF_KR, _OFF_MQ, _OFF_G = 0, 512, 768, 1024, 1280, 1408, 1440, 1952


def _params(sem):
    return pltpu.CompilerParams(dimension_semantics=sem, vmem_limit_bytes=VMEM_LIMIT)


def _full(a):
    nd = a.ndim
    return pl.BlockSpec(a.shape, lambda *_: (0,) * nd)


def _dot(a, b):
    return jnp.dot(a, b, preferred_element_type=F32)


def _dot_nt(a, b):
    return lax.dot_general(a, b, (((1,), (1,)), ((), ())), preferred_element_type=F32)


def _rms_rows(x, eps=NORM_EPS):
    return x * lax.rsqrt(jnp.mean(x * x, axis=-1, keepdims=True) + eps)


def _group_sumsq(v, g_ref):
    sq = (v * v).astype(BF16)
    parts = [_dot(sq[:, j * MXU_DIM:(j + 1) * MXU_DIM], g_ref[...]) for j in range(v.shape[1] // MXU_DIM)]
    return parts[0] if len(parts) == 1 else jnp.concatenate(parts, axis=-1)


def _rope_lanes(v, c, sa, sb, half):
    outs = []
    for j in range(v.shape[1] // LANES):
        b = v[:, j * LANES:(j + 1) * LANES]
        outs.append(b * c + pltpu.roll(b, LANES - half, 1) * sa + pltpu.roll(b, half, 1) * sb)
    return outs[0] if len(outs) == 1 else jnp.concatenate(outs, axis=-1)


def _inproj_kernel(x_ref, gmix_ref, wa_ref, wuq_ref, wk_ref, g64_ref, g128_ref,
                   gq_ref, gk_ref, gcq_ref, gckv_ref, gmq_ref, gkm_ref, gmemq_ref,
                   c64_ref, sa64_ref, sb64_ref, c32_ref, sa32_ref, sb32_ref,
                   dq_ref, dk32_ref, dk16_ref, dv32_ref, dv16_ref, mq_ref,
                   ckv32_ref, ckv16_ref, kr32_ref, kmla_ref, memq_ref, dvt_ref, ckvt_ref, dkt_ref):
    x = x_ref[...]
    h = (_rms_rows(x) * gmix_ref[...]).astype(BF16)
    z = _dot(h, wa_ref[...])
    c64, sa64, sb64 = c64_ref[...], sa64_ref[...], sb64_ref[...]
    c32, sa32, sb32 = c32_ref[...], sa32_ref[...], sb32_ref[...]

    zq = z[:, 0:512]
    qn = zq * lax.rsqrt(_group_sumsq(zq, g64_ref) * (1.0 / DIFF_HEAD_DIM) + NORM_EPS) * gq_ref[...]
    dq_ref[...] = _rope_lanes(qn, c64, sa64, sb64, 32).astype(BF16)

    zk = z[:, 512:768]
    kn = zk * lax.rsqrt(_group_sumsq(zk, g64_ref) * (1.0 / DIFF_HEAD_DIM) + NORM_EPS) * gk_ref[...]
    dk = _rope_lanes(kn, c64, sa64, sb64, 32)
    dk32_ref[...] = dk
    dkt_ref[...] = dk.T
    dk16_ref[...] = dk.astype(BF16)

    dv = z[:, 768:1024]
    dv32_ref[...] = dv
    dv16_ref[...] = dv.astype(BF16)
    dvt_ref[...] = dv.T.astype(BF16)

    cq = (_rms_rows(z[:, 1024:1280]) * gcq_ref[...]).astype(BF16)
    mqr = _dot(cq, wuq_ref[...])
    mqn = mqr * lax.rsqrt(_group_sumsq(mqr, g128_ref) * (1.0 / MLA_QK) + NORM_EPS) * gmq_ref[...]
    mq_ref[...] = _rope_lanes(mqn, c32, sa32, sb32, 16).astype(BF16)

    ckv = _rms_rows(z[:, 1280:1408]) * gckv_ref[...]
    ckv32_ref[...] = ckv
    ckv16 = ckv.astype(BF16)
    ckv16_ref[...] = ckv16
    ckvt_ref[...] = ckv.T.astype(BF16)

    krb = _rope_lanes(z[:, 1408:1536], c32, sa32, sb32, 16)
    kr32_ref[...] = krb[:, 64:96]
    kin = jnp.concatenate([ckv16, krb.astype(BF16)], axis=-1)
    kraw = _dot(kin, wk_ref[...])
    kmla_ref[...] = (kraw * lax.rsqrt(_group_sumsq(kraw, g128_ref) * (1.0 / MLA_QK) + NORM_EPS)
                     * gkm_ref[...]).astype(BF16)

    zm = z[:, 1536:2048]
    memq_ref[...] = (zm * lax.rsqrt(_group_sumsq(zm, g128_ref) * (1.0 / MEM_HEAD_DIM) + NORM_EPS)
                     * gmemq_ref[...]).astype(BF16)


def _inproj(x2d, tabs, wts, seq_len):
    n = x2d.shape[0]
    tm = min(TOKEN_TILE, n)
    period = tabs[0].shape[0] // tm
    per_seq = seq_len // tm
    row = lambda w: pl.BlockSpec((tm, w), lambda i: (i, 0))
    tab = pl.BlockSpec((tm, LANES), lambda i: (i % period, 0))
    out_w = [(512, BF16), (256, F32), (256, BF16), (256, F32), (256, BF16), (1024, BF16),
             (128, F32), (128, BF16), (32, F32), (1024, BF16), (512, BF16)]
    tile_t = lambda d: pl.BlockSpec((None, d, tm), lambda i: (i, 0, 0))
    return pl.pallas_call(
        _inproj_kernel,
        grid=(n // tm,),
        in_specs=[row(D_MODEL)] + [_full(a) for a in wts] + [tab] * 6,
        out_specs=[row(w) for w, _ in out_w] + [tile_t(256), tile_t(LANES),
                                                pl.BlockSpec((None, 256, tm), lambda i: (i // per_seq, 0, i % per_seq))],
        out_shape=[jax.ShapeDtypeStruct((n, w), dt) for w, dt in out_w]
                  + [jax.ShapeDtypeStruct((n // tm, 256, tm), BF16), jax.ShapeDtypeStruct((n // tm, LANES, tm), BF16),
                     jax.ShapeDtypeStruct((n // seq_len, 256, seq_len), F32)],
        compiler_params=_params(("parallel",)),
        name="inproj",
    )(x2d, *wts, *tabs)


def _memkv_kernel(x_ref, g_ref, w_ref, g128_ref, gk_ref, k32_ref, k16_ref, v32_ref, v16_ref):
    h = (_rms_rows(x_ref[...]) * g_ref[...]).astype(BF16)
    kv = _dot(h, w_ref[...])
    k = kv[:, 0:512]
    k = k * lax.rsqrt(_group_sumsq(k, g128_ref) * (1.0 / MEM_HEAD_DIM) + NORM_EPS) * gk_ref[...]
    v = kv[:, 512:1024]
    k32_ref[...] = k
    k16_ref[...] = k.astype(BF16)
    v32_ref[...] = v
    v16_ref[...] = v.astype(BF16)


def _memkv(mem2d, g, w, g128, gk):
    n = mem2d.shape[0]
    tm = min(TOKEN_TILE, n)
    row = lambda w_: pl.BlockSpec((tm, w_), lambda i: (i, 0))
    return pl.pallas_call(
        _memkv_kernel,
        grid=(n // tm,),
        in_specs=[row(D_MODEL), _full(g), _full(w), _full(g128), _full(gk)],
        out_specs=[row(512)] * 4,
        out_shape=[jax.ShapeDtypeStruct((n, 512), dt) for dt in (F32, BF16, F32, BF16)],
        compiler_params=_params(("parallel",)),
        name="memkv",
    )(mem2d, g, w, g128, gk)


def _online_update(s, m_ref, l_ref, rows=None):
    sl = slice(None) if rows is None else rows
    m_old = m_ref[sl, :]
    m_new = jnp.maximum(m_old, jnp.max(s, axis=-1, keepdims=True))
    alpha = jnp.exp(m_old - m_new)
    p = jnp.exp(s - m_new)
    l_ref[sl, :] = alpha * l_ref[sl, :] + jnp.sum(p, axis=-1, keepdims=True)
    m_ref[sl, :] = m_new
    return p, alpha


def _cols_softmax_step(s, m_ref, l_ref, cols):
    m_old = m_ref[:, cols]
    m_new = jnp.maximum(m_old, jnp.max(s, axis=0, keepdims=True))
    alpha = jnp.exp(m_old - m_new)
    p = jnp.exp(s - m_new)
    l_ref[:, cols] = alpha * l_ref[:, cols] + jnp.sum(p, axis=0, keepdims=True)
    m_ref[:, cols] = m_new
    return p, alpha


def _causal_cols(s, tq, delta):
    row = lax.broadcasted_iota(jnp.int32, s.shape, 0)
    col = lax.broadcasted_iota(jnp.int32, s.shape, 1) & (tq - 1)
    return jnp.where(row <= col + delta, s, NEG_INF)


def _cols_attention(i, tq, tk, n_cols, score_fn, vt_ref, m_ref, l_ref, acc_ref):
    m_ref[...] = jnp.full(m_ref.shape, NEG_INF, F32)
    l_ref[...] = jnp.zeros(l_ref.shape, F32)
    acc_ref[...] = jnp.zeros(acc_ref.shape, F32)
    j_last = (i * tq) // tk
    delta = i * tq - j_last * tk

    n_chunks = n_cols // MXU_DIM

    def scores(j):
        return tuple(score_fn(j, cc) for cc in range(n_chunks))

    def finish(j, s_all, masked):
        vt = vt_ref[j]
        for cc in range(n_chunks):
            cols = slice(cc * MXU_DIM, (cc + 1) * MXU_DIM)
            s = _causal_cols(s_all[cc], tq, delta) if masked else s_all[cc]
            p, alpha = _cols_softmax_step(s, m_ref, l_ref, cols)
            acc_ref[:, cols] = alpha * acc_ref[:, cols] + _dot(vt, p.astype(BF16))

    def body(j, s_cur):
        s_next = scores(j + 1)
        finish(j, s_cur, False)
        return s_next

    s_last = lax.fori_loop(0, j_last, body, scores(0))
    finish(j_last, s_last, True)


def _diff_prompt_kernel(lam_ref, q_ref, k_ref, vt_ref, gs_ref, o_ref, qt_ref, m_ref, l_ref, acc_ref, *, tq, tk):
    i = pl.program_id(2)
    dim = lax.broadcasted_iota(jnp.int32, (LANES, tq), 0)
    for g in range(2):
        qgt = q_ref[:, g * LANES:(g + 1) * LANES].astype(F32).T
        qt_ref[:, g * tq:(g + 1) * tq] = jnp.where(dim < DIFF_HEAD_DIM, qgt, 0.0).astype(BF16)
        qt_ref[:, (2 + g) * tq:(3 + g) * tq] = jnp.where(dim >= DIFF_HEAD_DIM, qgt, 0.0).astype(BF16)

    def scores(j, cc):
        off = pl.multiple_of(j * tk, tk)
        return _dot(k_ref[pl.ds(off, tk), :], qt_ref[:, cc * MXU_DIM:(cc + 1) * MXU_DIM])

    _cols_attention(i, tq, tk, 4 * tq, scores, vt_ref, m_ref, l_ref, acc_ref)

    o = acc_ref[...] / l_ref[...]
    odt = o[:, 0:2 * tq] - lam_ref[0] * o[:, 2 * tq:4 * tq]
    for g in range(2):
        od = odt[:, g * tq:(g + 1) * tq].T
        od = _rms_rows(od, SUBLN_EPS) * gs_ref[...]
        o_ref[:, g * LANES:(g + 1) * LANES] = od.astype(BF16)


def _diff_prompt(lam, dq, dk16, dvt, gs):
    b, t, _ = dq.shape
    tq, tk = PROMPT_Q_TILE, dvt.shape[-1]
    kern = functools.partial(_diff_prompt_kernel, tq=tq, tk=tk)
    return pl.pallas_call(
        kern,
        grid=(b, 2, t // tq),
        in_specs=[pl.BlockSpec(memory_space=pltpu.SMEM),
                  pl.BlockSpec((None, tq, 256), lambda bi, n, i: (bi, i, n)),
                  pl.BlockSpec((None, t, LANES), lambda bi, n, i: (bi, 0, n)),
                  pl.BlockSpec((None, t // tk, LANES, tk), lambda bi, n, i: (bi, 0, n, 0)),
                  pl.BlockSpec((1, LANES), lambda bi, n, i: (0, 0))],
        out_specs=pl.BlockSpec((None, tq, 256), lambda bi, n, i: (bi, i, n)),
        out_shape=jax.ShapeDtypeStruct((b, t, 512), BF16),
        scratch_shapes=[pltpu.VMEM((LANES, 4 * tq), BF16), pltpu.VMEM((1, 4 * tq), F32),
                        pltpu.VMEM((1, 4 * tq), F32), pltpu.VMEM((LANES, 4 * tq), F32)],
        compiler_params=_params(("parallel", "parallel", "parallel")),
        name="diff_prompt",
    )(lam, dq, dk16, dvt.reshape(b, t // tk, 256, tk), gs)


def _mla_prompt_kernel(q_ref, k_ref, ct_ref, wuv_ref, o_ref, qt_ref, m_ref, l_ref, acc_ref, *, tq, tk):
    i = pl.program_id(1)
    for h in range(MLA_HEADS):
        qt_ref[:, h * tq:(h + 1) * tq] = q_ref[:, h * LANES:(h + 1) * LANES].astype(F32).T.astype(BF16)

    def scores(j, h):
        off = pl.multiple_of(j * tk, tk)
        return _dot(k_ref[pl.ds(off, tk), h * LANES:(h + 1) * LANES], qt_ref[:, h * tq:(h + 1) * tq])

    _cols_attention(i, tq, tk, MLA_HEADS * tq, scores, ct_ref, m_ref, l_ref, acc_ref)

    lat_t = acc_ref[...] / l_ref[...]
    o = None
    for h in range(MLA_HEADS):
        part = _dot(lat_t[:, h * tq:(h + 1) * tq].T.astype(BF16), wuv_ref[h])
        o = part if o is None else o + part
    o_ref[...] = o.astype(BF16)


def _mla_prompt(mq, kmla, ckvt, wuv2):
    b, t, _ = mq.shape
    tq, tk = PROMPT_Q_TILE, ckvt.shape[-1]
    assert tq == MXU_DIM, "one head per MXU-wide column chunk"
    kern = functools.partial(_mla_prompt_kernel, tq=tq, tk=tk)
    cols = MLA_HEADS * tq
    return pl.pallas_call(
        kern,
        grid=(b, t // tq),
        in_specs=[pl.BlockSpec((None, tq, 1024), lambda bi, i: (bi, i, 0)),
                  pl.BlockSpec((None, t, 1024), lambda bi, i: (bi, 0, 0)),
                  pl.BlockSpec((None, t // tk, LANES, tk), lambda bi, i: (bi, 0, 0, 0)),
                  pl.BlockSpec(wuv2.shape, lambda bi, i: (0, 0, 0))],
        out_specs=pl.BlockSpec((None, tq, 512), lambda bi, i: (bi, i, 0)),
        out_shape=jax.ShapeDtypeStruct((b, t, 512), BF16),
        scratch_shapes=[pltpu.VMEM((LANES, cols), BF16), pltpu.VMEM((1, cols), F32),
                        pltpu.VMEM((1, cols), F32), pltpu.VMEM((LANES, cols), F32)],
        compiler_params=_params(("parallel", "parallel")),
        name="mla_prompt",
    )(mq, kmla, ckvt.reshape(b, t // tk, LANES, tk), wuv2)


def _softmax_pv(s, v):
    m = jnp.max(s, axis=-1, keepdims=True)
    p = jnp.exp(s - m)
    l = jnp.sum(p, axis=-1, keepdims=True)
    return _dot(p.astype(BF16), v) / l


def _mem_prompt_kernel(q_ref, k_ref, v_ref, o_ref):
    for h in range(MEM_HEADS):
        sl = slice(h * LANES, (h + 1) * LANES)
        s = _dot_nt(q_ref[:, sl], k_ref[:, sl])
        o_ref[:, sl] = _softmax_pv(s, v_ref[:, sl]).astype(BF16)


def _mem_prompt(memq, mk16, mv16):
    b, t, _ = memq.shape
    tq = min(2 * ATTN_TILE, t)
    return pl.pallas_call(
        _mem_prompt_kernel,
        grid=(b, t // tq),
        in_specs=[pl.BlockSpec((None, tq, 512), lambda bi, i: (bi, i, 0)),
                  pl.BlockSpec((None, N_MEM, 512), lambda bi, i: (bi, 0, 0)),
                  pl.BlockSpec((None, N_MEM, 512), lambda bi, i: (bi, 0, 0))],
        out_specs=pl.BlockSpec((None, tq, 512), lambda bi, i: (bi, i, 0)),
        out_shape=jax.ShapeDtypeStruct((b, t, 512), BF16),
        compiler_params=_params(("parallel", "parallel")),
        name="mem_prompt",
    )(memq, mk16, mv16)


def _mem_sample_kernel(q_ref, k_ref, v_ref, o_ref, *, bb):
    heads = [(bi, h) for bi in range(bb) for h in range(MEM_HEADS)]
    scores = []
    for bi, h in heads:
        k = k_ref[bi, pl.ds(h, N_MEM, stride=MEM_HEADS), :].astype(BF16)
        scores.append(_dot_nt(q_ref[bi, :, h * LANES:(h + 1) * LANES], k))
    for (bi, h), s in zip(heads, scores):
        v = v_ref[bi, pl.ds(h, N_MEM, stride=MEM_HEADS), :].astype(BF16)
        o_ref[bi, :, h * LANES:(h + 1) * LANES] = _softmax_pv(s, v).astype(BF16)


def _mem_sample(memq, ck, cv, b_off):
    b, t, _ = memq.shape
    bb = math.gcd(b, 4)
    off = b_off // bb
    kern = functools.partial(_mem_sample_kernel, bb=bb)
    rows = N_MEM * MEM_HEADS
    return pl.pallas_call(
        kern,
        grid=(b // bb,),
        in_specs=[pl.BlockSpec((bb, t, 512), lambda i: (i, 0, 0)),
                  pl.BlockSpec((bb, rows, LANES), lambda i: (i + off, 0, 0)),
                  pl.BlockSpec((bb, rows, LANES), lambda i: (i + off, 0, 0))],
        out_specs=pl.BlockSpec((bb, t, 512), lambda i: (i, 0, 0)),
        out_shape=jax.ShapeDtypeStruct((b, t, 512), BF16),
        compiler_params=_params(("parallel",)),
        name="mem_sample",
    )(memq, ck, cv)


def _page_copies(pt_ref, grp, j, slot, bb, ch, nc, streams):
    cps = []
    for bi in range(bb):
        base = ((grp * bb + bi) * nc + j) * ch
        for r in range(ch):
            pg = pt_ref[base + r]
            for src, buf, sem in streams:
                cps.append(pltpu.make_async_copy(src.at[pg], buf.at[slot, bi * ch + r], sem.at[slot]))
    return cps


def _start_all(copies):
    for idx, cp in enumerate(copies):
        cp.start(priority=idx % 2)


def _stream_begin(pt_ref, grp, j, bb, ch, nc, total, streams):
    d = grp * nc + j

    @pl.when(d == 0)
    def _():
        for ahead in range(min(PAGE_BUFFERS - 1, total)):
            _start_all(_page_copies(pt_ref, ahead // nc, ahead % nc, ahead, bb, ch, nc, streams))

    for cp in _page_copies(pt_ref, grp, j, lax.rem(d, PAGE_BUFFERS), bb, ch, nc, streams):
        cp.wait()


def _stream_end(pt_ref, d, bb, ch, nc, total, streams):
    nxt = d + (PAGE_BUFFERS - 1)

    @pl.when(nxt < total)
    def _():
        _start_all(_page_copies(pt_ref, lax.div(nxt, nc), lax.rem(nxt, nc), lax.rem(nxt, PAGE_BUFFERS),
                                bb, ch, nc, streams))


def _causal_new(s2, nq):
    qpos = lax.broadcasted_iota(jnp.int32, s2.shape, 0) & (nq - 1)
    kpos = lax.broadcasted_iota(jnp.int32, s2.shape, 1)
    return jnp.where(kpos <= qpos, s2, NEG_INF)


def _decode_kernel(pt_ref, lam_ref, qd_ref, kn_ref, vn_ref, gs_ref, mq_ref, kmn_ref, cn_ref, gk_ref, aq_ref,
                   wukt_ref, wuv_ref, kc_hbm, vc_hbm, cc_hbm, rc_hbm, od_ref, om_ref,
                   kbuf, vbuf, cbuf, rbuf, sem, md_ref, ld_ref, accd_ref, lw_ref, qr_ref, mm_ref, lm_ref, accm_ref,
                   sd_a, sd_b, sm_a, sm_b, *, bb, ch, nc, total):
    grp, j = pl.program_id(0), pl.program_id(1)
    d = grp * nc + j
    streams = [(kc_hbm, kbuf, sem.at[0]), (vc_hbm, vbuf, sem.at[1]), (cc_hbm, cbuf, sem.at[2]), (rc_hbm, rbuf, sem.at[3])]
    score_bufs = ((sd_a, sm_a), (sd_b, sm_b))
    nq = mq_ref.shape[1]
    nr = MLA_HEADS * nq
    nk = MLA_HEADS * MLA_NOPE

    def init():
        for ref in (md_ref, mm_ref):
            ref[...] = jnp.full(ref.shape, NEG_INF, F32)
        for ref in (ld_ref, lm_ref, accd_ref, accm_ref):
            ref[...] = jnp.zeros(ref.shape, F32)
        for bi in range(bb):
            lw_ref[bi, 0:nk, :] = wukt_ref[...]
            qg = (mq_ref[bi].astype(F32) * gk_ref[...]).astype(BF16)
            for h in range(MLA_HEADS):
                qa = _dot(qg[:, h * LANES:(h + 1) * LANES], aq_ref[h])
                lw_ref[bi, nk + h * nq:nk + (h + 1) * nq, :] = qa[:, 0:LANES].astype(BF16)
                qr_ref[bi, h * nq:(h + 1) * nq, :] = qa[:, LANES + 64:LANES + 96].astype(BF16)

    def score_phase(slot, sd_ref, sm_ref, seqs=range(bb)):
        for bi in seqs:
            pages = range(bi * ch, (bi + 1) * ch)
            kt = jnp.concatenate([kbuf[slot, r].astype(BF16) for r in pages], axis=-1)
            sd_ref[bi] = _dot(qd_ref[bi], kt)
            c = jnp.concatenate([cbuf[slot, r] for r in pages], axis=0).astype(BF16)
            krt = jnp.concatenate([rbuf[slot, r] for r in pages], axis=-1)
            big = _dot_nt(lw_ref[bi], c)
            nkeys = big.shape[1]
            knt = big[0:nk]
            ssq = jnp.sum((knt * knt).reshape(MLA_HEADS, MLA_NOPE, nkeys), axis=1)
            ssq = ssq + jnp.sum(krt * krt, axis=0, keepdims=True)
            rn = lax.rsqrt(ssq * (1.0 / MLA_QK) + NORM_EPS)
            sm = big[nk:nk + nr] + _dot(qr_ref[bi], krt.astype(BF16))
            sm_ref[bi] = (sm.reshape(MLA_HEADS, nq, nkeys) * rn[:, None, :]).reshape(nr, nkeys)

    def value_phase(slot, sd_ref, sm_ref, seqs=range(bb)):
        for bi in seqs:
            pages = range(bi * ch, (bi + 1) * ch)
            p, alpha = _online_update(sd_ref[bi], md_ref.at[bi], ld_ref.at[bi])
            p = p.astype(BF16)
            pv = []
            for n in range(2):
                v = jnp.concatenate([vbuf[slot, r, pl.ds(n, PAGE, stride=2), :].astype(BF16) for r in pages], axis=0)
                pv.append(_dot(p[n * 32:(n + 1) * 32], v))
            accd_ref[bi] = alpha * accd_ref[bi] + jnp.concatenate(pv, axis=0)
            c = jnp.concatenate([cbuf[slot, r] for r in pages], axis=0).astype(BF16)
            pm, alpham = _online_update(sm_ref[bi], mm_ref.at[bi], lm_ref.at[bi])
            accm_ref[bi] = alpham * accm_ref[bi] + _dot(pm.astype(BF16), c)

    cur_slot = lax.rem(d, PAGE_BUFFERS)
    prev_slot = lax.rem(d + (PAGE_BUFFERS - 1), PAGE_BUFFERS)

    @pl.when(j == 0)
    def _():
        _stream_begin(pt_ref, grp, j, bb, ch, nc, total, streams)
        init()
        score_phase(cur_slot, *score_bufs[0])

    for parity in range(2):
        @pl.when((j > 0) & (j < nc) & ((j & 1) == parity))
        def _():
            _stream_begin(pt_ref, grp, j, bb, ch, nc, total, streams)
            score_phase(cur_slot, *score_bufs[parity])
            value_phase(prev_slot, *score_bufs[1 - parity])

    @pl.when(j < nc)
    def _():
        _stream_end(pt_ref, d, bb, ch, nc, total, streams)

    @pl.when(j == nc)
    def _():
        value_phase(prev_slot, *score_bufs[(nc - 1) % 2])
        for bi in range(bb):
            q = qd_ref[bi]
            vn = vn_ref[bi]
            p2, alpha2 = _online_update(_causal_new(_dot_nt(q, kn_ref[bi]), nq), md_ref.at[bi], ld_ref.at[bi])
            p2 = p2.astype(BF16)
            pv2 = [_dot(p2[n * 32:(n + 1) * 32], vn[:, n * LANES:(n + 1) * LANES]) for n in range(2)]
            o = (alpha2 * accd_ref[bi] + jnp.concatenate(pv2, axis=0)) / ld_ref[bi]
            for n in range(2):
                on = o[n * 32:(n + 1) * 32]
                od = on[0:16] - lam_ref[0] * on[16:32]
                od = _rms_rows(od, SUBLN_EPS) * gs_ref[...]
                for g in range(2):
                    hh = n * 2 + g
                    od_ref[bi, :, hh * LANES:(hh + 1) * LANES] = od[g * nq:(g + 1) * nq].astype(BF16)

            qf = mq_ref[bi]
            kn = kmn_ref[bi]
            s2 = jnp.concatenate([_dot_nt(qf[:, h * LANES:(h + 1) * LANES], kn[:, h * LANES:(h + 1) * LANES])
                                  for h in range(MLA_HEADS)], axis=0)
            p3, alpha3 = _online_update(_causal_new(s2, nq), mm_ref.at[bi], lm_ref.at[bi])
            lat = ((alpha3 * accm_ref[bi] + _dot(p3.astype(BF16), cn_ref[bi])) / lm_ref[bi]).astype(BF16)
            om = _dot(lat[0:nq], wuv_ref[0])
            for h in range(1, MLA_HEADS):
                om = om + _dot(lat[h * nq:(h + 1) * nq], wuv_ref[h])
            om_ref[bi] = om.astype(BF16)


def _decode(pt_flat, lam, qbd, kn16, vn16, gs, mq, kmla, ckv16, gk_pad, aq, wukt, wuv2, kc, vc, cc, rc, n_pages):
    b, nq, _ = mq.shape
    assert nq == 8, "score-row layouts assume 8 new tokens per sequence"
    ch = math.gcd(PAGES_PER_STEP, n_pages)
    bb = math.gcd(SEQS_PER_STEP, b)
    nc = n_pages // ch
    kern = functools.partial(_decode_kernel, bb=bb, ch=ch, nc=nc, total=(b // bb) * nc)
    nr = MLA_HEADS * nq
    per_seq = lambda w, r=nq: pl.BlockSpec((bb, r, w), lambda g, j, pt: (g, 0, 0))
    const = lambda a: pl.BlockSpec(a.shape, lambda g, j, pt: (0,) * a.ndim)
    grid_spec = pltpu.PrefetchScalarGridSpec(
        num_scalar_prefetch=1,
        grid=(b // bb, nc + 1),
        in_specs=[pl.BlockSpec(memory_space=pltpu.SMEM), per_seq(256, 64), per_seq(256), per_seq(256), const(gs),
                  per_seq(1024), per_seq(1024), per_seq(LANES), const(gk_pad), const(aq), const(wukt), const(wuv2)]
                 + [pl.BlockSpec(memory_space=pl.ANY)] * 4,
        out_specs=[per_seq(512), per_seq(512)],
        scratch_shapes=[pltpu.VMEM((PAGE_BUFFERS, bb * ch, 256, LANES), F32),
                        pltpu.VMEM((PAGE_BUFFERS, bb * ch, 256, LANES), F32),
                        pltpu.VMEM((PAGE_BUFFERS, bb * ch, PAGE, LANES), F32),
                        pltpu.VMEM((PAGE_BUFFERS, bb * ch, MLA_ROPE, PAGE), F32),
                        pltpu.SemaphoreType.DMA((4, PAGE_BUFFERS)),
                        pltpu.VMEM((bb, 64, 1), F32), pltpu.VMEM((bb, 64, 1), F32), pltpu.VMEM((bb, 64, LANES), F32),
                        pltpu.VMEM((bb, MLA_HEADS * MLA_NOPE + nr, LANES), BF16), pltpu.VMEM((bb, nr, MLA_ROPE), BF16),
                        pltpu.VMEM((bb, nr, 1), F32), pltpu.VMEM((bb, nr, 1), F32), pltpu.VMEM((bb, nr, LANES), F32)]
                       + [pltpu.VMEM((bb, 64, ch * PAGE), F32)] * 2 + [pltpu.VMEM((bb, nr, ch * PAGE), F32)] * 2,
    )
    return pl.pallas_call(
        kern,
        grid_spec=grid_spec,
        out_shape=[jax.ShapeDtypeStruct((b, nq, 512), BF16)] * 2,
        compiler_params=_params(("arbitrary", "arbitrary")),
        name="decode",
    )(pt_flat, lam, qbd, kn16, vn16, gs, mq, kmla, ckv16, gk_pad, aq, wukt, wuv2, kc, vc, cc, rc)


def _merge_kernel(xp_ref, odp_ref, omp_ref, ocp_ref, xs_ref, ods_ref, oms_ref, ocs_ref,
                  gmix_ref, wg_ref, bg_ref, wbr_ref, wout_ref, gffn_ref,
                  wrh_ref, wrl_ref, br_ref, xm_ref, h2_ref, tv_ref, ti_ref, *, p_tiles):
    is_p = pl.program_id(0) < p_tiles
    pick = lambda a_ref, b_ref: jnp.where(is_p, a_ref[...], b_ref[...])
    x = pick(xp_ref, xs_ref)
    h = (_rms_rows(x) * gmix_ref[...]).astype(BF16)
    gates = jax.nn.sigmoid(_dot(h, wg_ref[...]) + bg_ref[...])
    merged = gates[:, 0:D_MODEL] * _dot(pick(odp_ref, ods_ref), wbr_ref[0])
    merged = merged + gates[:, D_MODEL:2 * D_MODEL] * _dot(pick(omp_ref, oms_ref), wbr_ref[1])
    merged = merged + gates[:, 2 * D_MODEL:3 * D_MODEL] * _dot(pick(ocp_ref, ocs_ref), wbr_ref[2])
    xm = x + _dot(merged.astype(BF16), wout_ref[...])
    xm_ref[...] = xm
    h2 = _rms_rows(xm) * gffn_ref[...]
    h2_ref[...] = pltpu.einshape("stl->tsl", jnp.stack([h2[:, s * LANES:(s + 1) * LANES]
                                                        for s in range(D_MODEL // LANES)], axis=0))
    hh = h2.astype(BF16)
    hl = (h2 - hh.astype(F32)).astype(BF16)
    logits = _dot(hh, wrh_ref[...]) + _dot(hl, wrh_ref[...]) + _dot(hh, wrl_ref[...]) + br_ref[...]
    lane = lax.broadcasted_iota(jnp.int32, logits.shape, 1)
    logits = jnp.where(lane < N_EXPERTS, logits, -jnp.inf)
    tv = jnp.zeros(logits.shape, F32)
    ti = jnp.zeros(logits.shape, jnp.int32)
    vals = []
    for k in range(TOP_K):
        mx = jnp.max(logits, axis=-1, keepdims=True)
        idx = jnp.min(jnp.where(logits == mx, lane, LANES), axis=-1, keepdims=True)
        vals.append(mx)
        ti = jnp.where(lane == k, idx, ti)
        logits = jnp.where(lane == idx, -jnp.inf, logits)
    es = [jnp.exp(v - vals[0]) for v in vals]
    den = es[0] + es[1] + es[2] + es[3]
    for k in range(TOP_K):
        tv = jnp.where(lane == k, es[k] / den, tv)
    tv_ref[...] = tv
    ti_ref[...] = ti


def _merge(group_p, group_s, wts):
    n_p, n_s = group_p[0].shape[0], group_s[0].shape[0]
    tm = math.gcd(TOKEN_TILE, math.gcd(n_p, n_s))
    p_tiles, n = n_p // tm, n_p + n_s
    row = lambda w: pl.BlockSpec((tm, w), lambda i: (i, 0))
    row_p = lambda w: pl.BlockSpec((tm, w), lambda i: (jnp.minimum(i, p_tiles - 1), 0))
    row_s = lambda w: pl.BlockSpec((tm, w), lambda i: (jnp.maximum(i - p_tiles, 0), 0))
    widths = (D_MODEL, 512, 512, 512)
    return pl.pallas_call(
        functools.partial(_merge_kernel, p_tiles=p_tiles),
        grid=(n // tm,),
        in_specs=[row_p(w) for w in widths] + [row_s(w) for w in widths] + [_full(a) for a in wts],
        out_specs=[row(D_MODEL), pl.BlockSpec((tm, 8, LANES), lambda i: (i, 0, 0)), row(LANES), row(LANES)],
        out_shape=[jax.ShapeDtypeStruct((n, D_MODEL), F32), jax.ShapeDtypeStruct((n, 8, LANES), F32),
                   jax.ShapeDtypeStruct((n, LANES), F32), jax.ShapeDtypeStruct((n, LANES), jnp.int32)],
        compiler_params=_params(("parallel",)),
        name="merge",
    )(*group_p, *group_s, *wts)


def _expert_kernel(be_ref, first_ref, nval_ref, slot_ref, wgu_ref, bgu_ref, wd_ref, bd_ref, x_hbm, y_hbm,
                   slot_smem, xbuf, ybuf, zbuf, wgu16, wd16, gsem, ssem, isem, *, n_blocks, n_real):
    i = pl.program_id(0)
    nv = nval_ref[i]
    groups = lambda k: lax.shift_right_logical(nval_ref[k] + (ROW_GROUP - 1), ROW_GROUP.bit_length() - 1)

    def idx_copy(k):
        ring = lax.rem(k, 3)
        return pltpu.make_async_copy(slot_ref.at[k], slot_smem.at[ring], isem.at[ring])

    halves = slot_smem.shape[1] // 2
    bm = halves * LANES

    def for_groups(k, fn):
        used = groups(k)
        for g in range(bm // ROW_GROUP):
            pl.when(g < used)(functools.partial(fn, g * ROW_GROUP))

    def issue_gather(k):
        ring, xb = lax.rem(k, 3), k & 1

        def rows(r0):
            for r in range(r0, r0 + ROW_GROUP):
                tok = slot_smem[ring, halves + r // LANES, r % LANES]
                pltpu.make_async_copy(x_hbm.at[tok], xbuf.at[xb, r], gsem.at[xb]).start(priority=r % 2)

        for_groups(k, rows)

    def wait_gather(k):
        xb = k & 1
        for_groups(k, lambda r0: pltpu.make_async_copy(
            x_hbm.at[pl.ds(0, ROW_GROUP)], xbuf.at[xb, pl.ds(r0, ROW_GROUP)], gsem.at[xb]).wait())

    def issue_scatter(k):
        ring = lax.rem(k, 3)

        def rows(r0):
            for r in range(r0, r0 + ROW_GROUP):
                slot = slot_smem[ring, r // LANES, r % LANES]
                pltpu.make_async_copy(ybuf.at[r], y_hbm.at[slot], ssem.at[0]).start(priority=r % 2)

        for_groups(k, rows)

    def wait_scatter(k):
        for_groups(k, lambda r0: pltpu.make_async_copy(
            ybuf.at[pl.ds(r0, ROW_GROUP)], y_hbm.at[pl.ds(0, ROW_GROUP)], ssem.at[0]).wait())

    @pl.when(i == 0)
    def _():
        xbuf[...] = jnp.zeros(xbuf.shape, F32)
        zbuf[...] = jnp.zeros(zbuf.shape, F32)
        spare = pltpu.make_async_copy(zbuf, y_hbm.at[pl.ds(n_real, ROW_GROUP)], ssem.at[0])
        spare.start()
        spare.wait()
        first_idx = idx_copy(0)
        first_idx.start()
        first_idx.wait()
        issue_gather(0)
        if n_blocks > 1:
            idx_copy(1).start()

    @pl.when(i + 1 < n_blocks)
    def _():
        idx_copy(i + 1).wait()
        issue_gather(i + 1)

    @pl.when(i + 2 < n_blocks)
    def _():
        idx_copy(i + 2).start()

    @pl.when(nv > 0)
    def _():
        @pl.when(first_ref[i] == 1)
        def _():
            wgu16[...] = wgu_ref[...].astype(BF16)
            wd16[...] = wd_ref[...].astype(BF16)

        wait_gather(i)
        xb = i & 1
        xt = pltpu.einshape("rsl->srl", xbuf[xb])
        x = jnp.concatenate([xt[s] for s in range(D_MODEL // LANES)], axis=-1).astype(BF16)
        n_chunks = D_EXPERT // MXU_DIM

        def gate_up(c):
            g0, u0 = c * MXU_DIM, D_EXPERT + c * MXU_DIM
            return (_dot(x, wgu16[:, g0:g0 + MXU_DIM]) + bgu_ref[:, g0:g0 + MXU_DIM],
                    _dot(x, wgu16[:, u0:u0 + MXU_DIM]) + bgu_ref[:, u0:u0 + MXU_DIM])

        y = jnp.broadcast_to(bd_ref[...], (bm, D_MODEL))
        nxt = gate_up(0)
        for c in range(n_chunks):
            gate, up = nxt
            if c + 1 < n_chunks:
                nxt = gate_up(c + 1)
            gate = jnp.minimum(gate, SWIGLU_LIMIT)
            up = jnp.clip(up, -SWIGLU_LIMIT, SWIGLU_LIMIT)
            act = (up + 1.0) * gate * jax.nn.sigmoid(SWIGLU_ALPHA * gate)
            y = y + _dot(act.astype(BF16), wd16[c * MXU_DIM:(c + 1) * MXU_DIM, :])

        @pl.when(i > 0)
        def _():
            wait_scatter(i - 1)

        ybuf[...] = pltpu.einshape("srl->rsl", jnp.stack([y[:, s * LANES:(s + 1) * LANES]
                                                          for s in range(D_MODEL // LANES)], axis=0))
        issue_scatter(i)

    @pl.when((nv == 0) & (i > 0))
    def _():
        wait_scatter(i - 1)

    @pl.when(i == n_blocks - 1)
    def _():
        wait_scatter(i)


def _experts(block_e, first, nval, row_slot3, wgu, bgu, wd, bd, x3, n_real):
    n_blocks = row_slot3.shape[0]
    bm = row_slot3.shape[1] // 2 * LANES
    n_slots = n_real + ROW_GROUP
    kern = functools.partial(_expert_kernel, n_blocks=n_blocks, n_real=n_real)
    grid_spec = pltpu.PrefetchScalarGridSpec(
        num_scalar_prefetch=3,
        grid=(n_blocks,),
        in_specs=[pl.BlockSpec(row_slot3.shape, lambda i, be, fi, na: (0, 0, 0)),
                  pl.BlockSpec((None, D_MODEL, 2 * D_EXPERT), lambda i, be, fi, na: (be[i], 0, 0)),
                  pl.BlockSpec((None, 1, 2 * D_EXPERT), lambda i, be, fi, na: (be[i], 0, 0)),
                  pl.BlockSpec((None, D_EXPERT, D_MODEL), lambda i, be, fi, na: (be[i], 0, 0)),
                  pl.BlockSpec((None, 1, D_MODEL), lambda i, be, fi, na: (be[i], 0, 0)),
                  pl.BlockSpec(memory_space=pl.ANY)],
        out_specs=pl.BlockSpec(memory_space=pl.ANY),
        scratch_shapes=[pltpu.SMEM((3,) + row_slot3.shape[1:], jnp.int32),
                        pltpu.VMEM((2, bm, 8, LANES), F32), pltpu.VMEM((bm, 8, LANES), F32),
                        pltpu.VMEM((ROW_GROUP, 8, LANES), F32),
                        pltpu.VMEM((D_MODEL, 2 * D_EXPERT), BF16), pltpu.VMEM((D_EXPERT, D_MODEL), BF16),
                        pltpu.SemaphoreType.DMA((2,)), pltpu.SemaphoreType.DMA((1,)), pltpu.SemaphoreType.DMA((3,))],
    )
    return pl.pallas_call(
        kern,
        grid_spec=grid_spec,
        out_shape=jax.ShapeDtypeStruct((n_slots, 8, LANES), F32),
        compiler_params=_params(("arbitrary",)),
        name="experts",
    )(block_e, first, nval, row_slot3, wgu, bgu, wd, bd, x3)


def _combine_kernel(xm_ref, tv_ref, y_ref, o_ref):
    tv = tv_ref[...]
    gate = [jnp.broadcast_to(tv[:, k:k + 1], (tv.shape[0], LANES)) for k in range(TOP_K)]
    y = pltpu.einshape("tjl->jtl", y_ref[...])
    for s in range(D_MODEL // LANES):
        acc = xm_ref[:, s * LANES:(s + 1) * LANES]
        for k in range(TOP_K):
            acc = acc + gate[k] * y[k * 8 + s]
        o_ref[:, s * LANES:(s + 1) * LANES] = acc


def _combine(xm, tv, yslots, tok_off, n):
    tm = math.gcd(TOKEN_TILE, math.gcd(n, tok_off)) if tok_off else min(TOKEN_TILE, n)
    off = tok_off // tm
    y4 = yslots.reshape(yslots.shape[0] // TOP_K, TOP_K * 8, LANES)
    return pl.pallas_call(
        _combine_kernel,
        grid=(n // tm,),
        in_specs=[pl.BlockSpec((tm, D_MODEL), lambda i: (i + off, 0)),
                  pl.BlockSpec((tm, LANES), lambda i: (i + off, 0)),
                  pl.BlockSpec((tm, TOP_K * 8, LANES), lambda i: (i + off, 0, 0))],
        out_specs=pl.BlockSpec((tm, D_MODEL), lambda i: (i, 0)),
        out_shape=jax.ShapeDtypeStruct((n, D_MODEL), F32),
        compiler_params=_params(("parallel",)),
        name="combine",
    )(xm, tv, y4)


def _rope_tables(pos, rows):
    pos = pos.astype(F32)[:, None]
    lane = jnp.arange(LANES)

    def ang(half):
        inv = jnp.power(ROPE_THETA, -jnp.arange(half, dtype=F32) / half)
        return pos * inv[None, :]

    a64 = ang(32)[:, lane % 32]
    first = (lane % 64) < 32
    c64 = jnp.cos(a64)
    sa64 = jnp.where(first, -jnp.sin(a64), 0.0)
    sb64 = jnp.where(first, 0.0, jnp.sin(a64))
    a32 = ang(16)[:, lane % 16]
    in_a = (lane >= 64) & (lane < 80)
    in_b = (lane >= 80) & (lane < 96)
    c32 = jnp.where(in_a | in_b, jnp.cos(a32), 1.0)
    sa32 = jnp.where(in_a, -jnp.sin(a32), 0.0)
    sb32 = jnp.where(in_b, jnp.sin(a32), 0.0)
    tabs = [c64, sa64, sb64, c32, sa32, sb32]
    reps = rows // pos.shape[0]
    return [jnp.tile(t, (reps, 1)) if reps > 1 else t for t in tabs]


def _tied_pad(g, scale):
    blk = jnp.concatenate([g[:MLA_NOPE], g[MLA_NOPE:], g[MLA_NOPE:], jnp.zeros((32,), F32)]) * scale
    return jnp.tile(blk, MLA_HEADS)[None, :]


def _layer_weights(lp):
    w_in = lp["w_in"]
    kr_blk = jnp.zeros((D_MODEL, LANES), F32).at[:, 64:96].set(w_in[:, _OFF_KR:_OFF_MQ])
    wa = jnp.concatenate([w_in[:, _OFF_Q:_OFF_KR], kr_blk, w_in[:, _OFF_MQ:_OFF_G]], axis=1).astype(BF16)
    wuq = jnp.pad(lp["w_mla_uq"], ((0, 0), (0, 0), (0, LANES - MLA_QK))).reshape(MLA_Q_LORA, MLA_HEADS * LANES).astype(BF16)
    wuk_pad = jnp.pad(lp["w_mla_uk"], ((0, 0), (0, 0), (0, LANES - MLA_NOPE))).reshape(MLA_KV_LORA, MLA_HEADS * LANES)
    lane = jnp.arange(LANES)
    rope_eye = jnp.where(((lane >= 64) & (lane < 96))[:, None], jnp.eye(LANES, dtype=F32), 0.0)
    wk = jnp.concatenate([wuk_pad, jnp.tile(rope_eye, (1, MLA_HEADS))], axis=0).astype(BF16)
    g64 = jnp.kron(jnp.eye(4, dtype=F32), jnp.ones((64, 64), F32)).astype(BF16)
    g128 = jnp.kron(jnp.eye(2, dtype=F32), jnp.ones((128, 128), F32)).astype(BF16)
    gk_pad = _tied_pad(lp["mla_k_norm"], 1.0)
    inproj_w = [lp["norm_mix"][None, :], wa, wuq, wk, g64, g128,
                jnp.tile(lp["diff_q_norm"], 8)[None, :] * DIFF_SCALE, jnp.tile(lp["diff_k_norm"], 4)[None, :],
                lp["mla_q_a_norm"][None, :], lp["mla_kv_a_norm"][None, :],
                _tied_pad(lp["mla_q_norm"], MLA_SCALE), gk_pad,
                jnp.tile(lp["mem_q_norm"], MEM_HEADS)[None, :] * MEM_SCALE]
    wuv = lp["w_mla_uv"]
    wuv2 = jnp.einsum("rhd,hg->hrgd", wuv, jnp.eye(MLA_HEADS, dtype=F32)).reshape(MLA_HEADS, MLA_KV_LORA, 512).astype(BF16)
    wuk_t = jnp.transpose(lp["w_mla_uk"], (1, 2, 0))
    aq = jnp.zeros((MLA_HEADS, LANES, 2 * LANES), F32).at[:, 0:MLA_NOPE, 0:LANES].set(wuk_t)
    aq = aq.at[:, :, LANES:].add(rope_eye[None]).astype(BF16)
    wukt = wuk_t.reshape(MLA_HEADS * MLA_NOPE, MLA_KV_LORA).astype(BF16)
    wr = jnp.pad(lp["w_router"], ((0, 0), (0, LANES - N_EXPERTS)))
    wrh = wr.astype(BF16)
    wrl = (wr - wrh.astype(F32)).astype(BF16)
    merge_w = [lp["norm_mix"][None, :], w_in[:, _OFF_G:].astype(BF16), lp["b_gate"][None, :],
               lp["w_branch"].reshape(3, 512, D_MODEL).astype(BF16), lp["w_out"].astype(BF16),
               lp["norm_ffn"][None, :], wrh, wrl, jnp.pad(lp["b_router"], (0, LANES - N_EXPERTS))[None, :]]
    return dict(inproj=inproj_w, g128=g128, wuv2=wuv2, aq=aq, wukt=wukt, gk_pad=gk_pad, merge=merge_w)


def _moe_plan(ti, bm):
    n = ti.shape[0]
    a = n * TOP_K
    flat_e = ti[:, :TOP_K].reshape(a)
    onehot = (flat_e[:, None] == jnp.arange(N_EXPERTS, dtype=jnp.int32)[None, :]).astype(jnp.int32)
    csum = jnp.cumsum(onehot, axis=0)
    counts = csum[-1]
    rank = jnp.sum(onehot * csum, axis=1) - 1
    padded = (counts + bm - 1) // bm * bm
    pend = jnp.cumsum(padded)
    pstart = pend - padded
    dest = jnp.sum(onehot * pstart[None, :], axis=1) + rank
    n_rows = (a + N_EXPERTS * (bm - 1) + bm - 1) // bm * bm
    n_blocks = n_rows // bm
    spare = a + (jnp.arange(n_rows, dtype=jnp.int32) % ROW_GROUP)
    row_slot = spare.at[dest].set(jnp.arange(a, dtype=jnp.int32), unique_indices=True)
    blk_start = jnp.arange(n_blocks, dtype=jnp.int32) * bm
    block_e = jnp.minimum(jnp.sum((blk_start[:, None] >= pend[None, :]).astype(jnp.int32), axis=1), N_EXPERTS - 1)
    first = jnp.concatenate([jnp.ones((1,), jnp.int32), (block_e[1:] != block_e[:-1]).astype(jnp.int32)])
    eh = (block_e[:, None] == jnp.arange(N_EXPERTS, dtype=jnp.int32)[None, :]).astype(jnp.int32)
    valid_end = jnp.sum(eh * (pstart + counts)[None, :], axis=1)
    nval = jnp.where(blk_start < pend[-1], jnp.clip(valid_end - blk_start, 0, bm), 0).astype(jnp.int32)
    row_tok = jnp.minimum(lax.shift_right_logical(row_slot, 2), n - 1)
    row_idx = jnp.concatenate([row_slot.reshape(n_blocks, bm // LANES, LANES),
                               row_tok.reshape(n_blocks, bm // LANES, LANES)], axis=1)
    return block_e, first, nval, row_idx, a


def _qbd(dq_s, b, nq):
    q = dq_s.reshape(b, nq, 2, 2, 2, DIFF_HEAD_DIM)
    q = jnp.transpose(q, (0, 2, 4, 3, 1, 5))
    eye = jnp.eye(4, dtype=q.dtype).reshape(2, 2, 2, 2)
    out = jnp.einsum("bncgqd,ncmk->bncgqmkd", q, eye)
    return out.reshape(b, 64, 256)


def kernel(x_prompt, x_sample, mem_prompt, cache_diff_k, cache_diff_v, cache_mla_ckv, cache_mla_krope, cache_mem_k, cache_mem_v, page_table, norm_mix, norm_mem, w_in, b_gate, diff_q_norm, diff_k_norm, diff_lambda, diff_subln, mla_q_a_norm, w_mla_uq, mla_kv_a_norm, w_mla_uk, w_mla_uv, mla_q_norm, mla_k_norm, w_mem_kv, mem_q_norm, mem_k_norm, w_branch, w_out, norm_ffn, w_router, b_router, w_gate_up, b_gate_up, w_down, b_down):
    depth = w_in.shape[0]
    bp, t, _ = x_prompt.shape
    bs, nq, _ = x_sample.shape
    n_pool, n_pages = cache_diff_k.shape[1], page_table.shape[1]
    past_len = n_pages * PAGE
    n_p, n_s = bp * t, bs * nq

    tm_p, tm_s = min(TOKEN_TILE, n_p), min(TOKEN_TILE, n_s)
    tabs_p = _rope_tables(jnp.arange(t, dtype=jnp.int32), t)
    tabs_s = _rope_tables(past_len + jnp.arange(nq, dtype=jnp.int32), tm_s)
    del tm_p

    kc = jnp.transpose(cache_diff_k, (0, 1, 3, 4, 5, 2)).reshape(depth * n_pool, 256, PAGE)
    vc = cache_diff_v.reshape(depth * n_pool, 2 * PAGE, DIFF_V_DIM)
    cc = cache_mla_ckv.reshape(depth * n_pool, PAGE, MLA_KV_LORA)
    rc = jnp.transpose(cache_mla_krope, (0, 1, 3, 2)).reshape(depth * n_pool, MLA_ROPE, PAGE)
    mkc = cache_mem_k.reshape(depth * bs, N_MEM * MEM_HEADS, MEM_HEAD_DIM)
    mvc = cache_mem_v.reshape(depth * bs, N_MEM * MEM_HEADS, MEM_HEAD_DIM)

    xp = x_prompt.reshape(n_p, D_MODEL)
    xs = x_sample.reshape(n_s, D_MODEL)
    outs = [[] for _ in range(10)]
    for layer in range(depth):
        lp = dict(norm_mix=norm_mix[layer], w_in=w_in[layer], b_gate=b_gate[layer], diff_q_norm=diff_q_norm[layer],
                  diff_k_norm=diff_k_norm[layer], mla_q_a_norm=mla_q_a_norm[layer], w_mla_uq=w_mla_uq[layer],
                  mla_kv_a_norm=mla_kv_a_norm[layer], w_mla_uk=w_mla_uk[layer], w_mla_uv=w_mla_uv[layer],
                  mla_q_norm=mla_q_norm[layer], mla_k_norm=mla_k_norm[layer], mem_q_norm=mem_q_norm[layer],
                  w_branch=w_branch[layer], w_out=w_out[layer], norm_ffn=norm_ffn[layer],
                  w_router=w_router[layer], b_router=b_router[layer])
        w = _layer_weights(lp)
        lam_init = 0.8 - 0.6 * math.exp(-0.3 * layer)
        lamp = diff_lambda[layer].astype(F32)
        lam = (jnp.exp(jnp.sum(lamp[0] * lamp[1])) - jnp.exp(jnp.sum(lamp[2] * lamp[3])) + lam_init).reshape(1)
        gs = (diff_subln[layer] * (1.0 - lam_init))[None, :]
        pt_flat = (page_table + layer * n_pool).reshape(-1).astype(jnp.int32)

        (dq, _, dk16, dv32, dv16, mq, ckv32, ckv16, kr32, kmla, memq, dvt, ckvt, dkt) = _inproj(xp, tabs_p, w["inproj"], t)
        p_dk = jnp.transpose(dkt.reshape(bp, 2, 2, DIFF_HEAD_DIM, t), (0, 4, 1, 2, 3))
        o_diff = _diff_prompt(lam, dq.reshape(bp, t, 512), dk16.reshape(bp, t, 256), dvt, gs)
        o_mla = _mla_prompt(mq.reshape(bp, t, 1024), kmla.reshape(bp, t, 1024), ckvt, w["wuv2"])
        mk32, mk16, mv32, mv16 = _memkv(mem_prompt.reshape(bp * N_MEM, D_MODEL), norm_mem[layer][None, :],
                                        w_mem_kv[layer].astype(BF16), w["g128"],
                                        jnp.tile(mem_k_norm[layer], MEM_HEADS)[None, :])
        o_mem = _mem_prompt(memq.reshape(bp, t, 512), mk16.reshape(bp, N_MEM, 512), mv16.reshape(bp, N_MEM, 512))
        group_p = (xp, o_diff.reshape(n_p, 512), o_mla.reshape(n_p, 512), o_mem.reshape(n_p, 512))
        for lst, val in zip(outs[:6], (p_dk, dv32.reshape(bp, t, 2, 128),
                                       ckv32.reshape(bp, t, 128), kr32.reshape(bp, t, 32),
                                       mk32.reshape(bp, N_MEM, 4, 128), mv32.reshape(bp, N_MEM, 4, 128))):
            lst.append(val)

        (dq, dk32, dk16, dv32, dv16, mq, ckv32, ckv16, kr32, kmla, memq, _, _, _) = _inproj(xs, tabs_s, w["inproj"],
                                                                                            min(TOKEN_TILE, n_s))
        o_diff, o_mla = _decode(pt_flat, lam, _qbd(dq, bs, nq), dk16.reshape(bs, nq, 256), dv16.reshape(bs, nq, 256), gs,
                                mq.reshape(bs, nq, 1024), kmla.reshape(bs, nq, 1024), ckv16.reshape(bs, nq, LANES),
                                w["gk_pad"], w["aq"], w["wukt"], w["wuv2"], kc, vc, cc, rc, n_pages)
        o_mem = _mem_sample(memq.reshape(bs, nq, 512), mkc, mvc, layer * bs)
        group_s = (xs, o_diff.reshape(n_s, 512), o_mla.reshape(n_s, 512), o_mem.reshape(n_s, 512))
        for lst, val in zip(outs[6:], (dk32.reshape(bs, nq, 2, 2, 64), dv32.reshape(bs, nq, 2, 128),
                                       ckv32.reshape(bs, nq, 128), kr32.reshape(bs, nq, 32))):
            lst.append(val)

        xm, h2, tv, ti = _merge(group_p, group_s, w["merge"])
        block_e, first, nval, row_slot3, n_slots = _moe_plan(ti, MOE_ROWS)
        yslots = _experts(block_e + layer * N_EXPERTS, first, nval, row_slot3,
                          w_gate_up.reshape(depth * N_EXPERTS, D_MODEL, 2 * D_EXPERT),
                          b_gate_up.reshape(depth * N_EXPERTS, 1, 2 * D_EXPERT),
                          w_down.reshape(depth * N_EXPERTS, D_EXPERT, D_MODEL),
                          b_down.reshape(depth * N_EXPERTS, 1, D_MODEL), h2, n_slots)
        xp = _combine(xm, tv, yslots, 0, n_p)
        xs = _combine(xm, tv, yslots, n_p, n_s)

    stack = lambda lst: jnp.stack(lst)
    return (xp.reshape(bp, t, D_MODEL), xs.reshape(bs, nq, D_MODEL)) + tuple(stack(o) for o in outs)
```
